```python
import jax
import jax.numpy as jnp
from jax import lax
import numpy as np

D_MODEL = 2048
BATCH = 2
SEQ = 16384
DEPTH = 2

GRID_W = 64
CTX_LEN = 256
F32 = jnp.float32
EPS = 1e-6
ROPE_BASE = 10000.0
N_MOD = 6

SWA_HEADS = 8
SWA_KV_HEADS = 2
SWA_GROUP = SWA_HEADS // SWA_KV_HEADS
SWA_HEAD_DIM = 64
SWA_WINDOW = 128
SWA_BLOCK = 128

MLA_HEADS = 4
MLA_Q_RANK = 384
MLA_KV_RANK = 128
MLA_NOPE_DIM = 128
MLA_ROPE_DIM = 64
MLA_V_DIM = 128
MLA_BLOCK = 128

GLA_HEADS = 4
GLA_DK = 64
GLA_DV = 128
GLA_DECAY_RANK = 16
GLA_TAU = 16.0
GLA_CHUNK = 64

SMLP_GROUPS = 4
SMLP_CHUNK = 128
SMLP_GROUP_DIM = 128
SMLP_WIDTH = SMLP_GROUPS * SMLP_GROUP_DIM

N_BRANCHES = 4
BRANCH_WIDTH = 512

N_EXPERTS = 32
TOP_K = 4
D_EXPERT = 1024
SWIGLU_LIMIT = 7.0
SWIGLU_ALPHA = 1.702
MOE_BLOCK = 128

IN_SIZES = (
    SWA_HEADS * SWA_HEAD_DIM,
    SWA_KV_HEADS * SWA_HEAD_DIM,
    SWA_KV_HEADS * SWA_HEAD_DIM,
    MLA_Q_RANK,
    MLA_KV_RANK,
    MLA_ROPE_DIM,
    GLA_HEADS * GLA_DK,
    GLA_HEADS * GLA_DK,
    GLA_HEADS * GLA_DV,
    2 * GLA_DECAY_RANK,
    GLA_HEADS * GLA_DV,
    SMLP_WIDTH,
    SMLP_WIDTH,
)
D_IN = sum(IN_SIZES)

kernel_name = "hybrid_dit_parallel_mixers_moe"


def rmsnorm(x, g):
    xf = x.astype(F32)
    y = xf * lax.rsqrt(jnp.mean(xf * xf, axis=-1, keepdims=True) + EPS)
    return (y * g.astype(F32)).astype(x.dtype)


def modulate(xn, shift, scale):
    return xn * (1.0 + scale) + shift


def split_cols(p):
    return jnp.split(p, np.cumsum(IN_SIZES)[:-1].tolist(), axis=-1)


def grid_rope_tables(n_tok, d_rot):
    n_rows = n_tok // GRID_W
    rows = jnp.repeat(jnp.arange(n_rows, dtype=F32), GRID_W)
    cols = jnp.tile(jnp.arange(GRID_W, dtype=F32), n_rows)
    n_freq = d_rot // 4
    inv_freq = ROPE_BASE ** (-jnp.arange(n_freq, dtype=F32) / n_freq)
    ang = jnp.concatenate([rows[:, None] * inv_freq, cols[:, None] * inv_freq], axis=-1)
    return jnp.cos(ang), jnp.sin(ang)


def apply_rope(x, cos, sin):
    half = x.shape[-1] // 2
    shape = (1, x.shape[1]) + (1,) * (x.ndim - 3) + (half,)
    c = cos.reshape(shape)
    s = sin.reshape(shape)
    xf = x.astype(F32)
    x1, x2 = xf[..., :half], xf[..., half:]
    return jnp.concatenate([x1 * c - x2 * s, x2 * c + x1 * s], axis=-1).astype(x.dtype)


def softmax_with_sink(logits, sink):
    sink_col = jnp.broadcast_to(sink.astype(F32).reshape(SWA_KV_HEADS, SWA_GROUP, 1, 1),
                                logits.shape[:-1] + (1,))
    p = jax.nn.softmax(jnp.concatenate([logits, sink_col], axis=-1), axis=-1)
    return p[..., :-1]


def swa_latent(q, k, v, kc, vc, sink):
    B, S = q.shape[:2]
    nb = S // SWA_BLOCK
    scale = SWA_HEAD_DIM ** -0.5
    qb = q.reshape(B, nb, SWA_BLOCK, SWA_KV_HEADS, SWA_GROUP, SWA_HEAD_DIM)
    pad = ((0, 0), (SWA_BLOCK, SWA_BLOCK), (0, 0), (0, 0))
    kp = jnp.pad(k, pad).reshape(B, nb + 2, SWA_BLOCK, SWA_KV_HEADS, SWA_HEAD_DIM)
    vp = jnp.pad(v, pad).reshape(B, nb + 2, SWA_BLOCK, SWA_KV_HEADS, SWA_HEAD_DIM)
    kw = jnp.concatenate([kp[:, :-2], kp[:, 1:-1], kp[:, 2:]], axis=2)
    vw = jnp.concatenate([vp[:, :-2], vp[:, 1:-1], vp[:, 2:]], axis=2)
    s_win = jnp.einsum('bnqhgd,bnkhd->bnhgqk', qb, kw, preferred_element_type=F32) * scale
    s_ctx = jnp.einsum('bnqhgd,bchd->bnhgqc', qb, kc, preferred_element_type=F32) * scale
    blk = jnp.arange(nb)[:, None, None]
    q_pos = blk * SWA_BLOCK + jnp.arange(SWA_BLOCK)[None, :, None]
    k_pos = (blk - 1) * SWA_BLOCK + jnp.arange(3 * SWA_BLOCK)[None, None, :]
    valid = (jnp.abs(q_pos - k_pos) <= SWA_WINDOW) & (k_pos >= 0) & (k_pos < S)
    s_win = jnp.where(valid[None, :, None, None], s_win, -jnp.inf)
    p = softmax_with_sink(jnp.concatenate([s_win, s_ctx], axis=-1), sink).astype(v.dtype)
    n_win = 3 * SWA_BLOCK
    o = (jnp.einsum('bnhgqk,bnkhd->bnqhgd', p[..., :n_win], vw)
         + jnp.einsum('bnhgqc,bchd->bnqhgd', p[..., n_win:], vc))
    return o.reshape(B, S, SWA_HEADS * SWA_HEAD_DIM)


def swa_context(qc, kc, vc, sink):
    B, L = qc.shape[:2]
    qg = qc.reshape(B, L, SWA_KV_HEADS, SWA_GROUP, SWA_HEAD_DIM)
    s = jnp.einsum('bqhgd,bkhd->bhgqk', qg, kc, preferred_element_type=F32) * SWA_HEAD_DIM ** -0.5
    p = softmax_with_sink(s, sink).astype(vc.dtype)
    return jnp.einsum('bhgqk,bkhd->bqhgd', p, vc).reshape(B, L, SWA_HEADS * SWA_HEAD_DIM)


def mla_queries(cq, q_norm, w_uq):
    B, n, _ = cq.shape
    q = (rmsnorm(cq, q_norm) @ w_uq).reshape(B, n, MLA_HEADS, MLA_NOPE_DIM + MLA_ROPE_DIM)
    return q[..., :MLA_NOPE_DIM], q[..., MLA_NOPE_DIM:]


def mla_keys_values(ckv, kv_norm, w_ukv):
    B, n, _ = ckv.shape
    kv = (rmsnorm(ckv, kv_norm) @ w_ukv).reshape(B, n, MLA_HEADS, MLA_NOPE_DIM + MLA_V_DIM)
    return kv[..., :MLA_NOPE_DIM], kv[..., MLA_NOPE_DIM:]


def mla_attend(qn, qr, kn, kr, v):
    scale = (MLA_NOPE_DIM + MLA_ROPE_DIM) ** -0.5
    s = (jnp.einsum('bqhd,bkhd->bhqk', qn, kn, preferred_element_type=F32)
         + jnp.einsum('bqhr,bkr->bhqk', qr, kr, preferred_element_type=F32)) * scale
    p = jax.nn.softmax(s, axis=-1).astype(v.dtype)
    return jnp.einsum('bhqk,bkhd->bqhd', p, v)


def mla_latent(qn, qr, kn, kr, v, kn_c, kr_c, v_c):
    B, S = qn.shape[:2]
    nb = S // MLA_BLOCK
    kn_all = jnp.concatenate([kn, kn_c], axis=1)
    kr_all = jnp.concatenate([kr, kr_c], axis=1)
    v_all = jnp.concatenate([v, v_c], axis=1)

    def to_blocks(t):
        return jnp.moveaxis(t.reshape((B, nb, MLA_BLOCK) + t.shape[2:]), 1, 0)

    out = lax.map(lambda qb: mla_attend(qb[0], qb[1], kn_all, kr_all, v_all),
                  (to_blocks(qn), to_blocks(qr)))
    return jnp.moveaxis(out, 0, 1).reshape(B, S, MLA_HEADS * MLA_V_DIM)


def gla_log_decay(lr, w, b):
    B, n, _ = lr.shape
    z = (lr @ w).astype(F32) + b.astype(F32)
    return (jax.nn.log_sigmoid(z) / GLA_TAU).reshape(B, n, GLA_HEADS, GLA_DK)


def gla_chunked(q, k, v, log_a, s0, need_out):
    B, n, H, dk = k.shape
    dv = v.shape[-1]
    nc = n // GLA_CHUNK
    kf = k.astype(F32).reshape(B, nc, GLA_CHUNK, H, dk)
    vf = v.astype(F32).reshape(B, nc, GLA_CHUNK, H, dv)
    b = jnp.cumsum(log_a.reshape(B, nc, GLA_CHUNK, H, dk), axis=2)
    b_end = b[:, :, -1]
    kv_chunk = jnp.einsum('bnshd,bnshv->bnhdv', kf * jnp.exp(b_end[:, :, None] - b), vf)

    def step(state, inp):
        dec, kv = inp
        return dec[..., None] * state + kv, state

    s_fin, s_before = lax.scan(step, s0, (jnp.moveaxis(jnp.exp(b_end), 1, 0),
                                          jnp.moveaxis(kv_chunk, 1, 0)))
    if not need_out:
        return None, s_fin
    qf = q.astype(F32).reshape(B, nc, GLA_CHUNK, H, dk) * dk ** -0.5
    q_dec = qf * jnp.exp(b)
    scores = jnp.einsum('bnthd,bnshd->bnhts', q_dec, kf * jnp.exp(-b))
    tril = jnp.tril(jnp.ones((GLA_CHUNK, GLA_CHUNK), dtype=bool))
    scores = jnp.where(tril, scores, 0.0)
    o = (jnp.einsum('bnhts,bnshv->bnthv', scores, vf)
         + jnp.einsum('bnthd,nbhdv->bnthv', q_dec, s_before))
    return o.reshape(B, n, H, dv).astype(v.dtype), s_fin


def gla_output(o, r, out_norm):
    B, n = o.shape[:2]
    gated = rmsnorm(o, out_norm) * jax.nn.silu(r).reshape(B, n, GLA_HEADS, GLA_DV)
    return gated.reshape(B, n, GLA_HEADS * GLA_DV)


def gla_branch(q, k, v, lr, r, q_c, k_c, v_c, lr_c, r_c, w_decay, b_decay, out_norm, ctx_out):
    B, S = q.shape[:2]
    L = q_c.shape[1]
    lat = (q.reshape(B, S, GLA_HEADS, GLA_DK), k.reshape(B, S, GLA_HEADS, GLA_DK),
           v.reshape(B, S, GLA_HEADS, GLA_DV))
    con = (q_c.reshape(B, L, GLA_HEADS, GLA_DK), k_c.reshape(B, L, GLA_HEADS, GLA_DK),
           v_c.reshape(B, L, GLA_HEADS, GLA_DV))
    s0 = jnp.zeros((B, GLA_HEADS, GLA_DK, GLA_DV), F32)
    outs, outs_c = [], []
    for direction in range(2):
        cols = slice(direction * GLA_DECAY_RANK, (direction + 1) * GLA_DECAY_RANK)
        seq = lat + (gla_log_decay(lr[..., cols], w_decay[direction], b_decay[direction]),)
        seq_c = con + (gla_log_decay(lr_c[..., cols], w_decay[direction], b_decay[direction]),)
        if direction == 1:
            seq = tuple(jnp.flip(t, axis=1) for t in seq)
            seq_c = tuple(jnp.flip(t, axis=1) for t in seq_c)
        o_c_dir, s_ctx = gla_chunked(*seq_c, s0, ctx_out)
        o_dir, _ = gla_chunked(*seq, s_ctx, True)
        if direction == 1:
            o_dir = jnp.flip(o_dir, axis=1)
            if ctx_out:
                o_c_dir = jnp.flip(o_c_dir, axis=1)
        outs.append(o_dir)
        outs_c.append(o_c_dir)
    y = gla_output(outs[0] + outs[1], r, out_norm)
    yc = gla_output(outs_c[0] + outs_c[1], r_c, out_norm) if ctx_out else None
    return y, yc


def spatial_gating(u, v, v_norm, w_s, b_s):
    B, n, _ = u.shape
    nc = n // SMLP_CHUNK
    vb = rmsnorm(jax.nn.gelu(v), v_norm).reshape(B, nc, SMLP_CHUNK, SMLP_GROUPS, SMLP_GROUP_DIM)
    mixed = jnp.einsum('gpq,bnqgc->bnpgc', w_s, vb) + b_s.T[None, None, :, :, None]
    return jax.nn.gelu(u) * mixed.reshape(B, n, SMLP_WIDTH)


def merge_branches(h, branches, w_branch, w_gate, b_gate, w_out):
    y = sum(jax.nn.sigmoid(h @ w_gate[i] + b_gate[i]) * (o @ w_branch[i])
            for i, o in enumerate(branches))
    return y @ w_out


def hybrid_mixer(h, hc, ctx_out, w_in, attn_sink, mla_q_norm, mla_w_uq, mla_kv_norm, mla_w_ukv,
                 gla_w_decay, gla_b_decay, gla_out_norm, smlp_v_norm, smlp_w_spatial,
                 smlp_b_spatial, w_branch, w_gate, b_gate, w_out):
    B, S, _ = h.shape
    L = hc.shape[1]
    (a_q, a_k, a_v, m_cq, m_ckv, m_kr, g_q, g_k, g_v, g_lr, g_r, s_u, s_v) = split_cols(h @ w_in)
    (a_qc, a_kc, a_vc, m_cqc, m_ckvc, m_krc, g_qc, g_kc, g_vc, g_lrc, g_rc, s_uc, s_vc) = \
        split_cols(hc @ w_in)

    cos_a, sin_a = grid_rope_tables(S, SWA_HEAD_DIM)
    q_a = apply_rope(a_q.reshape(B, S, SWA_HEADS, SWA_HEAD_DIM), cos_a, sin_a)
    k_a = apply_rope(a_k.reshape(B, S, SWA_KV_HEADS, SWA_HEAD_DIM), cos_a, sin_a)
    v_a = a_v.reshape(B, S, SWA_KV_HEADS, SWA_HEAD_DIM)
    k_ac = a_kc.reshape(B, L, SWA_KV_HEADS, SWA_HEAD_DIM)
    v_ac = a_vc.reshape(B, L, SWA_KV_HEADS, SWA_HEAD_DIM)
    o_a = swa_latent(q_a, k_a, v_a, k_ac, v_ac, attn_sink)

    cos_m, sin_m = grid_rope_tables(S, MLA_ROPE_DIM)
    qn_m, qr_m = mla_queries(m_cq, mla_q_norm, mla_w_uq)
    qr_m = apply_rope(qr_m, cos_m, sin_m)
    kn_m, v_m = mla_keys_values(m_ckv, mla_kv_norm, mla_w_ukv)
    kr_m = apply_rope(m_kr, cos_m, sin_m)
    kn_mc, v_mc = mla_keys_values(m_ckvc, mla_kv_norm, mla_w_ukv)
    o_m = mla_latent(qn_m, qr_m, kn_m, kr_m, v_m, kn_mc, m_krc, v_mc)

    o_g, oc_g = gla_branch(g_q, g_k, g_v, g_lr, g_r, g_qc, g_kc, g_vc, g_lrc, g_rc,
                           gla_w_decay, gla_b_decay, gla_out_norm, ctx_out)

    o_s = spatial_gating(s_u, s_v, smlp_v_norm, smlp_w_spatial, smlp_b_spatial)

    y = merge_branches(h, (o_a, o_m, o_g, o_s), w_branch, w_gate, b_gate, w_out)
    if not ctx_out:
        return y, None
    oc_a = swa_context(a_qc.reshape(B, L, SWA_HEADS, SWA_HEAD_DIM), k_ac, v_ac, attn_sink)
    qn_mc, qr_mc = mla_queries(m_cqc, mla_q_norm, mla_w_uq)
    oc_m = mla_attend(qn_mc, qr_mc, kn_mc, m_krc, v_mc).reshape(B, L, MLA_HEADS * MLA_V_DIM)
    oc_s = spatial_gating(s_uc, s_vc, smlp_v_norm, smlp_w_spatial, smlp_b_spatial)
    yc = merge_branches(hc, (oc_a, oc_m, oc_g, oc_s), w_branch, w_gate, b_gate, w_out)
    return y, yc


def expert_swiglu(xb, w1, b1, w2, b2):
    z = xb @ w1 + b1
    glu = jnp.minimum(z[..., :D_EXPERT], SWIGLU_LIMIT)
    lin = jnp.clip(z[..., D_EXPERT:], -SWIGLU_LIMIT, SWIGLU_LIMIT)
    act = glu * jax.nn.sigmoid(SWIGLU_ALPHA * glu) * (lin + 1.0)
    return act @ w2 + b2


def moe_ffn(tok, router_w, router_b, w1, b1, w2, b2):
    T, D = tok.shape
    logits = jnp.matmul(tok, router_w, preferred_element_type=F32) + router_b.astype(F32)
    top_val, top_idx = lax.top_k(logits, TOP_K)
    top_w = jax.nn.softmax(top_val, axis=-1)
    n_assign = T * TOP_K
    flat_e = top_idx.reshape(-1)
    flat_tok = jnp.repeat(jnp.arange(T, dtype=jnp.int32), TOP_K)
    flat_w = top_w.reshape(-1)
    order = jnp.argsort(flat_e)
    e_sorted = flat_e[order]
    counts = jnp.bincount(flat_e, length=N_EXPERTS)
    padded = (counts + MOE_BLOCK - 1) // MOE_BLOCK * MOE_BLOCK
    start = jnp.cumsum(counts) - counts
    p_end = jnp.cumsum(padded)
    p_start = p_end - padded
    dest = p_start[e_sorted] + jnp.arange(n_assign, dtype=jnp.int32) - start[e_sorted]
    n_blocks = -(-n_assign // MOE_BLOCK) + N_EXPERTS
    n_rows = n_blocks * MOE_BLOCK
    row_tok = jnp.full((n_rows,), T, jnp.int32).at[dest].set(flat_tok[order])
    row_w = jnp.zeros((n_rows,), F32).at[dest].set(flat_w[order])
    block_e = jnp.minimum(
        jnp.searchsorted(p_end, jnp.arange(n_blocks, dtype=jnp.int32) * MOE_BLOCK, side="right"),
        N_EXPERTS - 1)
    tok_pad = jnp.concatenate([tok, jnp.zeros((1, D), tok.dtype)], axis=0)

    def run_block(args):
        rt, rw, e = args
        y = expert_swiglu(tok_pad[rt], w1[e], b1[e], w2[e], b2[e])
        return y.astype(F32) * rw[:, None]

    ys = lax.map(run_block, (row_tok.reshape(n_blocks, MOE_BLOCK),
                             row_w.reshape(n_blocks, MOE_BLOCK), block_e))
    out = jax.ops.segment_sum(ys.reshape(n_rows, D), row_tok, num_segments=T + 1)
    return out[:T].astype(tok.dtype)


def setup_inputs(seed: int = 0) -> dict:
    key = jax.random.key(seed)
    keys = list(jax.random.split(key, 29))

    def nrm(shape, scale):
        return jax.random.normal(keys.pop(), shape, F32) * scale

    def gain(shape):
        return 1.0 + nrm(shape, 0.02)

    D, L = D_MODEL, DEPTH
    return {
        "x": nrm((BATCH, SEQ, D), 1.0),
        "c": nrm((BATCH, D), 1.0),
        "ctx": nrm((BATCH, CTX_LEN, D), 1.0),
        "c_ctx": nrm((D,), 1.0),
        "ada_w": nrm((L, D, N_MOD * D), 0.5 * D ** -0.5),
        "ada_b": nrm((L, N_MOD * D), 0.02),
        "norm_g": gain((L, 4, D)),
        "w_in": nrm((L, D, D_IN), D ** -0.5),
        "attn_sink": nrm((L, SWA_HEADS), 0.5),
        "mla_q_norm": gain((L, MLA_Q_RANK)),
        "mla_w_uq": nrm((L, MLA_Q_RANK, MLA_HEADS * (MLA_NOPE_DIM + MLA_ROPE_DIM)), MLA_Q_RANK ** -0.5),
        "mla_kv_norm": gain((L, MLA_KV_RANK)),
        "mla_w_ukv": nrm((L, MLA_KV_RANK, MLA_HEADS * (MLA_NOPE_DIM + MLA_V_DIM)), MLA_KV_RANK ** -0.5),
        "gla_w_decay": nrm((L, 2, GLA_DECAY_RANK, GLA_HEADS * GLA_DK), GLA_DECAY_RANK ** -0.5),
        "gla_b_decay": nrm((L, 2, GLA_HEADS * GLA_DK), 0.02),
        "gla_out_norm": gain((L, GLA_DV)),
        "smlp_v_norm": gain((L, SMLP_WIDTH)),
        "smlp_w_spatial": nrm((L, SMLP_GROUPS, SMLP_CHUNK, SMLP_CHUNK), SMLP_CHUNK ** -0.5),
        "smlp_b_spatial": gain((L, SMLP_GROUPS, SMLP_CHUNK)),
        "w_branch": nrm((L, N_BRANCHES, BRANCH_WIDTH, D), BRANCH_WIDTH ** -0.5),
        "w_gate": nrm((L, N_BRANCHES, D, D), D ** -0.5),
        "b_gate": nrm((L, N_BRANCHES, D), 0.02),
        "w_out": nrm((L, D, D), D ** -0.5),
        "router_w": nrm((L, D, N_EXPERTS), D ** -0.5),
        "router_b": nrm((L, N_EXPERTS), 0.01),
        "expert_w1": nrm((L, N_EXPERTS, D, 2 * D_EXPERT), D ** -0.5),
        "expert_b1": nrm((L, N_EXPERTS, 2 * D_EXPERT), 0.02),
        "expert_w2": nrm((L, N_EXPERTS, D_EXPERT, D), D_EXPERT ** -0.5),
        "expert_b2": nrm((L, N_EXPERTS, D), 0.02),
    }


def reference(x, c, ctx, c_ctx, ada_w, ada_b, norm_g, w_in, attn_sink, mla_q_norm, mla_w_uq,
              mla_kv_norm, mla_w_ukv, gla_w_decay, gla_b_decay, gla_out_norm, smlp_v_norm,
              smlp_w_spatial, smlp_b_spatial, w_branch, w_gate, b_gate, w_out, router_w, router_b,
              expert_w1, expert_b1, expert_w2, expert_b2):
    B, S, D = x.shape
    for l in range(DEPTH):
        last = l == DEPTH - 1
        mod = (jax.nn.silu(c) @ ada_w[l] + ada_b[l]).reshape(B, 1, N_MOD, D)
        mod_c = (jax.nn.silu(c_ctx) @ ada_w[l] + ada_b[l]).reshape(N_MOD, D)
        h = modulate(rmsnorm(x, norm_g[l, 0]), mod[:, :, 0], mod[:, :, 1])
        hc = modulate(rmsnorm(ctx, norm_g[l, 0]), mod_c[0], mod_c[1])
        y, yc = hybrid_mixer(h, hc, not last, w_in[l], attn_sink[l], mla_q_norm[l], mla_w_uq[l],
                             mla_kv_norm[l], mla_w_ukv[l], gla_w_decay[l], gla_b_decay[l],
                             gla_out_norm[l], smlp_v_norm[l], smlp_w_spatial[l], smlp_b_spatial[l],
                             w_branch[l], w_gate[l], b_gate[l], w_out[l])
        x = x + mod[:, :, 2] * rmsnorm(y, norm_g[l, 1])
        h = modulate(rmsnorm(x, norm_g[l, 2]), mod[:, :, 3], mod[:, :, 4])
        tokens = h.reshape(B * S, D)
        if not last:
            ctx = ctx + mod_c[2] * rmsnorm(yc, norm_g[l, 1])
            hc = modulate(rmsnorm(ctx, norm_g[l, 2]), mod_c[3], mod_c[4])
            tokens = jnp.concatenate([tokens, hc.reshape(-1, D)], axis=0)
        f = moe_ffn(tokens, router_w[l], router_b[l], expert_w1[l], expert_b1[l],
                    expert_w2[l], expert_b2[l])
        x = x + mod[:, :, 5] * rmsnorm(f[:B * S].reshape(B, S, D), norm_g[l, 3])
        if not last:
            ctx = ctx + mod_c[5] * rmsnorm(f[B * S:].reshape(ctx.shape), norm_g[l, 3])
    return x
```

```python
import functools

import jax
import jax.numpy as jnp
from jax import lax
from jax.experimental import pallas as pl
from jax.experimental.pallas import tpu as pltpu

F32 = jnp.float32
BF16 = jnp.bfloat16
I32 = jnp.int32

EPS = 1e-6
ROPE_BASE = 10000.0
GRID_W = 64
N_MOD = 6

SWA_HEADS = 8
SWA_KV_HEADS = 2
SWA_GROUP = SWA_HEADS // SWA_KV_HEADS
SWA_HEAD_DIM = 64
SWA_BLOCK = 128

MLA_HEADS = 4
MLA_Q_RANK = 384
MLA_KV_RANK = 128
MLA_NOPE_DIM = 128
MLA_ROPE_DIM = 64
MLA_V_DIM = 128
MLA_QK_PAD = 256

GLA_HEADS = 4
GLA_DK = 64
GLA_DV = 128
GLA_DECAY_RANK = 16
GLA_TAU = 16.0
GLA_CHUNK = 64

SMLP_GROUPS = 4
SMLP_CHUNK = 128
SMLP_GROUP_DIM = 128

N_EXPERTS = 32
TOP_K = 4
D_EXPERT = 1024
SWIGLU_LIMIT = 7.0
SWIGLU_ALPHA = 1.702

LANE = 128
NEG_BIG = -1e30
VMEM_LIMIT = 56 * 1024 * 1024

P_BLK = 512
P_NBLK = 10
P_WIDTH = P_BLK * P_NBLK
EXPERT_TILE = 256


def _dot(a, b):
    return jnp.dot(a, b, preferred_element_type=F32)


def _dot_nt(a, b):
    return lax.dot_general(a, b, (((1,), (1,)), ((), ())), preferred_element_type=F32)


def _dot_tn(a, b):
    return lax.dot_general(a, b, (((0,), (0,)), ((), ())), preferred_element_type=F32)


def _split_bf16(a):
    hi = a.astype(BF16)
    lo = (a - hi.astype(F32)).astype(BF16)
    return hi, lo


def _rms(x, g):
    return x * lax.rsqrt(jnp.mean(x * x, axis=-1, keepdims=True) + EPS) * g


def _params(*sem):
    return pltpu.CompilerParams(dimension_semantics=sem, vmem_limit_bytes=VMEM_LIMIT)


def _ada_kernel(c_ref, w_ref, b_ref, o_ref):
    c = c_ref[...]
    a_hi, a_lo = _split_bf16(c * jax.nn.sigmoid(c))
    w_hi, w_lo = _split_bf16(w_ref[...])
    o_ref[...] = _dot(a_hi, w_hi) + _dot(a_lo, w_hi) + _dot(a_hi, w_lo) + b_ref[...]


def _ada(cs, ada_w, ada_b):
    n_layer, d, n = ada_w.shape
    tn = 1024
    return pl.pallas_call(
        _ada_kernel,
        grid=(n_layer, n // tn),
        in_specs=[pl.BlockSpec((8, d), lambda l, j: (0, 0)),
                  pl.BlockSpec((None, d, tn), lambda l, j: (l, 0, j)),
                  pl.BlockSpec((None, 1, tn), lambda l, j: (l, 0, j))],
        out_specs=pl.BlockSpec((None, 8, tn), lambda l, j: (l, 0, j)),
        out_shape=jax.ShapeDtypeStruct((n_layer, 8, n), F32),
        compiler_params=_params("arbitrary", "arbitrary"),
        name="ada",
    )(cs, ada_w, ada_b.reshape(n_layer, 1, n))


def _inproj_kernel(x_ref, mod_ref, g_ref, w_ref, h_ref, p_ref, hs_ref):
    @pl.when(pl.program_id(2) == 0)
    def _():
        y = _rms(x_ref[...], g_ref[0:1, :])
        hb = (y * (1.0 + mod_ref[1:2, :]) + mod_ref[0:1, :]).astype(BF16)
        hs_ref[...] = hb
        h_ref[...] = hb

    p_ref[...] = _dot(hs_ref[...], w_ref[...]).astype(BF16)


def _inproj(x, mod, norm_g, w_ext):
    b, s, d = x.shape
    tm = min(512, s)
    tn = 1024
    return pl.pallas_call(
        _inproj_kernel,
        grid=(b, s // tm, P_WIDTH // tn),
        in_specs=[pl.BlockSpec((None, tm, d), lambda b_, i, j: (b_, i, 0)),
                  pl.BlockSpec((None, 8, d), lambda b_, i, j: (b_, 0, 0)),
                  pl.BlockSpec((4, d), lambda b_, i, j: (0, 0)),
                  pl.BlockSpec((d, tn), lambda b_, i, j: (0, j))],
        out_specs=[pl.BlockSpec((None, tm, d), lambda b_, i, j: (b_, i, 0)),
                   pl.BlockSpec((None, tm, tn), lambda b_, i, j: (b_, i, j))],
        out_shape=[jax.ShapeDtypeStruct((b, s, d), BF16),
                   jax.ShapeDtypeStruct((b, s, P_WIDTH), BF16)],
        scratch_shapes=[pltpu.VMEM((tm, d), BF16)],
        compiler_params=_params("arbitrary", "arbitrary", "arbitrary"),
        name="inproj",
    )(x, mod, norm_g, w_ext)


def _prep_kernel(kblk_ref, cblk_ref, cos_ref, sin_ref, qn_g_ref, kv_g_ref, wuq_ref, wkn_ref,
                 ka_ref, kcat_ref, q_ref):
    cos = cos_ref[...]
    sin = sin_ref[...]
    kb = kblk_ref[...].astype(F32)
    ka_ref[...] = (kb[:, 0:128] * cos[:, 0:128] + kb[:, 128:256] * sin[:, 0:128]).astype(BF16)
    kr = kb[:, 384:448] * cos[:, 0:64] + kb[:, 448:512] * sin[:, 0:64]
    cb = cblk_ref[...].astype(F32)
    ckvn = _rms(cb[:, MLA_Q_RANK:], kv_g_ref[...])
    tm = kb.shape[0]
    kcat_ref[:, 0:128] = ckvn.astype(BF16)
    kcat_ref[:, 128:192] = kr.astype(BF16)
    kcat_ref[:, 192:256] = jnp.zeros((tm, 64), BF16)
    cqn = _rms(cb[:, :MLA_Q_RANK], qn_g_ref[...]).astype(BF16)
    qa = _dot(cqn, wuq_ref[...])
    qr = qa[:, 512:768] * cos[:, 0:256] + qa[:, 768:1024] * sin[:, 0:256]
    scale = (MLA_NOPE_DIM + MLA_ROPE_DIM) ** -0.5
    for h in range(MLA_HEADS):
        qn = qa[:, h * 128:(h + 1) * 128].astype(BF16)
        q_ref[h, :, 0:128] = (_dot_nt(qn, wkn_ref[h]) * scale).astype(BF16)
        q_ref[h, :, 128:192] = (qr[:, h * 64:(h + 1) * 64] * scale).astype(BF16)
        q_ref[h, :, 192:256] = jnp.zeros((tm, 64), BF16)


def _prep(p, cos_t, sin_t, qn_g, kv_g, wuq_ext, wkn):
    b, s, _ = p.shape
    tm = min(512, s)
    return pl.pallas_call(
        _prep_kernel,
        grid=(b, s // tm),
        in_specs=[pl.BlockSpec((None, tm, P_BLK), lambda b_, i: (b_, i, 2)),
                  pl.BlockSpec((None, tm, P_BLK), lambda b_, i: (b_, i, 3)),
                  pl.BlockSpec((tm, 512), lambda b_, i: (i, 0)),
                  pl.BlockSpec((tm, 512), lambda b_, i: (i, 0)),
                  pl.BlockSpec((1, MLA_Q_RANK), lambda b_, i: (0, 0)),
                  pl.BlockSpec((1, MLA_KV_RANK), lambda b_, i: (0, 0)),
                  pl.BlockSpec((MLA_Q_RANK, 1024), lambda b_, i: (0, 0)),
                  pl.BlockSpec((MLA_HEADS, MLA_KV_RANK, MLA_NOPE_DIM), lambda b_, i: (0, 0, 0))],
        out_specs=[pl.BlockSpec((None, tm, 128), lambda b_, i: (b_, i, 0)),
                   pl.BlockSpec((None, tm, MLA_QK_PAD), lambda b_, i: (b_, i, 0)),
                   pl.BlockSpec((None, MLA_HEADS, tm, MLA_QK_PAD), lambda b_, i: (b_, 0, i, 0))],
        out_shape=[jax.ShapeDtypeStruct((b, s, 128), BF16),
                   jax.ShapeDtypeStruct((b, s, MLA_QK_PAD), BF16),
                   jax.ShapeDtypeStruct((b, MLA_HEADS, s, MLA_QK_PAD), BF16)],
        compiler_params=_params("arbitrary", "arbitrary"),
        name="prep",
    )(p, p, cos_t, sin_t, qn_g, kv_g, wuq_ext, wkn)


def _swa_softmax_out(h, qh, sink, win, ctx):
    (kc, vc) = ctx
    parts = []
    for (k, v, mask) in win:
        s = _dot_nt(qh, k)
        if mask is not None:
            s = jnp.where(mask, s, NEG_BIG)
        parts.append((s, v))
    parts.append((_dot_nt(qh, kc), vc))
    m = jnp.full((qh.shape[0], 1), sink, F32)
    for s, _ in parts:
        m = jnp.maximum(m, jnp.max(s, axis=-1, keepdims=True))
    den = jnp.exp(sink - m)
    acc = None
    for s, v in parts:
        e = jnp.exp(s - m)
        den = den + jnp.sum(e, axis=-1, keepdims=True)
        pv = _dot(e.astype(BF16), v)
        acc = pv if acc is None else acc + pv
    return acc / den


def _swa_kernel(sink_ref, q_ref, qr_ref, cos_ref, sin_ref, kp_ref, kc_ref, kn_ref,
                vp_ref, vc_ref, vn_ref, kx_ref, vx_ref, o_ref):
    i = pl.program_id(1)
    nb = pl.num_programs(1)
    scale = SWA_HEAD_DIM ** -0.5
    q = ((q_ref[...].astype(F32) * cos_ref[...] + qr_ref[...].astype(F32) * sin_ref[...])
         * scale).astype(BF16)
    row = lax.broadcasted_iota(I32, (SWA_BLOCK, SWA_BLOCK), 0)
    col = lax.broadcasted_iota(I32, (SWA_BLOCK, SWA_BLOCK), 1)
    mask_prev = col >= row + jnp.where(i > 0, 0, 2 * SWA_BLOCK)
    mask_next = col <= row - jnp.where(i < nb - 1, 0, 2 * SWA_BLOCK)
    for h in range(SWA_HEADS):
        g = h // SWA_GROUP
        gs = slice(g * SWA_HEAD_DIM, (g + 1) * SWA_HEAD_DIM)
        win = [(kp_ref[:, gs], vp_ref[:, gs], mask_prev),
               (kc_ref[:, gs], vc_ref[:, gs], None),
               (kn_ref[:, gs], vn_ref[:, gs], mask_next)]
        o = _swa_softmax_out(h, q[:, h * 64:(h + 1) * 64], sink_ref[h], win,
                             (kx_ref[:, gs], vx_ref[:, gs]))
        o_ref[:, h * 64:(h + 1) * 64] = o.astype(BF16)


def _swa_ctx_kernel(sink_ref, q_ref, kx_ref, vx_ref, o_ref):
    scale = SWA_HEAD_DIM ** -0.5
    q = (q_ref[...].astype(F32) * scale).astype(BF16)
    for h in range(SWA_HEADS):
        g = h // SWA_GROUP
        gs = slice(g * SWA_HEAD_DIM, (g + 1) * SWA_HEAD_DIM)
        o = _swa_softmax_out(h, q[:, h * 64:(h + 1) * 64], sink_ref[h], [],
                             (kx_ref[:, gs], vx_ref[:, gs]))
        o_ref[:, h * 64:(h + 1) * 64] = o.astype(BF16)


def _swa(sink, p, ka, cos_t, sin_t, pc):
    b, s, _ = p.shape
    n_ctx = pc.shape[1]
    nb = s // SWA_BLOCK
    blk = SWA_BLOCK
    prev = lambda b_, i: (b_, jnp.maximum(i - 1, 0), 0)
    cur = lambda b_, i: (b_, i, 0)
    nxt = lambda b_, i: (b_, jnp.minimum(i + 1, nb - 1), 0)
    vcol = (2 * P_BLK + 256) // 128
    vprev = lambda b_, i: (b_, jnp.maximum(i - 1, 0), vcol)
    vcur = lambda b_, i: (b_, i, vcol)
    vnxt = lambda b_, i: (b_, jnp.minimum(i + 1, nb - 1), vcol)
    return pl.pallas_call(
        _swa_kernel,
        grid=(b, nb),
        in_specs=[pl.BlockSpec(memory_space=pltpu.SMEM),
                  pl.BlockSpec((None, blk, P_BLK), lambda b_, i: (b_, i, 0)),
                  pl.BlockSpec((None, blk, P_BLK), lambda b_, i: (b_, i, 1)),
                  pl.BlockSpec((blk, 512), lambda b_, i: (i, 0)),
                  pl.BlockSpec((blk, 512), lambda b_, i: (i, 0)),
                  pl.BlockSpec((None, blk, 128), prev),
                  pl.BlockSpec((None, blk, 128), cur),
                  pl.BlockSpec((None, blk, 128), nxt),
                  pl.BlockSpec((None, blk, 128), vprev),
                  pl.BlockSpec((None, blk, 128), vcur),
                  pl.BlockSpec((None, blk, 128), vnxt),
                  pl.BlockSpec((None, n_ctx, 128), lambda b_, i: (b_, 0, 2 * P_BLK // 128)),
                  pl.BlockSpec((None, n_ctx, 128), lambda b_, i: (b_, 0, vcol))],
        out_specs=pl.BlockSpec((None, blk, 512), lambda b_, i: (b_, i, 0)),
        out_shape=jax.ShapeDtypeStruct((b, s, 512), BF16),
        compiler_params=_params("arbitrary", "arbitrary"),
        name="swa",
    )(sink, p, p, cos_t, sin_t, ka, ka, ka, p, p, p, pc, pc)


def _swa_ctx(sink, pc):
    b, n_ctx, _ = pc.shape
    blk = SWA_BLOCK
    vcol = (2 * P_BLK + 256) // 128
    return pl.pallas_call(
        _swa_ctx_kernel,
        grid=(b, n_ctx // blk),
        in_specs=[pl.BlockSpec(memory_space=pltpu.SMEM),
                  pl.BlockSpec((None, blk, P_BLK), lambda b_, i: (b_, i, 0)),
                  pl.BlockSpec((None, n_ctx, 128), lambda b_, i: (b_, 0, 2 * P_BLK // 128)),
                  pl.BlockSpec((None, n_ctx, 128), lambda b_, i: (b_, 0, vcol))],
        out_specs=pl.BlockSpec((None, blk, 512), lambda b_, i: (b_, i, 0)),
        out_shape=jax.ShapeDtypeStruct((b, n_ctx, 512), BF16),
        compiler_params=_params("arbitrary", "arbitrary"),
        name="swa_ctx",
    )(sink, pc, pc, pc)


def _mla_kernel(q_ref, kv_ref, wv_ref, o_ref, m_ref, l_ref, acc_ref):
    j = pl.program_id(2)

    @pl.when(j == 0)
    def _():
        m_ref[...] = jnp.full(m_ref.shape, NEG_BIG, F32)
        l_ref[...] = jnp.zeros(l_ref.shape, F32)
        acc_ref[...] = jnp.zeros(acc_ref.shape, F32)

    nh, tq, dq = q_ref.shape
    q = q_ref[...].reshape(nh * tq, dq)
    kv = kv_ref[...]
    s = _dot_nt(q, kv)
    m_prev = m_ref[...]
    m_new = jnp.maximum(m_prev, jnp.max(s, axis=-1, keepdims=True))
    alpha = jnp.exp(m_prev - m_new)
    p = jnp.exp(s - m_new)
    l_ref[...] = alpha * l_ref[...] + jnp.sum(p, axis=-1, keepdims=True)
    acc_ref[...] = alpha * acc_ref[...] + _dot(p.astype(BF16), kv[:, 0:MLA_KV_RANK])
    m_ref[...] = m_new

    @pl.when(j == pl.num_programs(2) - 1)
    def _():
        o = (acc_ref[...] / l_ref[...]).astype(BF16)
        for h in range(nh):
            o_ref[:, h * MLA_V_DIM:(h + 1) * MLA_V_DIM] = _dot(
                o[h * tq:(h + 1) * tq, :], wv_ref[h]).astype(BF16)


def _mla(q, kcat, wv):
    b, nh, sq, dq = q.shape
    sk = kcat.shape[1]
    tq = min(512, sq)
    tk = next(t for t in (1280, 640, 512, 384, 256, 128) if sk % t == 0)
    return pl.pallas_call(
        _mla_kernel,
        grid=(b, sq // tq, sk // tk),
        in_specs=[pl.BlockSpec((None, nh, tq, dq), lambda b_, i, j: (b_, 0, i, 0)),
                  pl.BlockSpec((None, tk, dq), lambda b_, i, j: (b_, j, 0)),
                  pl.BlockSpec((nh, MLA_KV_RANK, MLA_V_DIM), lambda b_, i, j: (0, 0, 0))],
        out_specs=pl.BlockSpec((None, tq, nh * MLA_V_DIM), lambda b_, i, j: (b_, i, 0)),
        out_shape=jax.ShapeDtypeStruct((b, sq, nh * MLA_V_DIM), BF16),
        scratch_shapes=[pltpu.VMEM((nh * tq, 1), F32), pltpu.VMEM((nh * tq, 1), F32),
                        pltpu.VMEM((nh * tq, MLA_KV_RANK), F32)],
        compiler_params=_params("arbitrary", "arbitrary", "arbitrary"),
        name="mla",
    )(q, kcat, wv)


def _gla_kernel(qk_ref, v_ref, lr_ref, wd_ref, bd_ref, s0_ref, o_ref, sfin_ref, la_ref, st_ref):
    d = pl.program_id(1)
    i = pl.program_id(2)
    tc = qk_ref.shape[0]
    n_chunk = tc // GLA_CHUNK
    nk = GLA_HEADS * GLA_DK

    @pl.when(i == 0)
    def _():
        st_ref[...] = s0_ref[...]

    z = _dot(lr_ref[...], wd_ref[...]) + bd_ref[...]
    la_ref[...] = jax.nn.log_sigmoid(z) * (1.0 / GLA_TAU)

    row = lax.broadcasted_iota(I32, (GLA_CHUNK, GLA_CHUNK), 0)
    col = lax.broadcasted_iota(I32, (GLA_CHUNK, GLA_CHUNK), 1)
    tri = (row - col) * (1 - 2 * d) >= 0
    tri_b = jnp.where(tri, 1.0, 0.0).astype(BF16)

    def chunk(c, carry):
        cc = c + d * (n_chunk - 1 - 2 * c)
        off = pl.multiple_of(cc * GLA_CHUNK, GLA_CHUNK)
        la = la_ref[pl.ds(off, GLA_CHUNK), :]
        la_hi, la_lo = _split_bf16(la)
        bc = _dot(tri_b, la_hi) + _dot(tri_b, la_lo)
        bend = jnp.sum(la, axis=0, keepdims=True)
        qk = qk_ref[pl.ds(off, GLA_CHUNK), :].astype(F32)
        q = qk[:, 0:nk]
        k = qk[:, nk:2 * nk]
        v = v_ref[pl.ds(off, GLA_CHUNK), :]
        qd = (q * (GLA_DK ** -0.5) * jnp.exp(bc)).astype(BF16)
        ki = (k * jnp.exp(-bc)).astype(BF16)
        kd = (k * jnp.exp(bend - bc)).astype(BF16)
        dec = jnp.exp(bend)
        for h in range(GLA_HEADS):
            ks = slice(h * GLA_DK, (h + 1) * GLA_DK)
            vs = slice(h * GLA_DV, (h + 1) * GLA_DV)
            vh = v[:, vs]
            sc = jnp.where(tri, _dot_nt(qd[:, ks], ki[:, ks]), 0.0)
            st = st_ref[h]
            o = _dot(sc.astype(BF16), vh) + _dot_nt(qd[:, ks], st.astype(BF16))
            o_ref[pl.ds(off, GLA_CHUNK), vs] = o
            st_ref[h] = st * dec[:, ks] + _dot_tn(vh, kd[:, ks])
        return carry

    lax.fori_loop(0, n_chunk, chunk, 0)

    @pl.when(i == pl.num_programs(2) - 1)
    def _():
        sfin_ref[...] = st_ref[...]


def _gla(p, wd_pad, bd, s0):
    b, s, _ = p.shape
    tc = min(512, s)
    n_step = s // tc
    rows = lambda b_, d, i: i + d * (n_step - 1 - 2 * i)
    return pl.pallas_call(
        _gla_kernel,
        grid=(b, 2, n_step),
        in_specs=[pl.BlockSpec((None, tc, P_BLK), lambda b_, d, i: (b_, rows(b_, d, i), 4)),
                  pl.BlockSpec((None, tc, P_BLK), lambda b_, d, i: (b_, rows(b_, d, i), 5)),
                  pl.BlockSpec((None, tc, 128), lambda b_, d, i: (b_, rows(b_, d, i), 9 * P_BLK // 128)),
                  pl.BlockSpec((None, 128, 256), lambda b_, d, i: (d, 0, 0)),
                  pl.BlockSpec((None, 1, 256), lambda b_, d, i: (d, 0, 0)),
                  pl.BlockSpec((None, None, GLA_HEADS, GLA_DV, GLA_DK), lambda b_, d, i: (b_, d, 0, 0, 0))],
        out_specs=[pl.BlockSpec((None, None, tc, 512), lambda b_, d, i: (d, b_, rows(b_, d, i), 0)),
                   pl.BlockSpec((None, None, GLA_HEADS, GLA_DV, GLA_DK), lambda b_, d, i: (b_, d, 0, 0, 0))],
        out_shape=[jax.ShapeDtypeStruct((2, b, s, 512), F32),
                   jax.ShapeDtypeStruct((b, 2, GLA_HEADS, GLA_DV, GLA_DK), F32)],
        scratch_shapes=[pltpu.VMEM((tc, 256), F32), pltpu.VMEM((GLA_HEADS, GLA_DV, GLA_DK), F32)],
        compiler_params=_params("arbitrary", "arbitrary", "arbitrary"),
        name="gla",
    )(p, p, p, wd_pad, bd, s0)


def _smlp_kernel(u_ref, v_ref, g_ref, ws_ref, bs_ref, o_ref):
    n_chunk = u_ref.shape[0] // SMLP_CHUNK
    g = g_ref[...]
    for c in range(n_chunk):
        rs = slice(c * SMLP_CHUNK, (c + 1) * SMLP_CHUNK)
        vb = _rms(jax.nn.gelu(v_ref[rs, :].astype(F32)), g).astype(BF16)
        u = jax.nn.gelu(u_ref[rs, :].astype(F32))
        for k in range(SMLP_GROUPS):
            cs = slice(k * SMLP_GROUP_DIM, (k + 1) * SMLP_GROUP_DIM)
            mixed = _dot(ws_ref[k], vb[:, cs]) + bs_ref[:, cs]
            o_ref[rs, cs] = (u[:, cs] * mixed).astype(BF16)


def _smlp(p, v_norm, ws, bs_full):
    b, s, _ = p.shape
    tc = min(512, s)
    return pl.pallas_call(
        _smlp_kernel,
        grid=(b, s // tc),
        in_specs=[pl.BlockSpec((None, tc, P_BLK), lambda b_, i: (b_, i, 7)),
                  pl.BlockSpec((None, tc, P_BLK), lambda b_, i: (b_, i, 8)),
                  pl.BlockSpec((1, 512), lambda b_, i: (0, 0)),
                  pl.BlockSpec((SMLP_GROUPS, SMLP_CHUNK, SMLP_CHUNK), lambda b_, i: (0, 0, 0)),
                  pl.BlockSpec((SMLP_CHUNK, 512), lambda b_, i: (0, 0))],
        out_specs=pl.BlockSpec((None, tc, 512), lambda b_, i: (b_, i, 0)),
        out_shape=jax.ShapeDtypeStruct((b, s, 512), BF16),
        compiler_params=_params("arbitrary", "arbitrary"),
        name="smlp",
    )(p, p, v_norm, ws, bs_full)


def _merge_kernel(h_ref, oa_ref, om_ref, gf_ref, gb_ref, gr_ref, os_ref, gn_ref,
                  wg_ref, bg_ref, wb_ref, y_ref, og_ref):
    @pl.when(pl.program_id(2) == 0)
    def _():
        o = gf_ref[...] + gb_ref[...]
        r = gr_ref[...].astype(F32)
        gate = r * jax.nn.sigmoid(r)
        for hh in range(GLA_HEADS):
            vs = slice(hh * GLA_DV, (hh + 1) * GLA_DV)
            og_ref[:, vs] = (_rms(o[:, vs], gn_ref[...]) * gate[:, vs]).astype(BF16)

    h = h_ref[...]
    branches = (oa_ref[...], om_ref[...], og_ref[...], os_ref[...])
    y = None
    for n, o in enumerate(branches):
        gate = jax.nn.sigmoid(_dot(h, wg_ref[n]) + bg_ref[n])
        t = gate * _dot(o, wb_ref[n])
        y = t if y is None else y + t
    y_ref[...] = y.astype(BF16)


def _merge(h, o_a, o_m, o_g2, p, o_s, gla_norm, wg, bg, wb):
    b, s, d = h.shape
    tm = min(512, s)
    tn = 512
    row = lambda b_, i, j: (b_, i, 0)
    return pl.pallas_call(
        _merge_kernel,
        grid=(b, s // tm, d // tn),
        in_specs=[pl.BlockSpec((None, tm, d), row),
                  pl.BlockSpec((None, tm, 512), row),
                  pl.BlockSpec((None, tm, 512), row),
                  pl.BlockSpec((None, None, tm, 512), lambda b_, i, j: (0, b_, i, 0)),
                  pl.BlockSpec((None, None, tm, 512), lambda b_, i, j: (1, b_, i, 0)),
                  pl.BlockSpec((None, tm, P_BLK), lambda b_, i, j: (b_, i, 6)),
                  pl.BlockSpec((None, tm, 512), row),
                  pl.BlockSpec((1, GLA_DV), lambda b_, i, j: (0, 0)),
                  pl.BlockSpec((4, d, tn), lambda b_, i, j: (0, 0, j)),
                  pl.BlockSpec((4, 1, tn), lambda b_, i, j: (0, 0, j)),
                  pl.BlockSpec((4, 512, tn), lambda b_, i, j: (0, 0, j))],
        out_specs=pl.BlockSpec((None, tm, tn), lambda b_, i, j: (b_, i, j)),
        out_shape=jax.ShapeDtypeStruct((b, s, d), BF16),
        scratch_shapes=[pltpu.VMEM((tm, 512), BF16)],
        compiler_params=_params("arbitrary", "arbitrary", "arbitrary"),
        name="merge",
    )(h, o_a, o_m, o_g2, o_g2, p, o_s, gla_norm, wg, bg, wb)


def _outproj_kernel(y_ref, x_ref, mod_ref, g_ref, wo_ref, wr_ref, br_ref, xo_ref, h2_ref, lg_ref):
    z = _dot(y_ref[...], wo_ref[...])
    xn = x_ref[...] + mod_ref[2:3, :] * _rms(z, g_ref[1:2, :])
    xo_ref[...] = xn
    h2 = _rms(xn, g_ref[2:3, :]) * (1.0 + mod_ref[4:5, :]) + mod_ref[3:4, :]
    h2_ref[...] = h2
    h_hi, h_lo = _split_bf16(h2)
    w_hi, w_lo = _split_bf16(wr_ref[...])
    lg_ref[...] = _dot(h_hi, w_hi) + _dot(h_lo, w_hi) + _dot(h_hi, w_lo) + br_ref[...]


def _outproj(y, x, mod, norm_g, w_out, wr_pad, br_pad):
    b, s, d = x.shape
    tm = min(256, s)
    row = lambda b_, i: (b_, i, 0)
    return pl.pallas_call(
        _outproj_kernel,
        grid=(b, s // tm),
        in_specs=[pl.BlockSpec((None, tm, d), row),
                  pl.BlockSpec((None, tm, d), row),
                  pl.BlockSpec((None, 8, d), lambda b_, i: (b_, 0, 0)),
                  pl.BlockSpec((4, d), lambda b_, i: (0, 0)),
                  pl.BlockSpec((d, d), lambda b_, i: (0, 0)),
                  pl.BlockSpec((d, LANE), lambda b_, i: (0, 0)),
                  pl.BlockSpec((1, LANE), lambda b_, i: (0, 0))],
        out_specs=[pl.BlockSpec((None, tm, d), row),
                   pl.BlockSpec((None, tm, d), row),
                   pl.BlockSpec((None, tm, LANE), row)],
        out_shape=[jax.ShapeDtypeStruct((b, s, d), F32),
                   jax.ShapeDtypeStruct((b, s, d), F32),
                   jax.ShapeDtypeStruct((b, s, LANE), F32)],
        compiler_params=_params("arbitrary", "arbitrary"),
        name="outproj",
    )(y, x, mod, norm_g, w_out, wr_pad, br_pad)


def _route_kernel(lg_ref, idx_ref, w_ref, rank_ref, cnt_ref, carry_ref):
    i = pl.program_id(0)

    @pl.when(i == 0)
    def _():
        carry_ref[...] = jnp.zeros(carry_ref.shape, F32)

    lg = lg_ref[...]
    tm = lg.shape[0]
    lane = lax.broadcasted_iota(I32, lg.shape, 1).astype(F32)
    vals, idxs = [], []
    onehot = jnp.zeros(lg.shape, F32)
    for _ in range(TOP_K):
        m = jnp.max(lg, axis=-1, keepdims=True)
        sel = jnp.min(jnp.where(lg == m, lane, float(LANE)), axis=-1, keepdims=True)
        hit = lane == sel
        vals.append(m)
        idxs.append(sel)
        onehot = jnp.where(hit, 1.0, onehot)
        lg = jnp.where(hit, NEG_BIG, lg)
    es = [jnp.exp(v - vals[0]) for v in vals]
    den = es[0] + es[1] + es[2] + es[3]
    r = lax.broadcasted_iota(I32, (tm, tm), 0)
    c = lax.broadcasted_iota(I32, (tm, tm), 1)
    lower = (c < r).astype(F32).astype(BF16)
    before = _dot(lower, onehot.astype(BF16)) + carry_ref[...]
    for k in range(TOP_K):
        idx_ref[:, k:k + 1] = idxs[k].astype(I32)
        w_ref[:, k:k + 1] = es[k] / den
        rank_ref[:, k:k + 1] = jnp.sum(jnp.where(lane == idxs[k], before, 0.0), axis=-1,
                                       keepdims=True).astype(I32)
    carry_ref[...] = carry_ref[...] + jnp.sum(onehot, axis=0, keepdims=True)
    cnt_ref[...] = carry_ref[...]


def _route(logits):
    t = logits.shape[0]
    tm = 256
    return pl.pallas_call(
        _route_kernel,
        grid=(t // tm,),
        in_specs=[pl.BlockSpec((tm, LANE), lambda i: (i, 0))],
        out_specs=[pl.BlockSpec((tm, TOP_K), lambda i: (i, 0)),
                   pl.BlockSpec((tm, TOP_K), lambda i: (i, 0)),
                   pl.BlockSpec((tm, TOP_K), lambda i: (i, 0)),
                   pl.BlockSpec((1, LANE), lambda i: (0, 0))],
        out_shape=[jax.ShapeDtypeStruct((t, TOP_K), I32),
                   jax.ShapeDtypeStruct((t, TOP_K), F32),
                   jax.ShapeDtypeStruct((t, TOP_K), I32),
                   jax.ShapeDtypeStruct((1, LANE), F32)],
        scratch_shapes=[pltpu.VMEM((1, LANE), F32)],
        compiler_params=_params("arbitrary"),
        name="route",
    )(logits)


def _dispatch_kernel(dest_hbm, src_hbm, xs_in, xs_out, dest_smem, idx_sem, sem):
    del xs_in
    i = pl.program_id(0)
    n = dest_smem.shape[0]
    tm = n // TOP_K
    cp = pltpu.make_async_copy(dest_hbm.at[i], dest_smem, idx_sem)
    cp.start()
    cp.wait()

    def row_copy(t, k):
        return pltpu.make_async_copy(src_hbm.at[pl.ds(i * tm + t, 1)],
                                     xs_out.at[pl.ds(dest_smem[t * TOP_K + k], 1)], sem)

    def issue(t, carry):
        for k in range(TOP_K):
            row_copy(t, k).start()
        return carry

    def drain(t, carry):
        for k in range(TOP_K):
            row_copy(t, k).wait()
        return carry

    lax.fori_loop(0, tm, issue, 0)
    lax.fori_loop(0, tm, drain, 0)


def _dispatch(dest, src, xs):
    t, d = src.shape
    tm = min(256, t)
    dest2 = dest.reshape(t // tm, tm * TOP_K)
    return pl.pallas_call(
        _dispatch_kernel,
        grid=(t // tm,),
        in_specs=[pl.BlockSpec(memory_space=pl.ANY),
                  pl.BlockSpec(memory_space=pl.ANY),
                  pl.BlockSpec(memory_space=pl.ANY)],
        out_specs=pl.BlockSpec(memory_space=pl.ANY),
        out_shape=jax.ShapeDtypeStruct(xs.shape, xs.dtype),
        scratch_shapes=[pltpu.SMEM((tm * TOP_K,), I32), pltpu.SemaphoreType.DMA,
                        pltpu.SemaphoreType.DMA],
        input_output_aliases={2: 0},
        compiler_params=_params("arbitrary"),
        name="dispatch",
    )(dest2, src, xs)


def _expert_kernel(te_ref, nu_ref, x_ref, w1_ref, b1_ref, w2_ref, b2_ref, y_ref):
    @pl.when(pl.program_id(0) < nu_ref[0])
    def _():
        z = _dot(x_ref[...].astype(BF16), w1_ref[...]) + b1_ref[...]
        glu = jnp.minimum(z[:, :D_EXPERT], SWIGLU_LIMIT)
        lin = jnp.clip(z[:, D_EXPERT:], -SWIGLU_LIMIT, SWIGLU_LIMIT)
        act = glu * jax.nn.sigmoid(SWIGLU_ALPHA * glu) * (lin + 1.0)
        y_ref[...] = _dot(act.astype(BF16), w2_ref[...]) + b2_ref[...]


def _experts(tile_e, n_used, xs, w1, b1, w2, b2):
    n_rows, d = xs.shape
    tr = EXPERT_TILE
    n_tiles = n_rows // tr
    row = lambda i, te, nu: (jnp.minimum(i, nu[0] - 1), 0)
    wsel = lambda i, te, nu: (te[i], 0, 0)
    return pl.pallas_call(
        _expert_kernel,
        grid_spec=pltpu.PrefetchScalarGridSpec(
            num_scalar_prefetch=2,
            grid=(n_tiles,),
            in_specs=[pl.BlockSpec((tr, d), row),
                      pl.BlockSpec((None, d, 2 * D_EXPERT), wsel),
                      pl.BlockSpec((None, 1, 2 * D_EXPERT), wsel),
                      pl.BlockSpec((None, D_EXPERT, d), wsel),
                      pl.BlockSpec((None, 1, d), wsel)],
            out_specs=pl.BlockSpec((tr, d), row)),
        out_shape=jax.ShapeDtypeStruct((n_rows, d), F32),
        compiler_params=_params("arbitrary"),
        name="experts",
    )(tile_e, n_used, xs, w1, b1, w2, b2)


def _combine_kernel(dest_hbm, w_ref, x_ref, mod_ref, g_ref, ys_hbm, o_ref, dest_smem, buf, idx_sem, sem):
    tm = x_ref.shape[0]
    i = pl.program_id(0) * pl.num_programs(1) + pl.program_id(1)
    cp = pltpu.make_async_copy(dest_hbm.at[i], dest_smem, idx_sem)
    cp.start()
    cp.wait()

    def row_copy(t, k):
        return pltpu.make_async_copy(ys_hbm.at[pl.ds(dest_smem[t * TOP_K + k], 1)],
                                     buf.at[k, pl.ds(t, 1)], sem)

    def issue(t, carry):
        for k in range(TOP_K):
            row_copy(t, k).start()
        return carry

    def drain(t, carry):
        for k in range(TOP_K):
            row_copy(t, k).wait()
        return carry

    lax.fori_loop(0, tm, issue, 0)
    lax.fori_loop(0, tm, drain, 0)
    w = w_ref[...]
    f = w[:, 0:1] * buf[0]
    for k in range(1, TOP_K):
        f = f + w[:, k:k + 1] * buf[k]
    o_ref[...] = x_ref[...] + mod_ref[5:6, :] * _rms(f, g_ref[3:4, :])


def _combine(dest, w, x, mod, norm_g, ys):
    b, s, d = x.shape
    tm = min(128, s)
    ns = s // tm
    dest2 = dest.reshape(b * ns, tm * TOP_K)
    return pl.pallas_call(
        _combine_kernel,
        grid=(b, ns),
        in_specs=[pl.BlockSpec(memory_space=pl.ANY),
                  pl.BlockSpec((None, tm, TOP_K), lambda b_, i: (b_, i, 0)),
                  pl.BlockSpec((None, tm, d), lambda b_, i: (b_, i, 0)),
                  pl.BlockSpec((None, 8, d), lambda b_, i: (b_, 0, 0)),
                  pl.BlockSpec((4, d), lambda b_, i: (0, 0)),
                  pl.BlockSpec(memory_space=pl.ANY)],
        out_specs=pl.BlockSpec((None, tm, d), lambda b_, i: (b_, i, 0)),
        out_shape=jax.ShapeDtypeStruct((b, s, d), F32),
        scratch_shapes=[pltpu.SMEM((tm * TOP_K,), I32), pltpu.VMEM((TOP_K, tm, d), F32),
                        pltpu.SemaphoreType.DMA, pltpu.SemaphoreType.DMA],
        compiler_params=_params("arbitrary", "arbitrary"),
        name="combine",
    )(dest2, w.reshape(b, s, TOP_K), x, mod, norm_g, ys)


def _rot_half_cols(w, head_dim):
    d_in, n = w.shape
    w3 = w.reshape(d_in, n // head_dim, head_dim)
    half = head_dim // 2
    return jnp.concatenate([-w3[..., half:], w3[..., :half]], axis=-1).reshape(d_in, n)


def _extended_in_weights(w_in):
    sizes = (512, 128, 128, MLA_Q_RANK, MLA_KV_RANK, MLA_ROPE_DIM, 256, 256, 512,
             2 * GLA_DECAY_RANK, 512, 512, 512)
    cols, off = [], 0
    for n in sizes:
        cols.append(w_in[:, off:off + n])
        off += n
    (a_q, a_k, a_v, m_cq, m_ckv, m_kr, g_q, g_k, g_v, g_lr, g_r, s_u, s_v) = cols
    pad = jnp.zeros((w_in.shape[0], P_BLK - 2 * GLA_DECAY_RANK), w_in.dtype)
    ext = jnp.concatenate(
        [a_q, _rot_half_cols(a_q, SWA_HEAD_DIM),
         a_k, _rot_half_cols(a_k, SWA_HEAD_DIM), a_v, m_kr, _rot_half_cols(m_kr, MLA_ROPE_DIM),
         m_cq, m_ckv, g_q, g_k, g_v, g_r, s_u, s_v, g_lr, pad], axis=1)
    return ext.astype(BF16)


def _rope_tables(n_tok):
    n_rows = n_tok // GRID_W
    rows = jnp.repeat(jnp.arange(n_rows, dtype=F32), GRID_W)
    cols = jnp.tile(jnp.arange(GRID_W, dtype=F32), n_rows)
    n_freq = SWA_HEAD_DIM // 4
    inv_freq = ROPE_BASE ** (-jnp.arange(n_freq, dtype=F32) / n_freq)
    ang = jnp.concatenate([rows[:, None] * inv_freq, cols[:, None] * inv_freq], axis=-1)
    cos = jnp.concatenate([jnp.cos(ang), jnp.cos(ang)], axis=-1)
    sin = jnp.concatenate([jnp.sin(ang), jnp.sin(ang)], axis=-1)
    return jnp.tile(cos, (1, 8)), jnp.tile(sin, (1, 8))


def _moe(h2_parts, logits_parts, w1, b1, w2, b2):
    d = h2_parts[0].shape[1]
    logits = jnp.concatenate(logits_parts, axis=0)
    t_all = logits.shape[0]
    top_idx, top_w, rank, counts = _route(logits)
    counts = counts[0, :N_EXPERTS].astype(I32)
    tr = EXPERT_TILE
    padded = (counts + tr - 1) // tr * tr
    p_end = jnp.cumsum(padded)
    p_start = p_end - padded
    dest = p_start[top_idx] + rank
    n_tiles = -(-t_all * TOP_K // tr) + N_EXPERTS
    tile_e = jnp.minimum(jnp.searchsorted(p_end, jnp.arange(n_tiles, dtype=I32) * tr, side="right"),
                         N_EXPERTS - 1).astype(I32)
    n_used = (p_end[-1:] // tr).astype(I32)
    xs = jnp.zeros((n_tiles * tr, d), F32)
    off = 0
    for part in h2_parts:
        n = part.shape[0]
        xs = _dispatch(dest[off:off + n], part, xs)
        off += n
    ys = _experts(tile_e, n_used, xs, w1, b1, w2, b2)
    return dest, top_w, ys


def kernel(x, c, ctx, c_ctx, ada_w, ada_b, norm_g, w_in, attn_sink, mla_q_norm, mla_w_uq, mla_kv_norm, mla_w_ukv, gla_w_decay, gla_b_decay, gla_out_norm, smlp_v_norm, smlp_w_spatial, smlp_b_spatial, w_branch, w_gate, b_gate, w_out, router_w, router_b, expert_w1, expert_b1, expert_w2, expert_b2):
    bsz, seq, d = x.shape
    n_ctx = ctx.shape[1]
    depth = ada_w.shape[0]

    cs = jnp.zeros((8, d), F32).at[:bsz].set(c).at[bsz].set(c_ctx)
    mod_all = _ada(cs, ada_w, ada_b).reshape(depth, 8, N_MOD, d)
    mod_all = jnp.pad(mod_all, ((0, 0), (0, 0), (0, 8 - N_MOD), (0, 0)))

    cos_t, sin_t = _rope_tables(seq)
    cos_c = jnp.ones((bsz * n_ctx, 512), F32)
    sin_c = jnp.zeros((bsz * n_ctx, 512), F32)
    ctx = ctx.reshape(1, bsz * n_ctx, d)

    for l in range(depth):
        last = l == depth - 1
        mod = mod_all[l, :bsz]
        mod_c = mod_all[l, bsz:bsz + 1]
        g_l = norm_g[l]
        w_ext = _extended_in_weights(w_in[l])
        uq = mla_w_uq[l].reshape(MLA_Q_RANK, MLA_HEADS, MLA_NOPE_DIM + MLA_ROPE_DIM)
        uq_rope = uq[:, :, MLA_NOPE_DIM:].reshape(MLA_Q_RANK, MLA_HEADS * MLA_ROPE_DIM)
        wuq_ext = jnp.concatenate(
            [uq[:, :, :MLA_NOPE_DIM].reshape(MLA_Q_RANK, MLA_HEADS * MLA_NOPE_DIM), uq_rope,
             _rot_half_cols(uq_rope, MLA_ROPE_DIM)], axis=1).astype(BF16)
        ukv = mla_w_ukv[l].reshape(MLA_KV_RANK, MLA_HEADS, MLA_NOPE_DIM + MLA_V_DIM)
        wkn = jnp.transpose(ukv[:, :, :MLA_NOPE_DIM], (1, 0, 2)).astype(BF16)
        wv = jnp.transpose(ukv[:, :, MLA_NOPE_DIM:], (1, 0, 2)).astype(BF16)
        qn_g = mla_q_norm[l].reshape(1, MLA_Q_RANK)
        kv_g = mla_kv_norm[l].reshape(1, MLA_KV_RANK)
        wd_pad = jnp.zeros((2, 128, 256), F32)
        for dd in range(2):
            wd_pad = wd_pad.at[dd, dd * GLA_DECAY_RANK:(dd + 1) * GLA_DECAY_RANK].set(gla_w_decay[l, dd])
        wd_pad = wd_pad.astype(BF16)
        bd = gla_b_decay[l].reshape(2, 1, 256)
        gla_norm = gla_out_norm[l].reshape(1, GLA_DV)
        v_norm = smlp_v_norm[l].reshape(1, 512)
        ws = smlp_w_spatial[l].astype(BF16)
        bs_full = jnp.repeat(smlp_b_spatial[l].T, SMLP_GROUP_DIM, axis=1)
        wg = w_gate[l].astype(BF16)
        bg = b_gate[l].reshape(4, 1, d)
        wb = w_branch[l].astype(BF16)
        wo = w_out[l].astype(BF16)
        wr_pad = jnp.pad(router_w[l], ((0, 0), (0, LANE - N_EXPERTS)))
        br_pad = jnp.pad(router_b[l].reshape(1, N_EXPERTS), ((0, 0), (0, LANE - N_EXPERTS)),
                         constant_values=NEG_BIG)
        w1 = expert_w1[l].astype(BF16)
        b1 = expert_b1[l].reshape(N_EXPERTS, 1, 2 * D_EXPERT)
        w2 = expert_w2[l].astype(BF16)
        b2 = expert_b2[l].reshape(N_EXPERTS, 1, d)
        sink = attn_sink[l]

        h, p = _inproj(x, mod, g_l, w_ext)
        hc, pc1 = _inproj(ctx, mod_c, g_l, w_ext)
        pc = pc1.reshape(bsz, n_ctx, P_WIDTH)
        ka, kcat, q_m = _prep(p, cos_t, sin_t, qn_g, kv_g, wuq_ext, wkn)
        _, kcat_c, q_mc = _prep(pc1, cos_c, sin_c, qn_g, kv_g, wuq_ext, wkn)
        kcat_c = kcat_c.reshape(bsz, n_ctx, MLA_QK_PAD)

        o_a = _swa(sink, p, ka, cos_t, sin_t, pc)
        o_m = _mla(q_m, jnp.concatenate([kcat, kcat_c], axis=1), wv)
        s0 = jnp.zeros((bsz, 2, GLA_HEADS, GLA_DV, GLA_DK), F32)
        og_c, s_ctx = _gla(pc, wd_pad, bd, s0)
        og, _ = _gla(p, wd_pad, bd, s_ctx)
        o_s = _smlp(p, v_norm, ws, bs_full)
        y = _merge(h, o_a, o_m, og, p, o_s, gla_norm, wg, bg, wb)
        x, h2, lg = _outproj(y, x, mod, g_l, wo, wr_pad, br_pad)
        h2_parts = [h2.reshape(bsz * seq, d)]
        lg_parts = [lg.reshape(bsz * seq, LANE)]

        if not last:
            oc_a = _swa_ctx(sink, pc)
            q_mc = q_mc.reshape(MLA_HEADS, bsz, n_ctx, MLA_QK_PAD).transpose(1, 0, 2, 3)
            oc_m = _mla(q_mc, kcat_c, wv)
            oc_s = _smlp(pc, v_norm, ws, bs_full)
            flat = lambda a: a.reshape(1, bsz * n_ctx, a.shape[-1])
            yc = _merge(hc, flat(oc_a), flat(oc_m), og_c.reshape(2, 1, bsz * n_ctx, 512), pc1,
                        flat(oc_s), gla_norm, wg, bg, wb)
            ctx, h2c, lgc = _outproj(yc, ctx, mod_c, g_l, wo, wr_pad, br_pad)
            h2_parts.append(h2c.reshape(bsz * n_ctx, d))
            lg_parts.append(lgc.reshape(bsz * n_ctx, LANE))

        dest, top_w, ys = _moe(h2_parts, lg_parts, w1, b1, w2, b2)
        n_lat = bsz * seq
        x = _combine(dest[:n_lat], top_w[:n_lat], x, mod, g_l, ys)
        if not last:
            ctx = _combine(dest[n_lat:], top_w[n_lat:], ctx, mod_c, g_l, ys)
    return x
```

```python
import functools

import jax
import jax.numpy as jnp
from jax import lax
from jax.experimental import pallas as pl
from jax.experimental.pallas import tpu as pltpu

F32 = jnp.float32
BF16 = jnp.bfloat16
I32 = jnp.int32

EPS = 1e-6
ROPE_BASE = 10000.0
GRID_W = 64
N_MOD = 6

SWA_HEADS = 8
SWA_KV_HEADS = 2
SWA_GROUP = SWA_HEADS // SWA_KV_HEADS
SWA_HEAD_DIM = 64
SWA_BLOCK = 128

MLA_HEADS = 4
MLA_Q_RANK = 384
MLA_KV_RANK = 128
MLA_NOPE_DIM = 128
MLA_ROPE_DIM = 64
MLA_V_DIM = 128
MLA_QK_PAD = 256
MLA_ROW_CHUNK = 256
LOG2_E = 1.4426950408889634

GLA_HEADS = 4
GLA_DK = 64
GLA_DV = 128
GLA_DECAY_RANK = 16
GLA_TAU = 16.0
GLA_CHUNK = 64

SMLP_GROUPS = 4
SMLP_CHUNK = 128
SMLP_GROUP_DIM = 128

N_EXPERTS = 32
TOP_K = 4
D_EXPERT = 1024
SWIGLU_LIMIT = 7.0
SWIGLU_ALPHA = 1.702

LANE = 128
NEG_BIG = -1e30
VMEM_LIMIT = 56 * 1024 * 1024

P_BLK = 512
P_NBLK = 10
P_WIDTH = P_BLK * P_NBLK
EXPERT_TILE = 256
DMA_ISSUE_UNROLL = 8


def _dot(a, b):
    return jnp.dot(a, b, preferred_element_type=F32)


def _dot_nt(a, b):
    return lax.dot_general(a, b, (((1,), (1,)), ((), ())), preferred_element_type=F32)


def _dot_tn(a, b):
    return lax.dot_general(a, b, (((0,), (0,)), ((), ())), preferred_element_type=F32)


def _split_bf16(a):
    hi = a.astype(BF16)
    lo = (a - hi.astype(F32)).astype(BF16)
    return hi, lo


def _rms(x, g):
    return x * lax.rsqrt(jnp.mean(x * x, axis=-1, keepdims=True) + EPS) * g


def _params(*sem):
    return pltpu.CompilerParams(dimension_semantics=sem, vmem_limit_bytes=VMEM_LIMIT)


def _ada_kernel(c_ref, w_ref, b_ref, o_ref):
    c = c_ref[...]
    a_hi, a_lo = _split_bf16(c * jax.nn.sigmoid(c))
    w_hi, w_lo = _split_bf16(w_ref[...])
    o_ref[...] = _dot(a_hi, w_hi) + _dot(a_lo, w_hi) + _dot(a_hi, w_lo) + b_ref[...]


def _ada(cs, ada_w, ada_b):
    n_layer, d, n = ada_w.shape
    tn = 1024
    return pl.pallas_call(
        _ada_kernel,
        grid=(n_layer, n // tn),
        in_specs=[pl.BlockSpec((8, d), lambda l, j: (0, 0)),
                  pl.BlockSpec((None, d, tn), lambda l, j: (l, 0, j)),
                  pl.BlockSpec((None, 1, tn), lambda l, j: (l, 0, j))],
        out_specs=pl.BlockSpec((None, 8, tn), lambda l, j: (l, 0, j)),
        out_shape=jax.ShapeDtypeStruct((n_layer, 8, n), F32),
        compiler_params=_params("arbitrary", "arbitrary"),
        name="ada",
    )(cs, ada_w, ada_b.reshape(n_layer, 1, n))


def _inproj_kernel(x_ref, mod_ref, g_ref, w_ref, h_ref, p_ref, hs_ref):
    @pl.when(pl.program_id(2) == 0)
    def _():
        y = _rms(x_ref[...], g_ref[0:1, :])
        hb = (y * (1.0 + mod_ref[1:2, :]) + mod_ref[0:1, :]).astype(BF16)
        hs_ref[...] = hb
        h_ref[...] = hb

    p_ref[...] = _dot(hs_ref[...], w_ref[...]).astype(BF16)


def _inproj(x, mod, norm_g, w_ext):
    b, s, d = x.shape
    tm = min(512, s)
    tn = 1024
    return pl.pallas_call(
        _inproj_kernel,
        grid=(b, s // tm, P_WIDTH // tn),
        in_specs=[pl.BlockSpec((None, tm, d), lambda b_, i, j: (b_, i, 0)),
                  pl.BlockSpec((None, 8, d), lambda b_, i, j: (b_, 0, 0)),
                  pl.BlockSpec((4, d), lambda b_, i, j: (0, 0)),
                  pl.BlockSpec((d, tn), lambda b_, i, j: (0, j))],
        out_specs=[pl.BlockSpec((None, tm, d), lambda b_, i, j: (b_, i, 0)),
                   pl.BlockSpec((None, tm, tn), lambda b_, i, j: (b_, i, j))],
        out_shape=[jax.ShapeDtypeStruct((b, s, d), BF16),
                   jax.ShapeDtypeStruct((b, s, P_WIDTH), BF16)],
        scratch_shapes=[pltpu.VMEM((tm, d), BF16)],
        compiler_params=_params("arbitrary", "arbitrary", "arbitrary"),
        name="inproj",
    )(x, mod, norm_g, w_ext)


def _prep_kernel(kblk_ref, cblk_ref, cos_ref, sin_ref, qn_g_ref, kv_g_ref, wuq_ref, wkn_ref,
                 ka_ref, kcat_ref, q_ref):
    cos = cos_ref[...]
    sin = sin_ref[...]
    kb = kblk_ref[...].astype(F32)
    ka_ref[...] = (kb[:, 0:128] * cos[:, 0:128] + kb[:, 128:256] * sin[:, 0:128]).astype(BF16)
    kr = kb[:, 384:448] * cos[:, 0:64] + kb[:, 448:512] * sin[:, 0:64]
    cb = cblk_ref[...].astype(F32)
    ckvn = _rms(cb[:, MLA_Q_RANK:], kv_g_ref[...])
    tm = kb.shape[0]
    kcat_ref[:, 0:128] = ckvn.astype(BF16)
    kcat_ref[:, 128:192] = kr.astype(BF16)
    kcat_ref[:, 192:256] = jnp.zeros((tm, 64), BF16)
    cqn = _rms(cb[:, :MLA_Q_RANK], qn_g_ref[...]).astype(BF16)
    qa = _dot(cqn, wuq_ref[...])
    qr = qa[:, 512:768] * cos[:, 0:256] + qa[:, 768:1024] * sin[:, 0:256]
    scale = (MLA_NOPE_DIM + MLA_ROPE_DIM) ** -0.5 * LOG2_E
    for h in range(MLA_HEADS):
        qn = qa[:, h * 128:(h + 1) * 128].astype(BF16)
        q_ref[h, :, 0:128] = (_dot_nt(qn, wkn_ref[h]) * scale).astype(BF16)
        q_ref[h, :, 128:192] = (qr[:, h * 64:(h + 1) * 64] * scale).astype(BF16)
        q_ref[h, :, 192:256] = jnp.zeros((tm, 64), BF16)


def _prep(p, cos_t, sin_t, qn_g, kv_g, wuq_ext, wkn):
    b, s, _ = p.shape
    tm = min(512, s)
    return pl.pallas_call(
        _prep_kernel,
        grid=(b, s // tm),
        in_specs=[pl.BlockSpec((None, tm, P_BLK), lambda b_, i: (b_, i, 2)),
                  pl.BlockSpec((None, tm, P_BLK), lambda b_, i: (b_, i, 3)),
                  pl.BlockSpec((tm, 512), lambda b_, i: (i, 0)),
                  pl.BlockSpec((tm, 512), lambda b_, i: (i, 0)),
                  pl.BlockSpec((1, MLA_Q_RANK), lambda b_, i: (0, 0)),
                  pl.BlockSpec((1, MLA_KV_RANK), lambda b_, i: (0, 0)),
                  pl.BlockSpec((MLA_Q_RANK, 1024), lambda b_, i: (0, 0)),
                  pl.BlockSpec((MLA_HEADS, MLA_KV_RANK, MLA_NOPE_DIM), lambda b_, i: (0, 0, 0))],
        out_specs=[pl.BlockSpec((None, tm, 128), lambda b_, i: (b_, i, 0)),
                   pl.BlockSpec((None, tm, MLA_QK_PAD), lambda b_, i: (b_, i, 0)),
                   pl.BlockSpec((None, MLA_HEADS, tm, MLA_QK_PAD), lambda b_, i: (b_, 0, i, 0))],
        out_shape=[jax.ShapeDtypeStruct((b, s, 128), BF16),
                   jax.ShapeDtypeStruct((b, s, MLA_QK_PAD), BF16),
                   jax.ShapeDtypeStruct((b, MLA_HEADS, s, MLA_QK_PAD), BF16)],
        compiler_params=_params("arbitrary", "arbitrary"),
        name="prep",
    )(p, p, cos_t, sin_t, qn_g, kv_g, wuq_ext, wkn)


def _swa_softmax_out(h, qh, sink, win, ctx):
    (kc, vc) = ctx
    parts = []
    for (k, v, mask) in win:
        s = _dot_nt(qh, k)
        if mask is not None:
            s = jnp.where(mask, s, NEG_BIG)
        parts.append((s, v))
    parts.append((_dot_nt(qh, kc), vc))
    m = jnp.full((qh.shape[0], 1), sink, F32)
    for s, _ in parts:
        m = jnp.maximum(m, jnp.max(s, axis=-1, keepdims=True))
    den = jnp.exp(sink - m)
    acc = None
    for s, v in parts:
        e = jnp.exp(s - m)
        den = den + jnp.sum(e, axis=-1, keepdims=True)
        pv = _dot(e.astype(BF16), v)
        acc = pv if acc is None else acc + pv
    return acc / den


def _swa_kernel(sink_ref, q_ref, qr_ref, cos_ref, sin_ref, kp_ref, kc_ref, kn_ref,
                vp_ref, vc_ref, vn_ref, kx_ref, vx_ref, o_ref):
    i = pl.program_id(1)
    nb = pl.num_programs(1)
    scale = SWA_HEAD_DIM ** -0.5
    q = ((q_ref[...].astype(F32) * cos_ref[...] + qr_ref[...].astype(F32) * sin_ref[...])
         * scale).astype(BF16)
    row = lax.broadcasted_iota(I32, (SWA_BLOCK, SWA_BLOCK), 0)
    col = lax.broadcasted_iota(I32, (SWA_BLOCK, SWA_BLOCK), 1)
    mask_prev = col >= row + jnp.where(i > 0, 0, 2 * SWA_BLOCK)
    mask_next = col <= row - jnp.where(i < nb - 1, 0, 2 * SWA_BLOCK)
    for h in range(SWA_HEADS):
        g = h // SWA_GROUP
        gs = slice(g * SWA_HEAD_DIM, (g + 1) * SWA_HEAD_DIM)
        win = [(kp_ref[:, gs], vp_ref[:, gs], mask_prev),
               (kc_ref[:, gs], vc_ref[:, gs], None),
               (kn_ref[:, gs], vn_ref[:, gs], mask_next)]
        o = _swa_softmax_out(h, q[:, h * 64:(h + 1) * 64], sink_ref[h], win,
                             (kx_ref[:, gs], vx_ref[:, gs]))
        o_ref[:, h * 64:(h + 1) * 64] = o.astype(BF16)


def _swa_ctx_kernel(sink_ref, q_ref, kx_ref, vx_ref, o_ref):
    scale = SWA_HEAD_DIM ** -0.5
    q = (q_ref[...].astype(F32) * scale).astype(BF16)
    for h in range(SWA_HEADS):
        g = h // SWA_GROUP
        gs = slice(g * SWA_HEAD_DIM, (g + 1) * SWA_HEAD_DIM)
        o = _swa_softmax_out(h, q[:, h * 64:(h + 1) * 64], sink_ref[h], [],
                             (kx_ref[:, gs], vx_ref[:, gs]))
        o_ref[:, h * 64:(h + 1) * 64] = o.astype(BF16)


def _swa(sink, p, ka, cos_t, sin_t, pc):
    b, s, _ = p.shape
    n_ctx = pc.shape[1]
    nb = s // SWA_BLOCK
    blk = SWA_BLOCK
    prev = lambda b_, i: (b_, jnp.maximum(i - 1, 0), 0)
    cur = lambda b_, i: (b_, i, 0)
    nxt = lambda b_, i: (b_, jnp.minimum(i + 1, nb - 1), 0)
    vcol = (2 * P_BLK + 256) // 128
    vprev = lambda b_, i: (b_, jnp.maximum(i - 1, 0), vcol)
    vcur = lambda b_, i: (b_, i, vcol)
    vnxt = lambda b_, i: (b_, jnp.minimum(i + 1, nb - 1), vcol)
    return pl.pallas_call(
        _swa_kernel,
        grid=(b, nb),
        in_specs=[pl.BlockSpec(memory_space=pltpu.SMEM),
                  pl.BlockSpec((None, blk, P_BLK), lambda b_, i: (b_, i, 0)),
                  pl.BlockSpec((None, blk, P_BLK), lambda b_, i: (b_, i, 1)),
                  pl.BlockSpec((blk, 512), lambda b_, i: (i, 0)),
                  pl.BlockSpec((blk, 512), lambda b_, i: (i, 0)),
                  pl.BlockSpec((None, blk, 128), prev),
                  pl.BlockSpec((None, blk, 128), cur),
                  pl.BlockSpec((None, blk, 128), nxt),
                  pl.BlockSpec((None, blk, 128), vprev),
                  pl.BlockSpec((None, blk, 128), vcur),
                  pl.BlockSpec((None, blk, 128), vnxt),
                  pl.BlockSpec((None, n_ctx, 128), lambda b_, i: (b_, 0, 2 * P_BLK // 128)),
                  pl.BlockSpec((None, n_ctx, 128), lambda b_, i: (b_, 0, vcol))],
        out_specs=pl.BlockSpec((None, blk, 512), lambda b_, i: (b_, i, 0)),
        out_shape=jax.ShapeDtypeStruct((b, s, 512), BF16),
        compiler_params=_params("arbitrary", "arbitrary"),
        name="swa",
    )(sink, p, p, cos_t, sin_t, ka, ka, ka, p, p, p, pc, pc)


def _swa_ctx(sink, pc):
    b, n_ctx, _ = pc.shape
    blk = SWA_BLOCK
    vcol = (2 * P_BLK + 256) // 128
    return pl.pallas_call(
        _swa_ctx_kernel,
        grid=(b, n_ctx // blk),
        in_specs=[pl.BlockSpec(memory_space=pltpu.SMEM),
                  pl.BlockSpec((None, blk, P_BLK), lambda b_, i: (b_, i, 0)),
                  pl.BlockSpec((None, n_ctx, 128), lambda b_, i: (b_, 0, 2 * P_BLK // 128)),
                  pl.BlockSpec((None, n_ctx, 128), lambda b_, i: (b_, 0, vcol))],
        out_specs=pl.BlockSpec((None, blk, 512), lambda b_, i: (b_, i, 0)),
        out_shape=jax.ShapeDtypeStruct((b, n_ctx, 512), BF16),
        compiler_params=_params("arbitrary", "arbitrary"),
        name="swa_ctx",
    )(sink, pc, pc, pc)


def _mla_kernel(q_ref, kv_ref, wv_ref, o_ref, m_ref, l_ref, acc_ref):
    j = pl.program_id(2)

    @pl.when(j == 0)
    def _():
        m_ref[...] = jnp.full(m_ref.shape, NEG_BIG, F32)
        l_ref[...] = jnp.zeros(l_ref.shape, F32)
        acc_ref[...] = jnp.zeros(acc_ref.shape, F32)

    nh, tq, dq = q_ref.shape
    kv = kv_ref[...]
    v = kv[:, 0:MLA_KV_RANK]
    rc = min(MLA_ROW_CHUNK, tq)
    per_head = tq // rc
    n_chunk = nh * per_head

    def scores(c):
        h, r = divmod(c, per_head)
        return _dot_nt(q_ref[h, r * rc:(r + 1) * rc, :], kv)

    def accumulate(pending):
        rows, alpha, p = pending
        acc_ref[rows, :] = alpha * acc_ref[rows, :] + _dot(p, v)

    s_next = scores(0)
    pending = None
    for c in range(n_chunk):
        s = s_next
        if c + 1 < n_chunk:
            s_next = scores(c + 1)
        if pending is not None:
            accumulate(pending)
        rows = slice(c * rc, (c + 1) * rc)
        m_prev = m_ref[rows, :]
        m_new = jnp.maximum(m_prev, jnp.max(s, axis=-1, keepdims=True))
        alpha = jnp.exp2(m_prev - m_new)
        p = jnp.exp2(s - m_new)
        l_ref[rows, :] = alpha * l_ref[rows, :] + jnp.sum(p, axis=-1, keepdims=True)
        m_ref[rows, :] = m_new
        pending = (rows, alpha, p.astype(BF16))
    accumulate(pending)

    @pl.when(j == pl.num_programs(2) - 1)
    def _():
        o = (acc_ref[...] / l_ref[...]).astype(BF16)
        for h in range(nh):
            o_ref[:, h * MLA_V_DIM:(h + 1) * MLA_V_DIM] = _dot(
                o[h * tq:(h + 1) * tq, :], wv_ref[h]).astype(BF16)


def _mla(q, kcat, wv):
    b, nh, sq, dq = q.shape
    sk = kcat.shape[1]
    tq = min(512, sq)
    tk = next(t for t in (1280, 640, 512, 384, 256, 128) if sk % t == 0)
    return pl.pallas_call(
        _mla_kernel,
        grid=(b, sq // tq, sk // tk),
        in_specs=[pl.BlockSpec((None, nh, tq, dq), lambda b_, i, j: (b_, 0, i, 0)),
                  pl.BlockSpec((None, tk, dq), lambda b_, i, j: (b_, j, 0)),
                  pl.BlockSpec((nh, MLA_KV_RANK, MLA_V_DIM), lambda b_, i, j: (0, 0, 0))],
        out_specs=pl.BlockSpec((None, tq, nh * MLA_V_DIM), lambda b_, i, j: (b_, i, 0)),
        out_shape=jax.ShapeDtypeStruct((b, sq, nh * MLA_V_DIM), BF16),
        scratch_shapes=[pltpu.VMEM((nh * tq, 1), F32), pltpu.VMEM((nh * tq, 1), F32),
                        pltpu.VMEM((nh * tq, MLA_KV_RANK), F32)],
        compiler_params=_params("arbitrary", "arbitrary", "arbitrary"),
        name="mla",
    )(q, kcat, wv)


def _gla_kernel(qk_ref, v_ref, lr_ref, wd_ref, bd_ref, s0_ref, o_ref, sfin_ref, la_ref, st_ref):
    d = pl.program_id(1)
    i = pl.program_id(2)
    tc = qk_ref.shape[0]
    n_chunk = tc // GLA_CHUNK
    nk = GLA_HEADS * GLA_DK

    @pl.when(i == 0)
    def _():
        st_ref[...] = s0_ref[...]

    z = _dot(lr_ref[...], wd_ref[...]) + bd_ref[...]
    la_ref[...] = jax.nn.log_sigmoid(z) * (1.0 / GLA_TAU)

    row = lax.broadcasted_iota(I32, (GLA_CHUNK, GLA_CHUNK), 0)
    col = lax.broadcasted_iota(I32, (GLA_CHUNK, GLA_CHUNK), 1)
    tri = (row - col) * (1 - 2 * d) >= 0
    tri_b = jnp.where(tri, 1.0, 0.0).astype(BF16)

    def chunk(c, carry):
        cc = c + d * (n_chunk - 1 - 2 * c)
        off = pl.multiple_of(cc * GLA_CHUNK, GLA_CHUNK)
        la = la_ref[pl.ds(off, GLA_CHUNK), :]
        la_hi, la_lo = _split_bf16(la)
        bc = _dot(tri_b, la_hi) + _dot(tri_b, la_lo)
        bend = jnp.sum(la, axis=0, keepdims=True)
        qk = qk_ref[pl.ds(off, GLA_CHUNK), :].astype(F32)
        q = qk[:, 0:nk]
        k = qk[:, nk:2 * nk]
        v = v_ref[pl.ds(off, GLA_CHUNK), :]
        qd = (q * (GLA_DK ** -0.5) * jnp.exp(bc)).astype(BF16)
        ki = (k * jnp.exp(-bc)).astype(BF16)
        kd = (k * jnp.exp(bend - bc)).astype(BF16)
        dec = jnp.exp(bend)
        for h in range(GLA_HEADS):
            ks = slice(h * GLA_DK, (h + 1) * GLA_DK)
            vs = slice(h * GLA_DV, (h + 1) * GLA_DV)
            vh = v[:, vs]
            sc = jnp.where(tri, _dot_nt(qd[:, ks], ki[:, ks]), 0.0)
            st = st_ref[h]
            o = _dot(sc.astype(BF16), vh) + _dot_nt(qd[:, ks], st.astype(BF16))
            o_ref[pl.ds(off, GLA_CHUNK), vs] = o
            st_ref[h] = st * dec[:, ks] + _dot_tn(vh, kd[:, ks])
        return carry

    lax.fori_loop(0, n_chunk, chunk, 0)

    @pl.when(i == pl.num_programs(2) - 1)
    def _():
        sfin_ref[...] = st_ref[...]


def _gla(p, wd_pad, bd, s0):
    b, s, _ = p.shape
    tc = min(512, s)
    n_step = s // tc
    rows = lambda b_, d, i: i + d * (n_step - 1 - 2 * i)
    return pl.pallas_call(
        _gla_kernel,
        grid=(b, 2, n_step),
        in_specs=[pl.BlockSpec((None, tc, P_BLK), lambda b_, d, i: (b_, rows(b_, d, i), 4)),
                  pl.BlockSpec((None, tc, P_BLK), lambda b_, d, i: (b_, rows(b_, d, i), 5)),
                  pl.BlockSpec((None, tc, 128), lambda b_, d, i: (b_, rows(b_, d, i), 9 * P_BLK // 128)),
                  pl.BlockSpec((None, 128, 256), lambda b_, d, i: (d, 0, 0)),
                  pl.BlockSpec((None, 1, 256), lambda b_, d, i: (d, 0, 0)),
                  pl.BlockSpec((None, None, GLA_HEADS, GLA_DV, GLA_DK), lambda b_, d, i: (b_, d, 0, 0, 0))],
        out_specs=[pl.BlockSpec((None, None, tc, 512), lambda b_, d, i: (d, b_, rows(b_, d, i), 0)),
                   pl.BlockSpec((None, None, GLA_HEADS, GLA_DV, GLA_DK), lambda b_, d, i: (b_, d, 0, 0, 0))],
        out_shape=[jax.ShapeDtypeStruct((2, b, s, 512), F32),
                   jax.ShapeDtypeStruct((b, 2, GLA_HEADS, GLA_DV, GLA_DK), F32)],
        scratch_shapes=[pltpu.VMEM((tc, 256), F32), pltpu.VMEM((GLA_HEADS, GLA_DV, GLA_DK), F32)],
        compiler_params=_params("arbitrary", "arbitrary", "arbitrary"),
        name="gla",
    )(p, p, p, wd_pad, bd, s0)


def _smlp_kernel(u_ref, v_ref, g_ref, ws_ref, bs_ref, o_ref):
    n_chunk = u_ref.shape[0] // SMLP_CHUNK
    g = g_ref[...]
    for c in range(n_chunk):
        rs = slice(c * SMLP_CHUNK, (c + 1) * SMLP_CHUNK)
        vb = _rms(jax.nn.gelu(v_ref[rs, :].astype(F32)), g).astype(BF16)
        u = jax.nn.gelu(u_ref[rs, :].astype(F32))
        for k in range(SMLP_GROUPS):
            cs = slice(k * SMLP_GROUP_DIM, (k + 1) * SMLP_GROUP_DIM)
            mixed = _dot(ws_ref[k], vb[:, cs]) + bs_ref[:, cs]
            o_ref[rs, cs] = (u[:, cs] * mixed).astype(BF16)


def _smlp(p, v_norm, ws, bs_full):
    b, s, _ = p.shape
    tc = min(512, s)
    return pl.pallas_call(
        _smlp_kernel,
        grid=(b, s // tc),
        in_specs=[pl.BlockSpec((None, tc, P_BLK), lambda b_, i: (b_, i, 7)),
                  pl.BlockSpec((None, tc, P_BLK), lambda b_, i: (b_, i, 8)),
                  pl.BlockSpec((1, 512), lambda b_, i: (0, 0)),
                  pl.BlockSpec((SMLP_GROUPS, SMLP_CHUNK, SMLP_CHUNK), lambda b_, i: (0, 0, 0)),
                  pl.BlockSpec((SMLP_CHUNK, 512), lambda b_, i: (0, 0))],
        out_specs=pl.BlockSpec((None, tc, 512), lambda b_, i: (b_, i, 0)),
        out_shape=jax.ShapeDtypeStruct((b, s, 512), BF16),
        compiler_params=_params("arbitrary", "arbitrary"),
        name="smlp",
    )(p, p, v_norm, ws, bs_full)


def _merge_kernel(h_ref, oa_ref, om_ref, gf_ref, gb_ref, gr_ref, os_ref, gn_ref,
                  wg_ref, bg_ref, wb_ref, y_ref, og_ref):
    @pl.when(pl.program_id(2) == 0)
    def _():
        o = gf_ref[...] + gb_ref[...]
        r = gr_ref[...].astype(F32)
        gate = r * jax.nn.sigmoid(r)
        for hh in range(GLA_HEADS):
            vs = slice(hh * GLA_DV, (hh + 1) * GLA_DV)
            og_ref[:, vs] = (_rms(o[:, vs], gn_ref[...]) * gate[:, vs]).astype(BF16)

    h = h_ref[...]
    branches = (oa_ref[...], om_ref[...], og_ref[...], os_ref[...])
    y = None
    for n, o in enumerate(branches):
        gate = jax.nn.sigmoid(_dot(h, wg_ref[n]) + bg_ref[n])
        t = gate * _dot(o, wb_ref[n])
        y = t if y is None else y + t
    y_ref[...] = y.astype(BF16)


def _merge(h, o_a, o_m, o_g2, p, o_s, gla_norm, wg, bg, wb):
    b, s, d = h.shape
    tm = min(512, s)
    tn = 512
    row = lambda b_, i, j: (b_, i, 0)
    return pl.pallas_call(
        _merge_kernel,
        grid=(b, s // tm, d // tn),
        in_specs=[pl.BlockSpec((None, tm, d), row),
                  pl.BlockSpec((None, tm, 512), row),
                  pl.BlockSpec((None, tm, 512), row),
                  pl.BlockSpec((None, None, tm, 512), lambda b_, i, j: (0, b_, i, 0)),
                  pl.BlockSpec((None, None, tm, 512), lambda b_, i, j: (1, b_, i, 0)),
                  pl.BlockSpec((None, tm, P_BLK), lambda b_, i, j: (b_, i, 6)),
                  pl.BlockSpec((None, tm, 512), row),
                  pl.BlockSpec((1, GLA_DV), lambda b_, i, j: (0, 0)),
                  pl.BlockSpec((4, d, tn), lambda b_, i, j: (0, 0, j)),
                  pl.BlockSpec((4, 1, tn), lambda b_, i, j: (0, 0, j)),
                  pl.BlockSpec((4, 512, tn), lambda b_, i, j: (0, 0, j))],
        out_specs=pl.BlockSpec((None, tm, tn), lambda b_, i, j: (b_, i, j)),
        out_shape=jax.ShapeDtypeStruct((b, s, d), BF16),
        scratch_shapes=[pltpu.VMEM((tm, 512), BF16)],
        compiler_params=_params("arbitrary", "arbitrary", "arbitrary"),
        name="merge",
    )(h, o_a, o_m, o_g2, o_g2, p, o_s, gla_norm, wg, bg, wb)


def _outproj_kernel(y_ref, x_ref, mod_ref, g_ref, wo_ref, wr_ref, br_ref, xo_ref, h2_ref, lg_ref):
    z = _dot(y_ref[...], wo_ref[...])
    xn = x_ref[...] + mod_ref[2:3, :] * _rms(z, g_ref[1:2, :])
    xo_ref[...] = xn
    h2 = _rms(xn, g_ref[2:3, :]) * (1.0 + mod_ref[4:5, :]) + mod_ref[3:4, :]
    h2_ref[...] = h2
    h_hi, h_lo = _split_bf16(h2)
    w_hi, w_lo = _split_bf16(wr_ref[...])
    lg_ref[...] = _dot(h_hi, w_hi) + _dot(h_lo, w_hi) + _dot(h_hi, w_lo) + br_ref[...]


def _outproj(y, x, mod, norm_g, w_out, wr_pad, br_pad):
    b, s, d = x.shape
    tm = min(256, s)
    row = lambda b_, i: (b_, i, 0)
    return pl.pallas_call(
        _outproj_kernel,
        grid=(b, s // tm),
        in_specs=[pl.BlockSpec((None, tm, d), row),
                  pl.BlockSpec((None, tm, d), row),
                  pl.BlockSpec((None, 8, d), lambda b_, i: (b_, 0, 0)),
                  pl.BlockSpec((4, d), lambda b_, i: (0, 0)),
                  pl.BlockSpec((d, d), lambda b_, i: (0, 0)),
                  pl.BlockSpec((d, LANE), lambda b_, i: (0, 0)),
                  pl.BlockSpec((1, LANE), lambda b_, i: (0, 0))],
        out_specs=[pl.BlockSpec((None, tm, d), row),
                   pl.BlockSpec((None, tm, d), row),
                   pl.BlockSpec((None, tm, LANE), row)],
        out_shape=[jax.ShapeDtypeStruct((b, s, d), F32),
                   jax.ShapeDtypeStruct((b, s, d), F32),
                   jax.ShapeDtypeStruct((b, s, LANE), F32)],
        compiler_params=_params("arbitrary", "arbitrary"),
        name="outproj",
    )(y, x, mod, norm_g, w_out, wr_pad, br_pad)


def _route_kernel(lg_ref, idx_ref, w_ref, rank_ref, cnt_ref, carry_ref):
    i = pl.program_id(0)

    @pl.when(i == 0)
    def _():
        carry_ref[...] = jnp.zeros(carry_ref.shape, F32)

    lg = lg_ref[...]
    tm = lg.shape[0]
    lane = lax.broadcasted_iota(I32, lg.shape, 1).astype(F32)
    vals, idxs = [], []
    onehot = jnp.zeros(lg.shape, F32)
    for _ in range(TOP_K):
        m = jnp.max(lg, axis=-1, keepdims=True)
        sel = jnp.min(jnp.where(lg == m, lane, float(LANE)), axis=-1, keepdims=True)
        hit = lane == sel
        vals.append(m)
        idxs.append(sel)
        onehot = jnp.where(hit, 1.0, onehot)
        lg = jnp.where(hit, NEG_BIG, lg)
    es = [jnp.exp(v - vals[0]) for v in vals]
    den = es[0] + es[1] + es[2] + es[3]
    r = lax.broadcasted_iota(I32, (tm, tm), 0)
    c = lax.broadcasted_iota(I32, (tm, tm), 1)
    lower = (c < r).astype(F32).astype(BF16)
    before = _dot(lower, onehot.astype(BF16)) + carry_ref[...]
    for k in range(TOP_K):
        idx_ref[:, k:k + 1] = idxs[k].astype(I32)
        w_ref[:, k:k + 1] = es[k] / den
        rank_ref[:, k:k + 1] = jnp.sum(jnp.where(lane == idxs[k], before, 0.0), axis=-1,
                                       keepdims=True).astype(I32)
    carry_ref[...] = carry_ref[...] + jnp.sum(onehot, axis=0, keepdims=True)
    cnt_ref[...] = carry_ref[...]


def _route(logits):
    t = logits.shape[0]
    tm = 256
    return pl.pallas_call(
        _route_kernel,
        grid=(t // tm,),
        in_specs=[pl.BlockSpec((tm, LANE), lambda i: (i, 0))],
        out_specs=[pl.BlockSpec((tm, TOP_K), lambda i: (i, 0)),
                   pl.BlockSpec((tm, TOP_K), lambda i: (i, 0)),
                   pl.BlockSpec((tm, TOP_K), lambda i: (i, 0)),
                   pl.BlockSpec((1, LANE), lambda i: (0, 0))],
        out_shape=[jax.ShapeDtypeStruct((t, TOP_K), I32),
                   jax.ShapeDtypeStruct((t, TOP_K), F32),
                   jax.ShapeDtypeStruct((t, TOP_K), I32),
                   jax.ShapeDtypeStruct((1, LANE), F32)],
        scratch_shapes=[pltpu.VMEM((1, LANE), F32)],
        compiler_params=_params("arbitrary"),
        name="route",
    )(logits)


def _dispatch_kernel(pad_ref, dest_hbm, src_ref, *rest, zero_pads):
    xs_out, dest_smem, zero_ref, idx_sem, sem = rest[-5:]
    i = pl.program_id(0)
    tm = src_ref.shape[0]
    cp = pltpu.make_async_copy(dest_hbm.at[i], dest_smem, idx_sem)
    cp.start()

    if zero_pads:
        @pl.when(i == 0)
        def _():
            zero_ref[...] = jnp.zeros(zero_ref.shape, zero_ref.dtype)

            def pad_copy(e, r):
                return pltpu.make_async_copy(zero_ref.at[pl.ds(0, 1)],
                                             xs_out.at[pl.ds(pad_ref[e] + r, 1)], sem)

            def start_pads(e, carry):
                def body(r, c):
                    pad_copy(e, r).start()
                    return c
                return lax.fori_loop(0, pad_ref[N_EXPERTS + e], body, carry)

            def wait_pads(e, carry):
                def body(r, c):
                    pad_copy(e, r).wait()
                    return c
                return lax.fori_loop(0, pad_ref[N_EXPERTS + e], body, carry)

            lax.fori_loop(0, N_EXPERTS, start_pads, 0)
            lax.fori_loop(0, N_EXPERTS, wait_pads, 0)

    cp.wait()

    def row_copy(t, k):
        return pltpu.make_async_copy(src_ref.at[pl.ds(t, 1)],
                                     xs_out.at[pl.ds(dest_smem[t * TOP_K + k], 1)], sem)

    def issue(t, carry):
        for k in range(TOP_K):
            row_copy(t, k).start()
        return carry

    def drain(t, carry):
        for k in range(TOP_K):
            row_copy(t, k).wait()
        return carry

    lax.fori_loop(0, tm, issue, 0, unroll=DMA_ISSUE_UNROLL)
    lax.fori_loop(0, tm, drain, 0, unroll=DMA_ISSUE_UNROLL)


def _dispatch(pads, dest, src, xs, n_rows):
    t, d = src.shape
    tm = min(256, t)
    dest2 = dest.reshape(t // tm, tm * TOP_K)
    first = xs is None
    args = (pads, dest2, src) + (() if first else (xs,))
    return pl.pallas_call(
        functools.partial(_dispatch_kernel, zero_pads=first),
        grid=(t // tm,),
        in_specs=[pl.BlockSpec(memory_space=pltpu.SMEM),
                  pl.BlockSpec(memory_space=pl.ANY),
                  pl.BlockSpec((tm, d), lambda i: (i, 0))] + ([] if first else [pl.BlockSpec(memory_space=pl.ANY)]),
        out_specs=pl.BlockSpec(memory_space=pl.ANY),
        out_shape=jax.ShapeDtypeStruct((n_rows, d), src.dtype),
        scratch_shapes=[pltpu.SMEM((tm * TOP_K,), I32), pltpu.VMEM((8, d), src.dtype),
                        pltpu.SemaphoreType.DMA, pltpu.SemaphoreType.DMA],
        input_output_aliases={} if first else {3: 0},
        compiler_params=_params("arbitrary"),
        name="dispatch",
    )(*args)


def _expert_kernel(te_ref, nu_ref, x_ref, w1_ref, b1_ref, w2_ref, b2_ref, y_ref):
    @pl.when(pl.program_id(0) < nu_ref[0])
    def _():
        z = _dot(x_ref[...].astype(BF16), w1_ref[...]) + b1_ref[...]
        glu = jnp.minimum(z[:, :D_EXPERT], SWIGLU_LIMIT)
        lin = jnp.clip(z[:, D_EXPERT:], -SWIGLU_LIMIT, SWIGLU_LIMIT)
        act = glu * jax.nn.sigmoid(SWIGLU_ALPHA * glu) * (lin + 1.0)
        y_ref[...] = _dot(act.astype(BF16), w2_ref[...]) + b2_ref[...]


def _experts(tile_e, n_used, xs, w1, b1, w2, b2):
    n_rows, d = xs.shape
    tr = EXPERT_TILE
    n_tiles = n_rows // tr
    row = lambda i, te, nu: (jnp.minimum(i, nu[0] - 1), 0)
    wsel = lambda i, te, nu: (te[i], 0, 0)
    return pl.pallas_call(
        _expert_kernel,
        grid_spec=pltpu.PrefetchScalarGridSpec(
            num_scalar_prefetch=2,
            grid=(n_tiles,),
            in_specs=[pl.BlockSpec((tr, d), row),
                      pl.BlockSpec((None, d, 2 * D_EXPERT), wsel),
                      pl.BlockSpec((None, 1, 2 * D_EXPERT), wsel),
                      pl.BlockSpec((None, D_EXPERT, d), wsel),
                      pl.BlockSpec((None, 1, d), wsel)],
            out_specs=pl.BlockSpec((tr, d), row)),
        out_shape=jax.ShapeDtypeStruct((n_rows, d), F32),
        compiler_params=_params("arbitrary"),
        name="experts",
    )(tile_e, n_used, xs, w1, b1, w2, b2)


def _combine_kernel(dest_hbm, w_ref, x_ref, mod_ref, g_ref, ys_hbm, o_ref, dest_smem, buf, idx_sem, sem):
    tm = x_ref.shape[0]
    i = pl.program_id(0) * pl.num_programs(1) + pl.program_id(1)
    cp = pltpu.make_async_copy(dest_hbm.at[i], dest_smem, idx_sem)
    cp.start()
    cp.wait()

    def row_copy(t, k):
        return pltpu.make_async_copy(ys_hbm.at[pl.ds(dest_smem[t * TOP_K + k], 1)],
                                     buf.at[k, pl.ds(t, 1)], sem)

    def issue(t, carry):
        for k in range(TOP_K):
            row_copy(t, k).start()
        return carry

    def drain(t, carry):
        for k in range(TOP_K):
            row_copy(t, k).wait()
        return carry

    lax.fori_loop(0, tm, issue, 0, unroll=DMA_ISSUE_UNROLL)
    lax.fori_loop(0, tm, drain, 0, unroll=DMA_ISSUE_UNROLL)
    w = w_ref[...]
    f = w[:, 0:1] * buf[0]
    for k in range(1, TOP_K):
        f = f + w[:, k:k + 1] * buf[k]
    o_ref[...] = x_ref[...] + mod_ref[5:6, :] * _rms(f, g_ref[3:4, :])


def _combine(dest, w, x, mod, norm_g, ys):
    b, s, d = x.shape
    tm = min(128, s)
    ns = s // tm
    dest2 = dest.reshape(b * ns, tm * TOP_K)
    return pl.pallas_call(
        _combine_kernel,
        grid=(b, ns),
        in_specs=[pl.BlockSpec(memory_space=pl.ANY),
                  pl.BlockSpec((None, tm, TOP_K), lambda b_, i: (b_, i, 0)),
                  pl.BlockSpec((None, tm, d), lambda b_, i: (b_, i, 0)),
                  pl.BlockSpec((None, 8, d), lambda b_, i: (b_, 0, 0)),
                  pl.BlockSpec((4, d), lambda b_, i: (0, 0)),
                  pl.BlockSpec(memory_space=pl.ANY)],
        out_specs=pl.BlockSpec((None, tm, d), lambda b_, i: (b_, i, 0)),
        out_shape=jax.ShapeDtypeStruct((b, s, d), F32),
        scratch_shapes=[pltpu.SMEM((tm * TOP_K,), I32), pltpu.VMEM((TOP_K, tm, d), F32),
                        pltpu.SemaphoreType.DMA, pltpu.SemaphoreType.DMA],
        compiler_params=_params("arbitrary", "arbitrary"),
        name="combine",
    )(dest2, w.reshape(b, s, TOP_K), x, mod, norm_g, ys)


def _rot_half_cols(w, head_dim):
    d_in, n = w.shape
    w3 = w.reshape(d_in, n // head_dim, head_dim)
    half = head_dim // 2
    return jnp.concatenate([-w3[..., half:], w3[..., :half]], axis=-1).reshape(d_in, n)


def _extended_in_weights(w_in):
    sizes = (512, 128, 128, MLA_Q_RANK, MLA_KV_RANK, MLA_ROPE_DIM, 256, 256, 512,
             2 * GLA_DECAY_RANK, 512, 512, 512)
    cols, off = [], 0
    for n in sizes:
        cols.append(w_in[:, off:off + n])
        off += n
    (a_q, a_k, a_v, m_cq, m_ckv, m_kr, g_q, g_k, g_v, g_lr, g_r, s_u, s_v) = cols
    pad = jnp.zeros((w_in.shape[0], P_BLK - 2 * GLA_DECAY_RANK), w_in.dtype)
    ext = jnp.concatenate(
        [a_q, _rot_half_cols(a_q, SWA_HEAD_DIM),
         a_k, _rot_half_cols(a_k, SWA_HEAD_DIM), a_v, m_kr, _rot_half_cols(m_kr, MLA_ROPE_DIM),
         m_cq, m_ckv, g_q, g_k, g_v, g_r, s_u, s_v, g_lr, pad], axis=1)
    return ext.astype(BF16)


def _rope_tables(n_tok):
    n_rows = n_tok // GRID_W
    rows = jnp.repeat(jnp.arange(n_rows, dtype=F32), GRID_W)
    cols = jnp.tile(jnp.arange(GRID_W, dtype=F32), n_rows)
    n_freq = SWA_HEAD_DIM // 4
    inv_freq = ROPE_BASE ** (-jnp.arange(n_freq, dtype=F32) / n_freq)
    ang = jnp.concatenate([rows[:, None] * inv_freq, cols[:, None] * inv_freq], axis=-1)
    cos = jnp.concatenate([jnp.cos(ang), jnp.cos(ang)], axis=-1)
    sin = jnp.concatenate([jnp.sin(ang), jnp.sin(ang)], axis=-1)
    return jnp.tile(cos, (1, 8)), jnp.tile(sin, (1, 8))


def _moe(h2_parts, logits_parts, w1, b1, w2, b2):
    d = h2_parts[0].shape[1]
    logits = jnp.concatenate(logits_parts, axis=0)
    t_all = logits.shape[0]
    top_idx, top_w, rank, counts = _route(logits)
    counts = counts[0, :N_EXPERTS].astype(I32)
    tr = EXPERT_TILE
    padded = (counts + tr - 1) // tr * tr
    p_end = jnp.cumsum(padded)
    p_start = p_end - padded
    dest = p_start[top_idx] + rank
    n_tiles = -(-t_all * TOP_K // tr) + N_EXPERTS
    tile_start = jnp.arange(n_tiles, dtype=I32) * tr
    tile_e = jnp.minimum(jnp.sum((p_end[None, :] <= tile_start[:, None]).astype(I32), axis=1),
                         N_EXPERTS - 1)
    n_used = (p_end[-1:] // tr).astype(I32)
    pads = jnp.concatenate([p_start + counts, padded - counts]).astype(I32)
    xs = None
    off = 0
    for part in h2_parts:
        n = part.shape[0]
        xs = _dispatch(pads, dest[off:off + n], part, xs, n_tiles * tr)
        off += n
    ys = _experts(tile_e, n_used, xs, w1, b1, w2, b2)
    return dest, top_w, ys


def kernel(x, c, ctx, c_ctx, ada_w, ada_b, norm_g, w_in, attn_sink, mla_q_norm, mla_w_uq, mla_kv_norm, mla_w_ukv, gla_w_decay, gla_b_decay, gla_out_norm, smlp_v_norm, smlp_w_spatial, smlp_b_spatial, w_branch, w_gate, b_gate, w_out, router_w, router_b, expert_w1, expert_b1, expert_w2, expert_b2):
    bsz, seq, d = x.shape
    n_ctx = ctx.shape[1]
    depth = ada_w.shape[0]

    cs = jnp.zeros((8, d), F32).at[:bsz].set(c).at[bsz].set(c_ctx)
    mod_all = _ada(cs, ada_w, ada_b).reshape(depth, 8, N_MOD, d)
    mod_all = jnp.pad(mod_all, ((0, 0), (0, 0), (0, 8 - N_MOD), (0, 0)))

    cos_t, sin_t = _rope_tables(seq)
    cos_c = jnp.ones((bsz * n_ctx, 512), F32)
    sin_c = jnp.zeros((bsz * n_ctx, 512), F32)
    ctx = ctx.reshape(1, bsz * n_ctx, d)

    for l in range(depth):
        last = l == depth - 1
        mod = mod_all[l, :bsz]
        mod_c = mod_all[l, bsz:bsz + 1]
        g_l = norm_g[l]
        w_ext = _extended_in_weights(w_in[l])
        uq = mla_w_uq[l].reshape(MLA_Q_RANK, MLA_HEADS, MLA_NOPE_DIM + MLA_ROPE_DIM)
        uq_rope = uq[:, :, MLA_NOPE_DIM:].reshape(MLA_Q_RANK, MLA_HEADS * MLA_ROPE_DIM)
        wuq_ext = jnp.concatenate(
            [uq[:, :, :MLA_NOPE_DIM].reshape(MLA_Q_RANK, MLA_HEADS * MLA_NOPE_DIM), uq_rope,
             _rot_half_cols(uq_rope, MLA_ROPE_DIM)], axis=1).astype(BF16)
        ukv = mla_w_ukv[l].reshape(MLA_KV_RANK, MLA_HEADS, MLA_NOPE_DIM + MLA_V_DIM)
        wkn = jnp.transpose(ukv[:, :, :MLA_NOPE_DIM], (1, 0, 2)).astype(BF16)
        wv = jnp.transpose(ukv[:, :, MLA_NOPE_DIM:], (1, 0, 2)).astype(BF16)
        qn_g = mla_q_norm[l].reshape(1, MLA_Q_RANK)
        kv_g = mla_kv_norm[l].reshape(1, MLA_KV_RANK)
        wd_pad = jnp.zeros((2, 128, 256), F32)
        for dd in range(2):
            wd_pad = wd_pad.at[dd, dd * GLA_DECAY_RANK:(dd + 1) * GLA_DECAY_RANK].set(gla_w_decay[l, dd])
        wd_pad = wd_pad.astype(BF16)
        bd = gla_b_decay[l].reshape(2, 1, 256)
        gla_norm = gla_out_norm[l].reshape(1, GLA_DV)
        v_norm = smlp_v_norm[l].reshape(1, 512)
        ws = smlp_w_spatial[l].astype(BF16)
        bs_full = jnp.repeat(smlp_b_spatial[l].T, SMLP_GROUP_DIM, axis=1)
        wg = w_gate[l].astype(BF16)
        bg = b_gate[l].reshape(4, 1, d)
        wb = w_branch[l].astype(BF16)
        wo = w_out[l].astype(BF16)
        wr_pad = jnp.pad(router_w[l], ((0, 0), (0, LANE - N_EXPERTS)))
        br_pad = jnp.pad(router_b[l].reshape(1, N_EXPERTS), ((0, 0), (0, LANE - N_EXPERTS)),
                         constant_values=NEG_BIG)
        w1 = expert_w1[l].astype(BF16)
        b1 = expert_b1[l].reshape(N_EXPERTS, 1, 2 * D_EXPERT)
        w2 = expert_w2[l].astype(BF16)
        b2 = expert_b2[l].reshape(N_EXPERTS, 1, d)
        sink = attn_sink[l]

        h, p = _inproj(x, mod, g_l, w_ext)
        hc, pc1 = _inproj(ctx, mod_c, g_l, w_ext)
        pc = pc1.reshape(bsz, n_ctx, P_WIDTH)
        ka, kcat, q_m = _prep(p, cos_t, sin_t, qn_g, kv_g, wuq_ext, wkn)
        _, kcat_c, q_mc = _prep(pc1, cos_c, sin_c, qn_g, kv_g, wuq_ext, wkn)
        kcat_c = kcat_c.reshape(bsz, n_ctx, MLA_QK_PAD)

        o_a = _swa(sink, p, ka, cos_t, sin_t, pc)
        o_m = _mla(q_m, jnp.concatenate([kcat, kcat_c], axis=1), wv)
        s0 = jnp.zeros((bsz, 2, GLA_HEADS, GLA_DV, GLA_DK), F32)
        og_c, s_ctx = _gla(pc, wd_pad, bd, s0)
        og, _ = _gla(p, wd_pad, bd, s_ctx)
        o_s = _smlp(p, v_norm, ws, bs_full)
        y = _merge(h, o_a, o_m, og, p, o_s, gla_norm, wg, bg, wb)
        x, h2, lg = _outproj(y, x, mod, g_l, wo, wr_pad, br_pad)
        h2_parts = [h2.reshape(bsz * seq, d)]
        lg_parts = [lg.reshape(bsz * seq, LANE)]

        if not last:
            oc_a = _swa_ctx(sink, pc)
            q_mc = q_mc.reshape(MLA_HEADS, bsz, n_ctx, MLA_QK_PAD).transpose(1, 0, 2, 3)
            oc_m = _mla(q_mc, kcat_c, wv)
            oc_s = _smlp(pc, v_norm, ws, bs_full)
            flat = lambda a: a.reshape(1, bsz * n_ctx, a.shape[-1])
            yc = _merge(hc, flat(oc_a), flat(oc_m), og_c.reshape(2, 1, bsz * n_ctx, 512), pc1,
                        flat(oc_s), gla_norm, wg, bg, wb)
            ctx, h2c, lgc = _outproj(yc, ctx, mod_c, g_l, wo, wr_pad, br_pad)
            h2_parts.append(h2c.reshape(bsz * n_ctx, d))
            lg_parts.append(lgc.reshape(bsz * n_ctx, LANE))

        dest, top_w, ys = _moe(h2_parts, lg_parts, w1, b1, w2, b2)
        n_lat = bsz * seq
        x = _combine(dest[:n_lat], top_w[:n_lat], x, mod, g_l, ys)
        if not last:
            ctx = _combine(dest[n_lat:], top_w[n_lat:], ctx, mod_c, g_l, ys)
    return x
```

```python
import functools

import jax
import jax.numpy as jnp
from jax import lax
from jax.experimental import pallas as pl
from jax.experimental.pallas import tpu as pltpu

F32 = jnp.float32
BF16 = jnp.bfloat16
I32 = jnp.int32

EPS = 1e-6
ROPE_BASE = 10000.0
GRID_W = 64
N_MOD = 6

SWA_HEADS = 8
SWA_KV_HEADS = 2
SWA_GROUP = SWA_HEADS // SWA_KV_HEADS
SWA_HEAD_DIM = 64
SWA_BLOCK = 128

MLA_HEADS = 4
MLA_Q_RANK = 384
MLA_KV_RANK = 128
MLA_NOPE_DIM = 128
MLA_ROPE_DIM = 64
MLA_V_DIM = 128
MLA_QK_PAD = 256
MLA_QUERY_CHUNK = 512
LOG2_E = 1.4426950408889634

GLA_HEADS = 4
GLA_DK = 64
GLA_DV = 128
GLA_DECAY_RANK = 16
GLA_TAU = 16.0
GLA_CHUNK = 64

SMLP_GROUPS = 4
SMLP_CHUNK = 128
SMLP_GROUP_DIM = 128

N_EXPERTS = 32
TOP_K = 4
D_EXPERT = 1024
SWIGLU_LIMIT = 7.0
SWIGLU_ALPHA = 1.702

LANE = 128
NEG_BIG = -1e30
VMEM_LIMIT = 56 * 1024 * 1024

P_BLK = 512
P_NBLK = 10
P_WIDTH = P_BLK * P_NBLK
EXPERT_TILE = 256
DMA_ISSUE_UNROLL = 8


def _dot(a, b):
    return jnp.dot(a, b, preferred_element_type=F32)


def _dot_nt(a, b):
    return lax.dot_general(a, b, (((1,), (1,)), ((), ())), preferred_element_type=F32)


def _dot_tn(a, b):
    return lax.dot_general(a, b, (((0,), (0,)), ((), ())), preferred_element_type=F32)


def _split_bf16(a):
    hi = a.astype(BF16)
    lo = (a - hi.astype(F32)).astype(BF16)
    return hi, lo


def _rms(x, g):
    return x * lax.rsqrt(jnp.mean(x * x, axis=-1, keepdims=True) + EPS) * g


def _params(*sem):
    return pltpu.CompilerParams(dimension_semantics=sem, vmem_limit_bytes=VMEM_LIMIT)


def _ada_kernel(c_ref, w_ref, b_ref, o_ref):
    c = c_ref[...]
    a_hi, a_lo = _split_bf16(c * jax.nn.sigmoid(c))
    w_hi, w_lo = _split_bf16(w_ref[...])
    o_ref[...] = _dot(a_hi, w_hi) + _dot(a_lo, w_hi) + _dot(a_hi, w_lo) + b_ref[...]


def _ada(cs, ada_w, ada_b):
    n_layer, d, n = ada_w.shape
    tn = 1024
    return pl.pallas_call(
        _ada_kernel,
        grid=(n_layer, n // tn),
        in_specs=[pl.BlockSpec((8, d), lambda l, j: (0, 0)),
                  pl.BlockSpec((None, d, tn), lambda l, j: (l, 0, j)),
                  pl.BlockSpec((None, 1, tn), lambda l, j: (l, 0, j))],
        out_specs=pl.BlockSpec((None, 8, tn), lambda l, j: (l, 0, j)),
        out_shape=jax.ShapeDtypeStruct((n_layer, 8, n), F32),
        compiler_params=_params("arbitrary", "arbitrary"),
        name="ada",
    )(cs, ada_w, ada_b.reshape(n_layer, 1, n))


def _inproj_kernel(x_ref, mod_ref, g_ref, w_ref, h_ref, p_ref, hs_ref):
    @pl.when(pl.program_id(2) == 0)
    def _():
        y = _rms(x_ref[...], g_ref[0:1, :])
        hb = (y * (1.0 + mod_ref[1:2, :]) + mod_ref[0:1, :]).astype(BF16)
        hs_ref[...] = hb
        h_ref[...] = hb

    p_ref[...] = _dot(hs_ref[...], w_ref[...]).astype(BF16)


def _inproj(x, mod, norm_g, w_ext):
    b, s, d = x.shape
    tm = min(512, s)
    tn = 1024
    return pl.pallas_call(
        _inproj_kernel,
        grid=(b, s // tm, P_WIDTH // tn),
        in_specs=[pl.BlockSpec((None, tm, d), lambda b_, i, j: (b_, i, 0)),
                  pl.BlockSpec((None, 8, d), lambda b_, i, j: (b_, 0, 0)),
                  pl.BlockSpec((4, d), lambda b_, i, j: (0, 0)),
                  pl.BlockSpec((d, tn), lambda b_, i, j: (0, j))],
        out_specs=[pl.BlockSpec((None, tm, d), lambda b_, i, j: (b_, i, 0)),
                   pl.BlockSpec((None, tm, tn), lambda b_, i, j: (b_, i, j))],
        out_shape=[jax.ShapeDtypeStruct((b, s, d), BF16),
                   jax.ShapeDtypeStruct((b, s, P_WIDTH), BF16)],
        scratch_shapes=[pltpu.VMEM((tm, d), BF16)],
        compiler_params=_params("arbitrary", "arbitrary", "arbitrary"),
        name="inproj",
    )(x, mod, norm_g, w_ext)


def _prep_kernel(kblk_ref, cblk_ref, cos_ref, sin_ref, qn_g_ref, kv_g_ref, wuq_ref, wkn_ref,
                 ka_ref, kcat_ref, q_ref):
    cos = cos_ref[...]
    sin = sin_ref[...]
    kb = kblk_ref[...].astype(F32)
    ka_ref[...] = (kb[:, 0:128] * cos[:, 0:128] + kb[:, 128:256] * sin[:, 0:128]).astype(BF16)
    kr = kb[:, 384:448] * cos[:, 0:64] + kb[:, 448:512] * sin[:, 0:64]
    cb = cblk_ref[...].astype(F32)
    ckvn = _rms(cb[:, MLA_Q_RANK:], kv_g_ref[...])
    tm = kb.shape[0]
    kcat_ref[:, 0:128] = ckvn.astype(BF16)
    kcat_ref[:, 128:192] = kr.astype(BF16)
    kcat_ref[:, 192:256] = jnp.zeros((tm, 64), BF16)
    cqn = _rms(cb[:, :MLA_Q_RANK], qn_g_ref[...]).astype(BF16)
    qa = _dot(cqn, wuq_ref[...])
    qr = qa[:, 512:768] * cos[:, 0:256] + qa[:, 768:1024] * sin[:, 0:256]
    scale = (MLA_NOPE_DIM + MLA_ROPE_DIM) ** -0.5 * LOG2_E
    for h in range(MLA_HEADS):
        qn = qa[:, h * 128:(h + 1) * 128].astype(BF16)
        q_ref[h, :, 0:128] = (_dot_nt(qn, wkn_ref[h]) * scale).astype(BF16)
        q_ref[h, :, 128:192] = (qr[:, h * 64:(h + 1) * 64] * scale).astype(BF16)
        q_ref[h, :, 192:256] = jnp.zeros((tm, 64), BF16)


def _prep(p, cos_t, sin_t, qn_g, kv_g, wuq_ext, wkn):
    b, s, _ = p.shape
    tm = min(512, s)
    return pl.pallas_call(
        _prep_kernel,
        grid=(b, s // tm),
        in_specs=[pl.BlockSpec((None, tm, P_BLK), lambda b_, i: (b_, i, 2)),
                  pl.BlockSpec((None, tm, P_BLK), lambda b_, i: (b_, i, 3)),
                  pl.BlockSpec((tm, 512), lambda b_, i: (i, 0)),
                  pl.BlockSpec((tm, 512), lambda b_, i: (i, 0)),
                  pl.BlockSpec((1, MLA_Q_RANK), lambda b_, i: (0, 0)),
                  pl.BlockSpec((1, MLA_KV_RANK), lambda b_, i: (0, 0)),
                  pl.BlockSpec((MLA_Q_RANK, 1024), lambda b_, i: (0, 0)),
                  pl.BlockSpec((MLA_HEADS, MLA_KV_RANK, MLA_NOPE_DIM), lambda b_, i: (0, 0, 0))],
        out_specs=[pl.BlockSpec((None, tm, 128), lambda b_, i: (b_, i, 0)),
                   pl.BlockSpec((None, tm, MLA_QK_PAD), lambda b_, i: (b_, i, 0)),
                   pl.BlockSpec((None, MLA_HEADS, tm, MLA_QK_PAD), lambda b_, i: (b_, 0, i, 0))],
        out_shape=[jax.ShapeDtypeStruct((b, s, 128), BF16),
                   jax.ShapeDtypeStruct((b, s, MLA_QK_PAD), BF16),
                   jax.ShapeDtypeStruct((b, MLA_HEADS, s, MLA_QK_PAD), BF16)],
        compiler_params=_params("arbitrary", "arbitrary"),
        name="prep",
    )(p, p, cos_t, sin_t, qn_g, kv_g, wuq_ext, wkn)


def _swa_group(q, sink_ref, g, k_all, v_all, mask, o_ref):
    heads = range(g * SWA_GROUP, (g + 1) * SWA_GROUP)
    n_q = q.shape[0]
    qs = jnp.concatenate([q[:, h * SWA_HEAD_DIM:(h + 1) * SWA_HEAD_DIM] for h in heads], axis=0)
    sink = jnp.concatenate([jnp.full((n_q, 1), sink_ref[h], F32) for h in heads], axis=0)
    s = _dot_nt(qs, k_all)
    if mask is not None:
        s = s + jnp.concatenate([mask] * SWA_GROUP, axis=0)
    m = jnp.maximum(jnp.max(s, axis=-1, keepdims=True), sink)
    e = jnp.exp(s - m)
    den = jnp.sum(e, axis=-1, keepdims=True) + jnp.exp(sink - m)
    o = _dot(e.astype(BF16), v_all) / den
    for n, h in enumerate(heads):
        o_ref[:, h * SWA_HEAD_DIM:(h + 1) * SWA_HEAD_DIM] = o[n * n_q:(n + 1) * n_q, :].astype(BF16)


def _swa_kernel(sink_ref, q_ref, qr_ref, cos_ref, sin_ref, kp_ref, kc_ref, kn_ref,
                vp_ref, vc_ref, vn_ref, kx_ref, vx_ref, o_ref):
    i = pl.program_id(1)
    nb = pl.num_programs(1)
    scale = SWA_HEAD_DIM ** -0.5
    q = ((q_ref[...].astype(F32) * cos_ref[...] + qr_ref[...].astype(F32) * sin_ref[...])
         * scale).astype(BF16)
    n_ctx = kx_ref.shape[0]
    n_keys = 3 * SWA_BLOCK + n_ctx
    row = lax.broadcasted_iota(I32, (SWA_BLOCK, n_keys), 0)
    col = lax.broadcasted_iota(I32, (SWA_BLOCK, n_keys), 1)
    dist = col - row
    lo = jnp.where(i > 0, 0, n_keys)
    hi = jnp.where(i < nb - 1, 2 * SWA_BLOCK, -n_keys)
    bias_prev = jnp.where(dist >= lo, 0.0, NEG_BIG)
    bias_next = jnp.where(dist <= hi, 0.0, NEG_BIG)
    in_next = (col >= 2 * SWA_BLOCK) & (col < 3 * SWA_BLOCK)
    mask = jnp.where(col < SWA_BLOCK, bias_prev, jnp.where(in_next, bias_next, 0.0))
    for g in range(SWA_KV_HEADS):
        gs = slice(g * SWA_HEAD_DIM, (g + 1) * SWA_HEAD_DIM)
        k_all = jnp.concatenate([kp_ref[:, gs], kc_ref[:, gs], kn_ref[:, gs], kx_ref[:, gs]], axis=0)
        v_all = jnp.concatenate([vp_ref[:, gs], vc_ref[:, gs], vn_ref[:, gs], vx_ref[:, gs]], axis=0)
        _swa_group(q, sink_ref, g, k_all, v_all, mask, o_ref)


def _swa_ctx_kernel(sink_ref, q_ref, kx_ref, vx_ref, o_ref):
    scale = SWA_HEAD_DIM ** -0.5
    q = (q_ref[...].astype(F32) * scale).astype(BF16)
    for g in range(SWA_KV_HEADS):
        gs = slice(g * SWA_HEAD_DIM, (g + 1) * SWA_HEAD_DIM)
        _swa_group(q, sink_ref, g, kx_ref[:, gs], vx_ref[:, gs], None, o_ref)


def _swa(sink, p, ka, cos_t, sin_t, pc):
    b, s, _ = p.shape
    n_ctx = pc.shape[1]
    nb = s // SWA_BLOCK
    blk = SWA_BLOCK
    prev = lambda b_, i: (b_, jnp.maximum(i - 1, 0), 0)
    cur = lambda b_, i: (b_, i, 0)
    nxt = lambda b_, i: (b_, jnp.minimum(i + 1, nb - 1), 0)
    vcol = (2 * P_BLK + 256) // 128
    vprev = lambda b_, i: (b_, jnp.maximum(i - 1, 0), vcol)
    vcur = lambda b_, i: (b_, i, vcol)
    vnxt = lambda b_, i: (b_, jnp.minimum(i + 1, nb - 1), vcol)
    return pl.pallas_call(
        _swa_kernel,
        grid=(b, nb),
        in_specs=[pl.BlockSpec(memory_space=pltpu.SMEM),
                  pl.BlockSpec((None, blk, P_BLK), lambda b_, i: (b_, i, 0)),
                  pl.BlockSpec((None, blk, P_BLK), lambda b_, i: (b_, i, 1)),
                  pl.BlockSpec((blk, 512), lambda b_, i: (i, 0)),
                  pl.BlockSpec((blk, 512), lambda b_, i: (i, 0)),
                  pl.BlockSpec((None, blk, 128), prev),
                  pl.BlockSpec((None, blk, 128), cur),
                  pl.BlockSpec((None, blk, 128), nxt),
                  pl.BlockSpec((None, blk, 128), vprev),
                  pl.BlockSpec((None, blk, 128), vcur),
                  pl.BlockSpec((None, blk, 128), vnxt),
                  pl.BlockSpec((None, n_ctx, 128), lambda b_, i: (b_, 0, 2 * P_BLK // 128)),
                  pl.BlockSpec((None, n_ctx, 128), lambda b_, i: (b_, 0, vcol))],
        out_specs=pl.BlockSpec((None, blk, 512), lambda b_, i: (b_, i, 0)),
        out_shape=jax.ShapeDtypeStruct((b, s, 512), BF16),
        compiler_params=_params("arbitrary", "arbitrary"),
        name="swa",
    )(sink, p, p, cos_t, sin_t, ka, ka, ka, p, p, p, pc, pc)


def _swa_ctx(sink, pc):
    b, n_ctx, _ = pc.shape
    blk = SWA_BLOCK
    vcol = (2 * P_BLK + 256) // 128
    return pl.pallas_call(
        _swa_ctx_kernel,
        grid=(b, n_ctx // blk),
        in_specs=[pl.BlockSpec(memory_space=pltpu.SMEM),
                  pl.BlockSpec((None, blk, P_BLK), lambda b_, i: (b_, i, 0)),
                  pl.BlockSpec((None, n_ctx, 128), lambda b_, i: (b_, 0, 2 * P_BLK // 128)),
                  pl.BlockSpec((None, n_ctx, 128), lambda b_, i: (b_, 0, vcol))],
        out_specs=pl.BlockSpec((None, blk, 512), lambda b_, i: (b_, i, 0)),
        out_shape=jax.ShapeDtypeStruct((b, n_ctx, 512), BF16),
        compiler_params=_params("arbitrary", "arbitrary"),
        name="swa_ctx",
    )(sink, pc, pc, pc)


def _mla_kernel(q_ref, kv_ref, vt_ref, wv_ref, o_ref, m_ref, l_ref, acc_ref):
    j = pl.program_id(2)

    @pl.when(j == 0)
    def _():
        m_ref[...] = jnp.full(m_ref.shape, NEG_BIG, F32)
        l_ref[...] = jnp.zeros(l_ref.shape, F32)
        acc_ref[...] = jnp.zeros(acc_ref.shape, F32)

    nh, tq, dq = q_ref.shape
    kv = kv_ref[...]
    vt = vt_ref[...]
    qc = min(MLA_QUERY_CHUNK, tq)
    per_head = tq // qc
    n_chunk = nh * per_head

    def scores(c):
        h, r = divmod(c, per_head)
        return _dot_nt(kv, q_ref[h, r * qc:(r + 1) * qc, :])

    def accumulate(pending):
        cols, alpha, p = pending
        acc_ref[:, cols] = alpha * acc_ref[:, cols] + _dot(vt, p)

    s_next = scores(0)
    pending = None
    for c in range(n_chunk):
        s = s_next
        if c + 1 < n_chunk:
            s_next = scores(c + 1)
        if pending is not None:
            accumulate(pending)
        cols = slice(c * qc, (c + 1) * qc)
        m_prev = m_ref[:, cols]
        m_new = jnp.maximum(m_prev, jnp.max(s, axis=0, keepdims=True))
        alpha = jnp.exp2(m_prev - m_new)
        p = jnp.exp2(s - m_new)
        l_ref[:, cols] = alpha * l_ref[:, cols] + jnp.sum(p, axis=0, keepdims=True)
        m_ref[:, cols] = m_new
        pending = (cols, alpha, p.astype(BF16))
    accumulate(pending)

    @pl.when(j == pl.num_programs(2) - 1)
    def _():
        o = (acc_ref[...] / l_ref[...]).astype(BF16)
        for h in range(nh):
            o_ref[:, h * MLA_V_DIM:(h + 1) * MLA_V_DIM] = _dot_tn(
                o[:, h * tq:(h + 1) * tq], wv_ref[h]).astype(BF16)


def _mla(q, kcat, wv):
    b, nh, sq, dq = q.shape
    sk = kcat.shape[1]
    tq = min(512, sq)
    tk = next(t for t in (1280, 640, 512, 384, 256, 128) if sk % t == 0)
    vt = jnp.swapaxes(kcat[:, :, :MLA_KV_RANK], 1, 2)
    return pl.pallas_call(
        _mla_kernel,
        grid=(b, sq // tq, sk // tk),
        in_specs=[pl.BlockSpec((None, nh, tq, dq), lambda b_, i, j: (b_, 0, i, 0)),
                  pl.BlockSpec((None, tk, dq), lambda b_, i, j: (b_, j, 0)),
                  pl.BlockSpec((None, MLA_KV_RANK, tk), lambda b_, i, j: (b_, 0, j)),
                  pl.BlockSpec((nh, MLA_KV_RANK, MLA_V_DIM), lambda b_, i, j: (0, 0, 0))],
        out_specs=pl.BlockSpec((None, tq, nh * MLA_V_DIM), lambda b_, i, j: (b_, i, 0)),
        out_shape=jax.ShapeDtypeStruct((b, sq, nh * MLA_V_DIM), BF16),
        scratch_shapes=[pltpu.VMEM((1, nh * tq), F32), pltpu.VMEM((1, nh * tq), F32),
                        pltpu.VMEM((MLA_KV_RANK, nh * tq), F32)],
        compiler_params=_params("arbitrary", "arbitrary", "arbitrary"),
        name="mla",
    )(q, kcat, vt, wv)


def _gla_kernel(qk_ref, v_ref, lr_ref, wd_ref, bd_ref, s0_ref, o_ref, sfin_ref, la_ref, st_ref):
    d = pl.program_id(1)
    i = pl.program_id(2)
    tc = qk_ref.shape[0]
    n_chunk = tc // GLA_CHUNK
    nk = GLA_HEADS * GLA_DK

    @pl.when(i == 0)
    def _():
        st_ref[...] = s0_ref[...]

    z = _dot(lr_ref[...], wd_ref[...]) + bd_ref[...]
    la_ref[...] = jax.nn.log_sigmoid(z) * (1.0 / GLA_TAU)

    row = lax.broadcasted_iota(I32, (GLA_CHUNK, GLA_CHUNK), 0)
    col = lax.broadcasted_iota(I32, (GLA_CHUNK, GLA_CHUNK), 1)
    tri = (row - col) * (1 - 2 * d) >= 0
    tri_b = jnp.where(tri, 1.0, 0.0).astype(BF16)

    def chunk(c, carry):
        cc = c + d * (n_chunk - 1 - 2 * c)
        off = pl.multiple_of(cc * GLA_CHUNK, GLA_CHUNK)
        la = la_ref[pl.ds(off, GLA_CHUNK), :]
        la_hi, la_lo = _split_bf16(la)
        bc = _dot(tri_b, la_hi) + _dot(tri_b, la_lo)
        bend = jnp.sum(la, axis=0, keepdims=True)
        qk = qk_ref[pl.ds(off, GLA_CHUNK), :].astype(F32)
        q = qk[:, 0:nk]
        k = qk[:, nk:2 * nk]
        v = v_ref[pl.ds(off, GLA_CHUNK), :]
        qd = (q * (GLA_DK ** -0.5) * jnp.exp(bc)).astype(BF16)
        ki = (k * jnp.exp(-bc)).astype(BF16)
        kd = (k * jnp.exp(bend - bc)).astype(BF16)
        dec = jnp.exp(bend)
        for h in range(GLA_HEADS):
            ks = slice(h * GLA_DK, (h + 1) * GLA_DK)
            vs = slice(h * GLA_DV, (h + 1) * GLA_DV)
            vh = v[:, vs]
            sc = jnp.where(tri, _dot_nt(qd[:, ks], ki[:, ks]), 0.0)
            st = st_ref[h]
            o = _dot(sc.astype(BF16), vh) + _dot_nt(qd[:, ks], st.astype(BF16))
            o_ref[pl.ds(off, GLA_CHUNK), vs] = o
            st_ref[h] = st * dec[:, ks] + _dot_tn(vh, kd[:, ks])
        return carry

    lax.fori_loop(0, n_chunk, chunk, 0)

    @pl.when(i == pl.num_programs(2) - 1)
    def _():
        sfin_ref[...] = st_ref[...]


def _gla(p, wd_pad, bd, s0):
    b, s, _ = p.shape
    tc = min(512, s)
    n_step = s // tc
    rows = lambda b_, d, i: i + d * (n_step - 1 - 2 * i)
    return pl.pallas_call(
        _gla_kernel,
        grid=(b, 2, n_step),
        in_specs=[pl.BlockSpec((None, tc, P_BLK), lambda b_, d, i: (b_, rows(b_, d, i), 4)),
                  pl.BlockSpec((None, tc, P_BLK), lambda b_, d, i: (b_, rows(b_, d, i), 5)),
                  pl.BlockSpec((None, tc, 128), lambda b_, d, i: (b_, rows(b_, d, i), 9 * P_BLK // 128)),
                  pl.BlockSpec((None, 128, 256), lambda b_, d, i: (d, 0, 0)),
                  pl.BlockSpec((None, 1, 256), lambda b_, d, i: (d, 0, 0)),
                  pl.BlockSpec((None, None, GLA_HEADS, GLA_DV, GLA_DK), lambda b_, d, i: (b_, d, 0, 0, 0))],
        out_specs=[pl.BlockSpec((None, None, tc, 512), lambda b_, d, i: (d, b_, rows(b_, d, i), 0)),
                   pl.BlockSpec((None, None, GLA_HEADS, GLA_DV, GLA_DK), lambda b_, d, i: (b_, d, 0, 0, 0))],
        out_shape=[jax.ShapeDtypeStruct((2, b, s, 512), F32),
                   jax.ShapeDtypeStruct((b, 2, GLA_HEADS, GLA_DV, GLA_DK), F32)],
        scratch_shapes=[pltpu.VMEM((tc, 256), F32), pltpu.VMEM((GLA_HEADS, GLA_DV, GLA_DK), F32)],
        compiler_params=_params("arbitrary", "arbitrary", "arbitrary"),
        name="gla",
    )(p, p, p, wd_pad, bd, s0)


def _smlp_kernel(u_ref, v_ref, g_ref, ws_ref, bs_ref, o_ref):
    n_chunk = u_ref.shape[0] // SMLP_CHUNK
    g = g_ref[...]
    for c in range(n_chunk):
        rs = slice(c * SMLP_CHUNK, (c + 1) * SMLP_CHUNK)
        vb = _rms(jax.nn.gelu(v_ref[rs, :].astype(F32)), g).astype(BF16)
        u = jax.nn.gelu(u_ref[rs, :].astype(F32))
        for k in range(SMLP_GROUPS):
            cs = slice(k * SMLP_GROUP_DIM, (k + 1) * SMLP_GROUP_DIM)
            mixed = _dot(ws_ref[k], vb[:, cs]) + bs_ref[:, cs]
            o_ref[rs, cs] = (u[:, cs] * mixed).astype(BF16)


def _smlp(p, v_norm, ws, bs_full):
    b, s, _ = p.shape
    tc = min(512, s)
    return pl.pallas_call(
        _smlp_kernel,
        grid=(b, s // tc),
        in_specs=[pl.BlockSpec((None, tc, P_BLK), lambda b_, i: (b_, i, 7)),
                  pl.BlockSpec((None, tc, P_BLK), lambda b_, i: (b_, i, 8)),
                  pl.BlockSpec((1, 512), lambda b_, i: (0, 0)),
                  pl.BlockSpec((SMLP_GROUPS, SMLP_CHUNK, SMLP_CHUNK), lambda b_, i: (0, 0, 0)),
                  pl.BlockSpec((SMLP_CHUNK, 512), lambda b_, i: (0, 0))],
        out_specs=pl.BlockSpec((None, tc, 512), lambda b_, i: (b_, i, 0)),
        out_shape=jax.ShapeDtypeStruct((b, s, 512), BF16),
        compiler_params=_params("arbitrary", "arbitrary"),
        name="smlp",
    )(p, p, v_norm, ws, bs_full)


def _merge_kernel(h_ref, oa_ref, om_ref, gf_ref, gb_ref, gr_ref, os_ref, gn_ref,
                  wg_ref, bg_ref, wb_ref, y_ref, og_ref):
    @pl.when(pl.program_id(2) == 0)
    def _():
        o = gf_ref[...] + gb_ref[...]
        r = gr_ref[...].astype(F32)
        gate = r * jax.nn.sigmoid(r)
        for hh in range(GLA_HEADS):
            vs = slice(hh * GLA_DV, (hh + 1) * GLA_DV)
            og_ref[:, vs] = (_rms(o[:, vs], gn_ref[...]) * gate[:, vs]).astype(BF16)

    h = h_ref[...]
    branches = (oa_ref[...], om_ref[...], og_ref[...], os_ref[...])
    y = None
    for n, o in enumerate(branches):
        gate = jax.nn.sigmoid(_dot(h, wg_ref[n]) + bg_ref[n])
        t = gate * _dot(o, wb_ref[n])
        y = t if y is None else y + t
    y_ref[...] = y.astype(BF16)


def _merge(h, o_a, o_m, o_g2, p, o_s, gla_norm, wg, bg, wb):
    b, s, d = h.shape
    tm = min(512, s)
    tn = 512
    row = lambda b_, i, j: (b_, i, 0)
    return pl.pallas_call(
        _merge_kernel,
        grid=(b, s // tm, d // tn),
        in_specs=[pl.BlockSpec((None, tm, d), row),
                  pl.BlockSpec((None, tm, 512), row),
                  pl.BlockSpec((None, tm, 512), row),
                  pl.BlockSpec((None, None, tm, 512), lambda b_, i, j: (0, b_, i, 0)),
                  pl.BlockSpec((None, None, tm, 512), lambda b_, i, j: (1, b_, i, 0)),
                  pl.BlockSpec((None, tm, P_BLK), lambda b_, i, j: (b_, i, 6)),
                  pl.BlockSpec((None, tm, 512), row),
                  pl.BlockSpec((1, GLA_DV), lambda b_, i, j: (0, 0)),
                  pl.BlockSpec((4, d, tn), lambda b_, i, j: (0, 0, j)),
                  pl.BlockSpec((4, 1, tn), lambda b_, i, j: (0, 0, j)),
                  pl.BlockSpec((4, 512, tn), lambda b_, i, j: (0, 0, j))],
        out_specs=pl.BlockSpec((None, tm, tn), lambda b_, i, j: (b_, i, j)),
        out_shape=jax.ShapeDtypeStruct((b, s, d), BF16),
        scratch_shapes=[pltpu.VMEM((tm, 512), BF16)],
        compiler_params=_params("arbitrary", "arbitrary", "arbitrary"),
        name="merge",
    )(h, o_a, o_m, o_g2, o_g2, p, o_s, gla_norm, wg, bg, wb)


def _outproj_kernel(y_ref, x_ref, mod_ref, g_ref, wo_ref, wr_ref, br_ref, xo_ref, h2_ref, lg_ref):
    z = _dot(y_ref[...], wo_ref[...])
    xn = x_ref[...] + mod_ref[2:3, :] * _rms(z, g_ref[1:2, :])
    xo_ref[...] = xn
    h2 = _rms(xn, g_ref[2:3, :]) * (1.0 + mod_ref[4:5, :]) + mod_ref[3:4, :]
    h2_ref[...] = h2
    h_hi, h_lo = _split_bf16(h2)
    w_hi, w_lo = _split_bf16(wr_ref[...])
    lg_ref[...] = _dot(h_hi, w_hi) + _dot(h_lo, w_hi) + _dot(h_hi, w_lo) + br_ref[...]


def _outproj(y, x, mod, norm_g, w_out, wr_pad, br_pad):
    b, s, d = x.shape
    tm = min(256, s)
    row = lambda b_, i: (b_, i, 0)
    return pl.pallas_call(
        _outproj_kernel,
        grid=(b, s // tm),
        in_specs=[pl.BlockSpec((None, tm, d), row),
                  pl.BlockSpec((None, tm, d), row),
                  pl.BlockSpec((None, 8, d), lambda b_, i: (b_, 0, 0)),
                  pl.BlockSpec((4, d), lambda b_, i: (0, 0)),
                  pl.BlockSpec((d, d), lambda b_, i: (0, 0)),
                  pl.BlockSpec((d, LANE), lambda b_, i: (0, 0)),
                  pl.BlockSpec((1, LANE), lambda b_, i: (0, 0))],
        out_specs=[pl.BlockSpec((None, tm, d), row),
                   pl.BlockSpec((None, tm, d), row),
                   pl.BlockSpec((None, tm, LANE), row)],
        out_shape=[jax.ShapeDtypeStruct((b, s, d), F32),
                   jax.ShapeDtypeStruct((b, s, d), F32),
                   jax.ShapeDtypeStruct((b, s, LANE), F32)],
        compiler_params=_params("arbitrary", "arbitrary"),
        name="outproj",
    )(y, x, mod, norm_g, w_out, wr_pad, br_pad)


def _route_kernel(lg_ref, idx_ref, w_ref, rank_ref, cnt_ref, carry_ref):
    i = pl.program_id(0)

    @pl.when(i == 0)
    def _():
        carry_ref[...] = jnp.zeros(carry_ref.shape, F32)

    lg = lg_ref[...]
    tm = lg.shape[0]
    lane = lax.broadcasted_iota(I32, lg.shape, 1).astype(F32)
    vals, idxs = [], []
    onehot = jnp.zeros(lg.shape, F32)
    for _ in range(TOP_K):
        m = jnp.max(lg, axis=-1, keepdims=True)
        sel = jnp.min(jnp.where(lg == m, lane, float(LANE)), axis=-1, keepdims=True)
        hit = lane == sel
        vals.append(m)
        idxs.append(sel)
        onehot = jnp.where(hit, 1.0, onehot)
        lg = jnp.where(hit, NEG_BIG, lg)
    es = [jnp.exp(v - vals[0]) for v in vals]
    den = es[0] + es[1] + es[2] + es[3]
    r = lax.broadcasted_iota(I32, (tm, tm), 0)
    c = lax.broadcasted_iota(I32, (tm, tm), 1)
    lower = (c < r).astype(F32).astype(BF16)
    before = _dot(lower, onehot.astype(BF16)) + carry_ref[...]
    for k in range(TOP_K):
        idx_ref[:, k:k + 1] = idxs[k].astype(I32)
        w_ref[:, k:k + 1] = es[k] / den
        rank_ref[:, k:k + 1] = jnp.sum(jnp.where(lane == idxs[k], before, 0.0), axis=-1,
                                       keepdims=True).astype(I32)
    carry_ref[...] = carry_ref[...] + jnp.sum(onehot, axis=0, keepdims=True)
    cnt_ref[...] = carry_ref[...]


def _route(logits):
    t = logits.shape[0]
    tm = 256
    return pl.pallas_call(
        _route_kernel,
        grid=(t // tm,),
        in_specs=[pl.BlockSpec((tm, LANE), lambda i: (i, 0))],
        out_specs=[pl.BlockSpec((tm, TOP_K), lambda i: (i, 0)),
                   pl.BlockSpec((tm, TOP_K), lambda i: (i, 0)),
                   pl.BlockSpec((tm, TOP_K), lambda i: (i, 0)),
                   pl.BlockSpec((1, LANE), lambda i: (0, 0))],
        out_shape=[jax.ShapeDtypeStruct((t, TOP_K), I32),
                   jax.ShapeDtypeStruct((t, TOP_K), F32),
                   jax.ShapeDtypeStruct((t, TOP_K), I32),
                   jax.ShapeDtypeStruct((1, LANE), F32)],
        scratch_shapes=[pltpu.VMEM((1, LANE), F32)],
        compiler_params=_params("arbitrary"),
        name="route",
    )(logits)


def _dispatch_kernel(pad_ref, dest_hbm, src_ref, *rest, zero_pads):
    xs_out, dest_smem, zero_ref, idx_sem, sem = rest[-5:]
    i = pl.program_id(0)
    tm = src_ref.shape[0]
    cp = pltpu.make_async_copy(dest_hbm.at[i], dest_smem, idx_sem)
    cp.start()

    if zero_pads:
        @pl.when(i == 0)
        def _():
            zero_ref[...] = jnp.zeros(zero_ref.shape, zero_ref.dtype)

            def pad_copy(e, r):
                return pltpu.make_async_copy(zero_ref.at[pl.ds(0, 1)],
                                             xs_out.at[pl.ds(pad_ref[e] + r, 1)], sem)

            def start_pads(e, carry):
                def body(r, c):
                    pad_copy(e, r).start()
                    return c
                return lax.fori_loop(0, pad_ref[N_EXPERTS + e], body, carry)

            def wait_pads(e, carry):
                def body(r, c):
                    pad_copy(e, r).wait()
                    return c
                return lax.fori_loop(0, pad_ref[N_EXPERTS + e], body, carry)

            lax.fori_loop(0, N_EXPERTS, start_pads, 0)
            lax.fori_loop(0, N_EXPERTS, wait_pads, 0)

    cp.wait()

    def row_copy(t, k):
        return pltpu.make_async_copy(src_ref.at[pl.ds(t, 1)],
                                     xs_out.at[pl.ds(dest_smem[t * TOP_K + k], 1)], sem)

    def issue(t, carry):
        for k in range(TOP_K):
            row_copy(t, k).start()
        return carry

    def drain(t, carry):
        for k in range(TOP_K):
            row_copy(t, k).wait()
        return carry

    lax.fori_loop(0, tm, issue, 0, unroll=DMA_ISSUE_UNROLL)
    lax.fori_loop(0, tm, drain, 0, unroll=DMA_ISSUE_UNROLL)


def _dispatch(pads, dest, src, xs, n_rows):
    t, d = src.shape
    tm = min(512, t)
    dest2 = dest.reshape(t // tm, tm * TOP_K)
    first = xs is None
    args = (pads, dest2, src) + (() if first else (xs,))
    return pl.pallas_call(
        functools.partial(_dispatch_kernel, zero_pads=first),
        grid=(t // tm,),
        in_specs=[pl.BlockSpec(memory_space=pltpu.SMEM),
                  pl.BlockSpec(memory_space=pl.ANY),
                  pl.BlockSpec((tm, d), lambda i: (i, 0))] + ([] if first else [pl.BlockSpec(memory_space=pl.ANY)]),
        out_specs=pl.BlockSpec(memory_space=pl.ANY),
        out_shape=jax.ShapeDtypeStruct((n_rows, d), src.dtype),
        scratch_shapes=[pltpu.SMEM((tm * TOP_K,), I32), pltpu.VMEM((8, d), src.dtype),
                        pltpu.SemaphoreType.DMA, pltpu.SemaphoreType.DMA],
        input_output_aliases={} if first else {3: 0},
        compiler_params=_params("arbitrary"),
        name="dispatch",
    )(*args)


def _expert_kernel(te_ref, nu_ref, x_ref, w1_ref, b1_ref, w2_ref, b2_ref, y_ref):
    @pl.when(pl.program_id(0) < nu_ref[0])
    def _():
        z = _dot(x_ref[...].astype(BF16), w1_ref[...]) + b1_ref[...]
        glu = jnp.minimum(z[:, :D_EXPERT], SWIGLU_LIMIT)
        lin = jnp.clip(z[:, D_EXPERT:], -SWIGLU_LIMIT, SWIGLU_LIMIT)
        act = glu * jax.nn.sigmoid(SWIGLU_ALPHA * glu) * (lin + 1.0)
        y_ref[...] = _dot(act.astype(BF16), w2_ref[...]) + b2_ref[...]


def _experts(layer, tile_e, n_used, xs, w1, b1, w2, b2):
    n_rows, d = xs.shape
    tr = EXPERT_TILE
    n_tiles = n_rows // tr
    row = lambda i, te, nu: (jnp.minimum(i, nu[0] - 1), 0)
    wsel = lambda i, te, nu: (layer, te[i], 0, 0)
    return pl.pallas_call(
        _expert_kernel,
        grid_spec=pltpu.PrefetchScalarGridSpec(
            num_scalar_prefetch=2,
            grid=(n_tiles,),
            in_specs=[pl.BlockSpec((tr, d), row),
                      pl.BlockSpec((None, None, d, 2 * D_EXPERT), wsel),
                      pl.BlockSpec((None, None, 1, 2 * D_EXPERT), wsel),
                      pl.BlockSpec((None, None, D_EXPERT, d), wsel),
                      pl.BlockSpec((None, None, 1, d), wsel)],
            out_specs=pl.BlockSpec((tr, d), row)),
        out_shape=jax.ShapeDtypeStruct((n_rows, d), F32),
        compiler_params=_params("arbitrary"),
        name="experts",
    )(tile_e, n_used, xs, w1, b1, w2, b2)


def _combine_kernel(dest_hbm, w_ref, x_ref, mod_ref, g_ref, ys_hbm, o_ref, dest_smem, buf, idx_sem, sem):
    tm = x_ref.shape[0]
    i = pl.program_id(0) * pl.num_programs(1) + pl.program_id(1)
    cp = pltpu.make_async_copy(dest_hbm.at[i], dest_smem, idx_sem)
    cp.start()
    cp.wait()

    def row_copy(t, k):
        return pltpu.make_async_copy(ys_hbm.at[pl.ds(dest_smem[t * TOP_K + k], 1)],
                                     buf.at[k, pl.ds(t, 1)], sem)

    def issue(t, carry):
        for k in range(TOP_K):
            row_copy(t, k).start()
        return carry

    def drain(t, carry):
        for k in range(TOP_K):
            row_copy(t, k).wait()
        return carry

    lax.fori_loop(0, tm, issue, 0, unroll=DMA_ISSUE_UNROLL)
    lax.fori_loop(0, tm, drain, 0, unroll=DMA_ISSUE_UNROLL)
    w = w_ref[...]
    f = w[:, 0:1] * buf[0]
    for k in range(1, TOP_K):
        f = f + w[:, k:k + 1] * buf[k]
    o_ref[...] = x_ref[...] + mod_ref[5:6, :] * _rms(f, g_ref[3:4, :])


def _combine(dest, w, x, mod, norm_g, ys):
    b, s, d = x.shape
    tm = min(256, s)
    ns = s // tm
    dest2 = dest.reshape(b * ns, tm * TOP_K)
    return pl.pallas_call(
        _combine_kernel,
        grid=(b, ns),
        in_specs=[pl.BlockSpec(memory_space=pl.ANY),
                  pl.BlockSpec((None, tm, TOP_K), lambda b_, i: (b_, i, 0)),
                  pl.BlockSpec((None, tm, d), lambda b_, i: (b_, i, 0)),
                  pl.BlockSpec((None, 8, d), lambda b_, i: (b_, 0, 0)),
                  pl.BlockSpec((4, d), lambda b_, i: (0, 0)),
                  pl.BlockSpec(memory_space=pl.ANY)],
        out_specs=pl.BlockSpec((None, tm, d), lambda b_, i: (b_, i, 0)),
        out_shape=jax.ShapeDtypeStruct((b, s, d), F32),
        scratch_shapes=[pltpu.SMEM((tm * TOP_K,), I32), pltpu.VMEM((TOP_K, tm, d), F32),
                        pltpu.SemaphoreType.DMA, pltpu.SemaphoreType.DMA],
        compiler_params=_params("arbitrary", "arbitrary"),
        name="combine",
    )(dest2, w.reshape(b, s, TOP_K), x, mod, norm_g, ys)


def _rot_half_cols(w, head_dim):
    d_in, n = w.shape
    w3 = w.reshape(d_in, n // head_dim, head_dim)
    half = head_dim // 2
    return jnp.concatenate([-w3[..., half:], w3[..., :half]], axis=-1).reshape(d_in, n)


def _extended_in_weights(w_in):
    sizes = (512, 128, 128, MLA_Q_RANK, MLA_KV_RANK, MLA_ROPE_DIM, 256, 256, 512,
             2 * GLA_DECAY_RANK, 512, 512, 512)
    cols, off = [], 0
    for n in sizes:
        cols.append(w_in[:, off:off + n])
        off += n
    (a_q, a_k, a_v, m_cq, m_ckv, m_kr, g_q, g_k, g_v, g_lr, g_r, s_u, s_v) = cols
    pad = jnp.zeros((w_in.shape[0], P_BLK - 2 * GLA_DECAY_RANK), w_in.dtype)
    ext = jnp.concatenate(
        [a_q, _rot_half_cols(a_q, SWA_HEAD_DIM),
         a_k, _rot_half_cols(a_k, SWA_HEAD_DIM), a_v, m_kr, _rot_half_cols(m_kr, MLA_ROPE_DIM),
         m_cq, m_ckv, g_q, g_k, g_v, g_r, s_u, s_v, g_lr, pad], axis=1)
    return ext.astype(BF16)


def _rope_tables(n_tok):
    n_rows = n_tok // GRID_W
    rows = jnp.repeat(jnp.arange(n_rows, dtype=F32), GRID_W)
    cols = jnp.tile(jnp.arange(GRID_W, dtype=F32), n_rows)
    n_freq = SWA_HEAD_DIM // 4
    inv_freq = ROPE_BASE ** (-jnp.arange(n_freq, dtype=F32) / n_freq)
    ang = jnp.concatenate([rows[:, None] * inv_freq, cols[:, None] * inv_freq], axis=-1)
    cos = jnp.concatenate([jnp.cos(ang), jnp.cos(ang)], axis=-1)
    sin = jnp.concatenate([jnp.sin(ang), jnp.sin(ang)], axis=-1)
    return jnp.tile(cos, (1, 8)), jnp.tile(sin, (1, 8))


def _moe(layer, h2_parts, logits_parts, w1, b1, w2, b2):
    d = h2_parts[0].shape[1]
    logits = jnp.concatenate(logits_parts, axis=0)
    t_all = logits.shape[0]
    top_idx, top_w, rank, counts = _route(logits)
    counts = counts[0, :N_EXPERTS].astype(I32)
    tr = EXPERT_TILE
    padded = (counts + tr - 1) // tr * tr
    p_end = jnp.cumsum(padded)
    p_start = p_end - padded
    dest = p_start[top_idx] + rank
    n_tiles = -(-t_all * TOP_K // tr) + N_EXPERTS
    tile_start = jnp.arange(n_tiles, dtype=I32) * tr
    tile_e = jnp.minimum(jnp.sum((p_end[None, :] <= tile_start[:, None]).astype(I32), axis=1),
                         N_EXPERTS - 1)
    n_used = (p_end[-1:] // tr).astype(I32)
    pads = jnp.concatenate([p_start + counts, padded - counts]).astype(I32)
    xs = None
    off = 0
    for part in h2_parts:
        n = part.shape[0]
        xs = _dispatch(pads, dest[off:off + n], part, xs, n_tiles * tr)
        off += n
    ys = _experts(layer, tile_e, n_used, xs, w1, b1, w2, b2)
    return dest, top_w, ys


def kernel(x, c, ctx, c_ctx, ada_w, ada_b, norm_g, w_in, attn_sink, mla_q_norm, mla_w_uq, mla_kv_norm, mla_w_ukv, gla_w_decay, gla_b_decay, gla_out_norm, smlp_v_norm, smlp_w_spatial, smlp_b_spatial, w_branch, w_gate, b_gate, w_out, router_w, router_b, expert_w1, expert_b1, expert_w2, expert_b2):
    bsz, seq, d = x.shape
    n_ctx = ctx.shape[1]
    depth = ada_w.shape[0]

    cs = jnp.zeros((8, d), F32).at[:bsz].set(c).at[bsz].set(c_ctx)
    mod_all = _ada(cs, ada_w, ada_b).reshape(depth, 8, N_MOD, d)
    mod_all = jnp.pad(mod_all, ((0, 0), (0, 0), (0, 8 - N_MOD), (0, 0)))

    cos_t, sin_t = _rope_tables(seq)
    cos_c = jnp.ones((bsz * n_ctx, 512), F32)
    sin_c = jnp.zeros((bsz * n_ctx, 512), F32)
    ctx = ctx.reshape(1, bsz * n_ctx, d)

    w1 = expert_w1.astype(BF16)
    b1 = expert_b1.reshape(depth, N_EXPERTS, 1, 2 * D_EXPERT)
    w2 = expert_w2.astype(BF16)
    b2 = expert_b2.reshape(depth, N_EXPERTS, 1, d)

    for l in range(depth):
        last = l == depth - 1
        mod = mod_all[l, :bsz]
        mod_c = mod_all[l, bsz:bsz + 1]
        g_l = norm_g[l]
        w_ext = _extended_in_weights(w_in[l])
        uq = mla_w_uq[l].reshape(MLA_Q_RANK, MLA_HEADS, MLA_NOPE_DIM + MLA_ROPE_DIM)
        uq_rope = uq[:, :, MLA_NOPE_DIM:].reshape(MLA_Q_RANK, MLA_HEADS * MLA_ROPE_DIM)
        wuq_ext = jnp.concatenate(
            [uq[:, :, :MLA_NOPE_DIM].reshape(MLA_Q_RANK, MLA_HEADS * MLA_NOPE_DIM), uq_rope,
             _rot_half_cols(uq_rope, MLA_ROPE_DIM)], axis=1).astype(BF16)
        ukv = mla_w_ukv[l].reshape(MLA_KV_RANK, MLA_HEADS, MLA_NOPE_DIM + MLA_V_DIM)
        wkn = jnp.transpose(ukv[:, :, :MLA_NOPE_DIM], (1, 0, 2)).astype(BF16)
        wv = jnp.transpose(ukv[:, :, MLA_NOPE_DIM:], (1, 0, 2)).astype(BF16)
        qn_g = mla_q_norm[l].reshape(1, MLA_Q_RANK)
        kv_g = mla_kv_norm[l].reshape(1, MLA_KV_RANK)
        wd_pad = jnp.zeros((2, 128, 256), F32)
        for dd in range(2):
            wd_pad = wd_pad.at[dd, dd * GLA_DECAY_RANK:(dd + 1) * GLA_DECAY_RANK].set(gla_w_decay[l, dd])
        wd_pad = wd_pad.astype(BF16)
        bd = gla_b_decay[l].reshape(2, 1, 256)
        gla_norm = gla_out_norm[l].reshape(1, GLA_DV)
        v_norm = smlp_v_norm[l].reshape(1, 512)
        ws = smlp_w_spatial[l].astype(BF16)
        bs_full = jnp.repeat(smlp_b_spatial[l].T, SMLP_GROUP_DIM, axis=1)
        wg = w_gate[l].astype(BF16)
        bg = b_gate[l].reshape(4, 1, d)
        wb = w_branch[l].astype(BF16)
        wo = w_out[l].astype(BF16)
        wr_pad = jnp.pad(router_w[l], ((0, 0), (0, LANE - N_EXPERTS)))
        br_pad = jnp.pad(router_b[l].reshape(1, N_EXPERTS), ((0, 0), (0, LANE - N_EXPERTS)),
                         constant_values=NEG_BIG)
        sink = attn_sink[l]

        h, p = _inproj(x, mod, g_l, w_ext)
        hc, pc1 = _inproj(ctx, mod_c, g_l, w_ext)
        pc = pc1.reshape(bsz, n_ctx, P_WIDTH)
        ka, kcat, q_m = _prep(p, cos_t, sin_t, qn_g, kv_g, wuq_ext, wkn)
        _, kcat_c, q_mc = _prep(pc1, cos_c, sin_c, qn_g, kv_g, wuq_ext, wkn)
        kcat_c = kcat_c.reshape(bsz, n_ctx, MLA_QK_PAD)

        o_a = _swa(sink, p, ka, cos_t, sin_t, pc)
        o_m = _mla(q_m, jnp.concatenate([kcat, kcat_c], axis=1), wv)
        s0 = jnp.zeros((bsz, 2, GLA_HEADS, GLA_DV, GLA_DK), F32)
        og_c, s_ctx = _gla(pc, wd_pad, bd, s0)
        og, _ = _gla(p, wd_pad, bd, s_ctx)
        o_s = _smlp(p, v_norm, ws, bs_full)
        y = _merge(h, o_a, o_m, og, p, o_s, gla_norm, wg, bg, wb)
        x, h2, lg = _outproj(y, x, mod, g_l, wo, wr_pad, br_pad)
        h2_parts = [h2.reshape(bsz * seq, d)]
        lg_parts = [lg.reshape(bsz * seq, LANE)]

        if not last:
            oc_a = _swa_ctx(sink, pc)
            q_mc = q_mc.reshape(MLA_HEADS, bsz, n_ctx, MLA_QK_PAD).transpose(1, 0, 2, 3)
            oc_m = _mla(q_mc, kcat_c, wv)
            oc_s = _smlp(pc, v_norm, ws, bs_full)
            flat = lambda a: a.reshape(1, bsz * n_ctx, a.shape[-1])
            yc = _merge(hc, flat(oc_a), flat(oc_m), og_c.reshape(2, 1, bsz * n_ctx, 512), pc1,
                        flat(oc_s), gla_norm, wg, bg, wb)
            ctx, h2c, lgc = _outproj(yc, ctx, mod_c, g_l, wo, wr_pad, br_pad)
            h2_parts.append(h2c.reshape(bsz * n_ctx, d))
            lg_parts.append(lgc.reshape(bsz * n_ctx, LANE))

        dest, top_w, ys = _moe(l, h2_parts, lg_parts, w1, b1, w2, b2)
        n_lat = bsz * seq
        x = _combine(dest[:n_lat], top_w[:n_lat], x, mod, g_l, ys)
        if not last:
            ctx = _combine(dest[n_lat:], top_w[n_lat:], ctx, mod_c, g_l, ys)
    return x
```

```python
import functools

import jax
import jax.numpy as jnp
from jax import lax
from jax.experimental import pallas as pl
from jax.experimental.pallas import tpu as pltpu

F32 = jnp.float32
BF16 = jnp.bfloat16
I32 = jnp.int32

EPS = 1e-6
ROPE_BASE = 10000.0
GRID_W = 64
N_MOD = 6

SWA_HEADS = 8
SWA_KV_HEADS = 2
SWA_GROUP = SWA_HEADS // SWA_KV_HEADS
SWA_HEAD_DIM = 64
SWA_BLOCK = 128

MLA_HEADS = 4
MLA_Q_RANK = 384
MLA_KV_RANK = 128
MLA_NOPE_DIM = 128
MLA_ROPE_DIM = 64
MLA_V_DIM = 128
MLA_QK_PAD = 256
MLA_QUERY_CHUNK = 512
MLA_QUERY_TILE = 2048
LOG2_E = 1.4426950408889634

GLA_HEADS = 4
GLA_DK = 64
GLA_DV = 128
GLA_DECAY_RANK = 16
GLA_TAU = 16.0
GLA_CHUNK = 64
GLA_CHUNK_UNROLL = 8

SMLP_GROUPS = 4
SMLP_CHUNK = 128
SMLP_GROUP_DIM = 128

N_EXPERTS = 32
TOP_K = 4
D_EXPERT = 1024
SWIGLU_LIMIT = 7.0
SWIGLU_ALPHA = 1.702

LANE = 128
NEG_BIG = -1e30
VMEM_LIMIT = 56 * 1024 * 1024

P_BLK = 512
P_NBLK = 10
P_WIDTH = P_BLK * P_NBLK
EXPERT_TILE = 256
DMA_ISSUE_UNROLL = 8


def _dot(a, b):
    return jnp.dot(a, b, preferred_element_type=F32)


def _dot_nt(a, b):
    return lax.dot_general(a, b, (((1,), (1,)), ((), ())), preferred_element_type=F32)


def _dot_tn(a, b):
    return lax.dot_general(a, b, (((0,), (0,)), ((), ())), preferred_element_type=F32)


def _split_bf16(a):
    hi = a.astype(BF16)
    lo = (a - hi.astype(F32)).astype(BF16)
    return hi, lo


def _rms(x, g):
    return x * lax.rsqrt(jnp.mean(x * x, axis=-1, keepdims=True) + EPS) * g


def _params(*sem):
    return pltpu.CompilerParams(dimension_semantics=sem, vmem_limit_bytes=VMEM_LIMIT)


def _ada_kernel(c_ref, w_ref, b_ref, o_ref):
    c = c_ref[...]
    a_hi, a_lo = _split_bf16(c * jax.nn.sigmoid(c))
    w_hi, w_lo = _split_bf16(w_ref[...])
    o_ref[...] = _dot(a_hi, w_hi) + _dot(a_lo, w_hi) + _dot(a_hi, w_lo) + b_ref[...]


def _ada(cs, ada_w, ada_b):
    n_layer, d, n = ada_w.shape
    tn = 1024
    return pl.pallas_call(
        _ada_kernel,
        grid=(n_layer, n // tn),
        in_specs=[pl.BlockSpec((8, d), lambda l, j: (0, 0)),
                  pl.BlockSpec((None, d, tn), lambda l, j: (l, 0, j)),
                  pl.BlockSpec((None, 1, tn), lambda l, j: (l, 0, j))],
        out_specs=pl.BlockSpec((None, 8, tn), lambda l, j: (l, 0, j)),
        out_shape=jax.ShapeDtypeStruct((n_layer, 8, n), F32),
        compiler_params=_params("arbitrary", "arbitrary"),
        name="ada",
    )(cs, ada_w, ada_b.reshape(n_layer, 1, n))


def _inproj_kernel(x_ref, mod_ref, g_ref, w_ref, h_ref, p_ref):
    y = _rms(x_ref[...], g_ref[0:1, :])
    hb = (y * (1.0 + mod_ref[1:2, :]) + mod_ref[0:1, :]).astype(BF16)
    h_ref[...] = hb
    p_ref[...] = _dot(hb, w_ref[...]).astype(BF16)


def _inproj(x, mod, norm_g, w_ext):
    b, s, d = x.shape
    tm = min(256, s)
    return pl.pallas_call(
        _inproj_kernel,
        grid=(b, s // tm),
        in_specs=[pl.BlockSpec((None, tm, d), lambda b_, i: (b_, i, 0)),
                  pl.BlockSpec((None, 8, d), lambda b_, i: (b_, 0, 0)),
                  pl.BlockSpec((4, d), lambda b_, i: (0, 0)),
                  pl.BlockSpec((d, P_WIDTH), lambda b_, i: (0, 0), pipeline_mode=pl.Buffered(1))],
        out_specs=[pl.BlockSpec((None, tm, d), lambda b_, i: (b_, i, 0)),
                   pl.BlockSpec((None, tm, P_WIDTH), lambda b_, i: (b_, i, 0))],
        out_shape=[jax.ShapeDtypeStruct((b, s, d), BF16),
                   jax.ShapeDtypeStruct((b, s, P_WIDTH), BF16)],
        compiler_params=_params("arbitrary", "arbitrary"),
        name="inproj",
    )(x, mod, norm_g, w_ext)


def _prep_kernel(kblk_ref, cblk_ref, cos_ref, sin_ref, qn_g_ref, kv_g_ref, wuq_ref, wkn_ref,
                 ka_ref, kcat_ref, q_ref):
    cos = cos_ref[...]
    sin = sin_ref[...]
    kb = kblk_ref[...].astype(F32)
    ka_ref[...] = (kb[:, 0:128] * cos[:, 0:128] + kb[:, 128:256] * sin[:, 0:128]).astype(BF16)
    kr = kb[:, 384:448] * cos[:, 0:64] + kb[:, 448:512] * sin[:, 0:64]
    cb = cblk_ref[...].astype(F32)
    ckvn = _rms(cb[:, MLA_Q_RANK:], kv_g_ref[...])
    tm = kb.shape[0]
    kcat_ref[:, 0:128] = ckvn.astype(BF16)
    kcat_ref[:, 128:192] = kr.astype(BF16)
    kcat_ref[:, 192:256] = jnp.zeros((tm, 64), BF16)
    cqn = _rms(cb[:, :MLA_Q_RANK], qn_g_ref[...]).astype(BF16)
    qa = _dot(cqn, wuq_ref[...])
    qr = qa[:, 512:768] * cos[:, 0:256] + qa[:, 768:1024] * sin[:, 0:256]
    scale = (MLA_NOPE_DIM + MLA_ROPE_DIM) ** -0.5 * LOG2_E
    for h in range(MLA_HEADS):
        qn = qa[:, h * 128:(h + 1) * 128].astype(BF16)
        q_ref[h, :, 0:128] = (_dot_nt(qn, wkn_ref[h]) * scale).astype(BF16)
        q_ref[h, :, 128:192] = (qr[:, h * 64:(h + 1) * 64] * scale).astype(BF16)
        q_ref[h, :, 192:256] = jnp.zeros((tm, 64), BF16)


def _prep(p, cos_t, sin_t, qn_g, kv_g, wuq_ext, wkn):
    b, s, _ = p.shape
    tm = min(512, s)
    return pl.pallas_call(
        _prep_kernel,
        grid=(b, s // tm),
        in_specs=[pl.BlockSpec((None, tm, P_BLK), lambda b_, i: (b_, i, 2)),
                  pl.BlockSpec((None, tm, P_BLK), lambda b_, i: (b_, i, 3)),
                  pl.BlockSpec((tm, 512), lambda b_, i: (i, 0)),
                  pl.BlockSpec((tm, 512), lambda b_, i: (i, 0)),
                  pl.BlockSpec((1, MLA_Q_RANK), lambda b_, i: (0, 0)),
                  pl.BlockSpec((1, MLA_KV_RANK), lambda b_, i: (0, 0)),
                  pl.BlockSpec((MLA_Q_RANK, 1024), lambda b_, i: (0, 0)),
                  pl.BlockSpec((MLA_HEADS, MLA_KV_RANK, MLA_NOPE_DIM), lambda b_, i: (0, 0, 0))],
        out_specs=[pl.BlockSpec((None, tm, 128), lambda b_, i: (b_, i, 0)),
                   pl.BlockSpec((None, tm, MLA_QK_PAD), lambda b_, i: (b_, i, 0)),
                   pl.BlockSpec((None, MLA_HEADS, tm, MLA_QK_PAD), lambda b_, i: (b_, 0, i, 0))],
        out_shape=[jax.ShapeDtypeStruct((b, s, 128), BF16),
                   jax.ShapeDtypeStruct((b, s, MLA_QK_PAD), BF16),
                   jax.ShapeDtypeStruct((b, MLA_HEADS, s, MLA_QK_PAD), BF16)],
        compiler_params=_params("arbitrary", "arbitrary"),
        name="prep",
    )(p, p, cos_t, sin_t, qn_g, kv_g, wuq_ext, wkn)


def _swa_group(q, sink_ref, g, k_all, v_all, mask, o_ref):
    heads = range(g * SWA_GROUP, (g + 1) * SWA_GROUP)
    n_q = q.shape[0]
    qs = jnp.concatenate([q[:, h * SWA_HEAD_DIM:(h + 1) * SWA_HEAD_DIM] for h in heads], axis=0)
    sink = jnp.concatenate([jnp.full((n_q, 1), sink_ref[h], F32) for h in heads], axis=0)
    s = _dot_nt(qs, k_all)
    if mask is not None:
        s = s + jnp.concatenate([mask] * SWA_GROUP, axis=0)
    m = jnp.maximum(jnp.max(s, axis=-1, keepdims=True), sink)
    e = jnp.exp(s - m)
    den = jnp.sum(e, axis=-1, keepdims=True) + jnp.exp(sink - m)
    o = _dot(e.astype(BF16), v_all) / den
    for n, h in enumerate(heads):
        o_ref[:, h * SWA_HEAD_DIM:(h + 1) * SWA_HEAD_DIM] = o[n * n_q:(n + 1) * n_q, :].astype(BF16)


def _swa_kernel(sink_ref, q_ref, qr_ref, cos_ref, sin_ref, kp_ref, kc_ref, kn_ref,
                vp_ref, vc_ref, vn_ref, kx_ref, vx_ref, o_ref):
    i = pl.program_id(1)
    nb = pl.num_programs(1)
    scale = SWA_HEAD_DIM ** -0.5
    q = ((q_ref[...].astype(F32) * cos_ref[...] + qr_ref[...].astype(F32) * sin_ref[...])
         * scale).astype(BF16)
    n_ctx = kx_ref.shape[0]
    n_keys = 3 * SWA_BLOCK + n_ctx
    row = lax.broadcasted_iota(I32, (SWA_BLOCK, n_keys), 0)
    col = lax.broadcasted_iota(I32, (SWA_BLOCK, n_keys), 1)
    dist = col - row
    lo = jnp.where(i > 0, 0, n_keys)
    hi = jnp.where(i < nb - 1, 2 * SWA_BLOCK, -n_keys)
    bias_prev = jnp.where(dist >= lo, 0.0, NEG_BIG)
    bias_next = jnp.where(dist <= hi, 0.0, NEG_BIG)
    in_next = (col >= 2 * SWA_BLOCK) & (col < 3 * SWA_BLOCK)
    mask = jnp.where(col < SWA_BLOCK, bias_prev, jnp.where(in_next, bias_next, 0.0))
    for g in range(SWA_KV_HEADS):
        gs = slice(g * SWA_HEAD_DIM, (g + 1) * SWA_HEAD_DIM)
        k_all = jnp.concatenate([kp_ref[:, gs], kc_ref[:, gs], kn_ref[:, gs], kx_ref[:, gs]], axis=0)
        v_all = jnp.concatenate([vp_ref[:, gs], vc_ref[:, gs], vn_ref[:, gs], vx_ref[:, gs]], axis=0)
        _swa_group(q, sink_ref, g, k_all, v_all, mask, o_ref)


def _swa_ctx_kernel(sink_ref, q_ref, kx_ref, vx_ref, o_ref):
    scale = SWA_HEAD_DIM ** -0.5
    q = (q_ref[...].astype(F32) * scale).astype(BF16)
    for g in range(SWA_KV_HEADS):
        gs = slice(g * SWA_HEAD_DIM, (g + 1) * SWA_HEAD_DIM)
        _swa_group(q, sink_ref, g, kx_ref[:, gs], vx_ref[:, gs], None, o_ref)


def _swa(sink, p, ka, cos_t, sin_t, pc):
    b, s, _ = p.shape
    n_ctx = pc.shape[1]
    nb = s // SWA_BLOCK
    blk = SWA_BLOCK
    prev = lambda b_, i: (b_, jnp.maximum(i - 1, 0), 0)
    cur = lambda b_, i: (b_, i, 0)
    nxt = lambda b_, i: (b_, jnp.minimum(i + 1, nb - 1), 0)
    vcol = (2 * P_BLK + 256) // 128
    vprev = lambda b_, i: (b_, jnp.maximum(i - 1, 0), vcol)
    vcur = lambda b_, i: (b_, i, vcol)
    vnxt = lambda b_, i: (b_, jnp.minimum(i + 1, nb - 1), vcol)
    return pl.pallas_call(
        _swa_kernel,
        grid=(b, nb),
        in_specs=[pl.BlockSpec(memory_space=pltpu.SMEM),
                  pl.BlockSpec((None, blk, P_BLK), lambda b_, i: (b_, i, 0)),
                  pl.BlockSpec((None, blk, P_BLK), lambda b_, i: (b_, i, 1)),
                  pl.BlockSpec((blk, 512), lambda b_, i: (i, 0)),
                  pl.BlockSpec((blk, 512), lambda b_, i: (i, 0)),
                  pl.BlockSpec((None, blk, 128), prev),
                  pl.BlockSpec((None, blk, 128), cur),
                  pl.BlockSpec((None, blk, 128), nxt),
                  pl.BlockSpec((None, blk, 128), vprev),
                  pl.BlockSpec((None, blk, 128), vcur),
                  pl.BlockSpec((None, blk, 128), vnxt),
                  pl.BlockSpec((None, n_ctx, 128), lambda b_, i: (b_, 0, 2 * P_BLK // 128)),
                  pl.BlockSpec((None, n_ctx, 128), lambda b_, i: (b_, 0, vcol))],
        out_specs=pl.BlockSpec((None, blk, 512), lambda b_, i: (b_, i, 0)),
        out_shape=jax.ShapeDtypeStruct((b, s, 512), BF16),
        compiler_params=_params("arbitrary", "arbitrary"),
        name="swa",
    )(sink, p, p, cos_t, sin_t, ka, ka, ka, p, p, p, pc, pc)


def _swa_ctx(sink, pc):
    b, n_ctx, _ = pc.shape
    blk = SWA_BLOCK
    vcol = (2 * P_BLK + 256) // 128
    return pl.pallas_call(
        _swa_ctx_kernel,
        grid=(b, n_ctx // blk),
        in_specs=[pl.BlockSpec(memory_space=pltpu.SMEM),
                  pl.BlockSpec((None, blk, P_BLK), lambda b_, i: (b_, i, 0)),
                  pl.BlockSpec((None, n_ctx, 128), lambda b_, i: (b_, 0, 2 * P_BLK // 128)),
                  pl.BlockSpec((None, n_ctx, 128), lambda b_, i: (b_, 0, vcol))],
        out_specs=pl.BlockSpec((None, blk, 512), lambda b_, i: (b_, i, 0)),
        out_shape=jax.ShapeDtypeStruct((b, n_ctx, 512), BF16),
        compiler_params=_params("arbitrary", "arbitrary"),
        name="swa_ctx",
    )(sink, pc, pc, pc)


def _mla_kernel(q_ref, kv_ref, vt_ref, wv_ref, o_ref, m_ref, l_ref, acc_ref):
    j = pl.program_id(2)

    @pl.when(j == 0)
    def _():
        m_ref[...] = jnp.full(m_ref.shape, NEG_BIG, F32)
        l_ref[...] = jnp.zeros(l_ref.shape, F32)
        acc_ref[...] = jnp.zeros(acc_ref.shape, F32)

    nh, tq, dq = q_ref.shape
    kv = kv_ref[...]
    vt = vt_ref[...]
    qc = min(MLA_QUERY_CHUNK, tq)
    per_head = tq // qc
    n_chunk = nh * per_head

    def scores(c):
        h, r = divmod(c, per_head)
        return _dot_nt(kv, q_ref[h, r * qc:(r + 1) * qc, :])

    def accumulate(pending):
        cols, alpha, p = pending
        acc_ref[:, cols] = alpha * acc_ref[:, cols] + _dot(vt, p)

    s_next = scores(0)
    pending = None
    for c in range(n_chunk):
        s = s_next
        if c + 1 < n_chunk:
            s_next = scores(c + 1)
        if pending is not None:
            accumulate(pending)
        cols = slice(c * qc, (c + 1) * qc)
        m_prev = m_ref[:, cols]
        m_new = jnp.maximum(m_prev, jnp.max(s, axis=0, keepdims=True))
        alpha = jnp.exp2(m_prev - m_new)
        p = jnp.exp2(s - m_new)
        l_ref[:, cols] = alpha * l_ref[:, cols] + jnp.sum(p, axis=0, keepdims=True)
        m_ref[:, cols] = m_new
        pending = (cols, alpha, p.astype(BF16))
    accumulate(pending)

    @pl.when(j == pl.num_programs(2) - 1)
    def _():
        o = (acc_ref[...] / l_ref[...]).astype(BF16)
        for h in range(nh):
            o_ref[:, h * MLA_V_DIM:(h + 1) * MLA_V_DIM] = _dot_tn(
                o[:, h * tq:(h + 1) * tq], wv_ref[h]).astype(BF16)


def _mla(q, kcat, wv):
    b, nh, sq, dq = q.shape
    sk = kcat.shape[1]
    tq = min(MLA_QUERY_TILE, sq)
    tk = next(t for t in (1280, 640, 512, 384, 256, 128) if sk % t == 0)
    vt = jnp.swapaxes(kcat[:, :, :MLA_KV_RANK], 1, 2)
    return pl.pallas_call(
        _mla_kernel,
        grid=(b, sq // tq, sk // tk),
        in_specs=[pl.BlockSpec((None, nh, tq, dq), lambda b_, i, j: (b_, 0, i, 0)),
                  pl.BlockSpec((None, tk, dq), lambda b_, i, j: (b_, j, 0)),
                  pl.BlockSpec((None, MLA_KV_RANK, tk), lambda b_, i, j: (b_, 0, j)),
                  pl.BlockSpec((nh, MLA_KV_RANK, MLA_V_DIM), lambda b_, i, j: (0, 0, 0))],
        out_specs=pl.BlockSpec((None, tq, nh * MLA_V_DIM), lambda b_, i, j: (b_, i, 0)),
        out_shape=jax.ShapeDtypeStruct((b, sq, nh * MLA_V_DIM), BF16),
        scratch_shapes=[pltpu.VMEM((1, nh * tq), F32), pltpu.VMEM((1, nh * tq), F32),
                        pltpu.VMEM((MLA_KV_RANK, nh * tq), F32)],
        compiler_params=_params("arbitrary", "arbitrary", "arbitrary"),
        name="mla",
    )(q, kcat, vt, wv)


def _gla_kernel(qk_ref, v_ref, lr_ref, wd_ref, bd_ref, s0_ref, o_ref, sfin_ref, la_ref, st_ref):
    d = pl.program_id(1)
    i = pl.program_id(2)
    tc = qk_ref.shape[0]
    n_chunk = tc // GLA_CHUNK
    nk = GLA_HEADS * GLA_DK

    @pl.when(i == 0)
    def _():
        st_ref[...] = s0_ref[...]

    z = _dot(lr_ref[...], wd_ref[...]) + bd_ref[...]
    la_ref[...] = jax.nn.log_sigmoid(z) * (1.0 / GLA_TAU)

    row = lax.broadcasted_iota(I32, (GLA_CHUNK, GLA_CHUNK), 0)
    col = lax.broadcasted_iota(I32, (GLA_CHUNK, GLA_CHUNK), 1)
    tri = (row - col) * (1 - 2 * d) >= 0
    tri_b = jnp.where(tri, 1.0, 0.0).astype(BF16)

    def chunk(c, states):
        new_states = []
        cc = c + d * (n_chunk - 1 - 2 * c)
        off = pl.multiple_of(cc * GLA_CHUNK, GLA_CHUNK)
        la = la_ref[pl.ds(off, GLA_CHUNK), :]
        la_hi, la_lo = _split_bf16(la)
        bc = _dot(tri_b, la_hi) + _dot(tri_b, la_lo)
        bend = jnp.sum(la, axis=0, keepdims=True)
        qk = qk_ref[pl.ds(off, GLA_CHUNK), :].astype(F32)
        q = qk[:, 0:nk]
        k = qk[:, nk:2 * nk]
        v = v_ref[pl.ds(off, GLA_CHUNK), :]
        qd = (q * (GLA_DK ** -0.5) * jnp.exp(bc)).astype(BF16)
        ki = (k * jnp.exp(-bc)).astype(BF16)
        kd = (k * jnp.exp(bend - bc)).astype(BF16)
        dec = jnp.exp(bend)
        for h in range(GLA_HEADS):
            ks = slice(h * GLA_DK, (h + 1) * GLA_DK)
            vs = slice(h * GLA_DV, (h + 1) * GLA_DV)
            vh = v[:, vs]
            sc = jnp.where(tri, _dot_nt(qd[:, ks], ki[:, ks]), 0.0)
            st = states[h]
            o = _dot(sc.astype(BF16), vh) + _dot_nt(qd[:, ks], st.astype(BF16))
            o_ref[pl.ds(off, GLA_CHUNK), vs] = o
            new_states.append(st * dec[:, ks] + _dot_tn(vh, kd[:, ks]))
        return tuple(new_states)

    states = lax.fori_loop(0, n_chunk, chunk, tuple(st_ref[h] for h in range(GLA_HEADS)),
                           unroll=GLA_CHUNK_UNROLL)
    for h in range(GLA_HEADS):
        st_ref[h] = states[h]

    @pl.when(i == pl.num_programs(2) - 1)
    def _():
        sfin_ref[...] = st_ref[...]


def _gla(p, wd_pad, bd, s0):
    b, s, _ = p.shape
    tc = min(512, s)
    n_step = s // tc
    rows = lambda b_, d, i: i + d * (n_step - 1 - 2 * i)
    return pl.pallas_call(
        _gla_kernel,
        grid=(b, 2, n_step),
        in_specs=[pl.BlockSpec((None, tc, P_BLK), lambda b_, d, i: (b_, rows(b_, d, i), 4)),
                  pl.BlockSpec((None, tc, P_BLK), lambda b_, d, i: (b_, rows(b_, d, i), 5)),
                  pl.BlockSpec((None, tc, 128), lambda b_, d, i: (b_, rows(b_, d, i), 9 * P_BLK // 128)),
                  pl.BlockSpec((None, 128, 256), lambda b_, d, i: (d, 0, 0)),
                  pl.BlockSpec((None, 1, 256), lambda b_, d, i: (d, 0, 0)),
                  pl.BlockSpec((None, None, GLA_HEADS, GLA_DV, GLA_DK), lambda b_, d, i: (b_, d, 0, 0, 0))],
        out_specs=[pl.BlockSpec((None, None, tc, 512), lambda b_, d, i: (d, b_, rows(b_, d, i), 0)),
                   pl.BlockSpec((None, None, GLA_HEADS, GLA_DV, GLA_DK), lambda b_, d, i: (b_, d, 0, 0, 0))],
        out_shape=[jax.ShapeDtypeStruct((2, b, s, 512), F32),
                   jax.ShapeDtypeStruct((b, 2, GLA_HEADS, GLA_DV, GLA_DK), F32)],
        scratch_shapes=[pltpu.VMEM((tc, 256), F32), pltpu.VMEM((GLA_HEADS, GLA_DV, GLA_DK), F32)],
        compiler_params=_params("arbitrary", "arbitrary", "arbitrary"),
        name="gla",
    )(p, p, p, wd_pad, bd, s0)


def _smlp_kernel(u_ref, v_ref, g_ref, ws_ref, bs_ref, o_ref):
    n_chunk = u_ref.shape[0] // SMLP_CHUNK
    g = g_ref[...]
    for c in range(n_chunk):
        rs = slice(c * SMLP_CHUNK, (c + 1) * SMLP_CHUNK)
        vb = _rms(jax.nn.gelu(v_ref[rs, :].astype(F32)), g).astype(BF16)
        u = jax.nn.gelu(u_ref[rs, :].astype(F32))
        for k in range(SMLP_GROUPS):
            cs = slice(k * SMLP_GROUP_DIM, (k + 1) * SMLP_GROUP_DIM)
            mixed = _dot(ws_ref[k], vb[:, cs]) + bs_ref[:, cs]
            o_ref[rs, cs] = (u[:, cs] * mixed).astype(BF16)


def _smlp(p, v_norm, ws, bs_full):
    b, s, _ = p.shape
    tc = min(512, s)
    return pl.pallas_call(
        _smlp_kernel,
        grid=(b, s // tc),
        in_specs=[pl.BlockSpec((None, tc, P_BLK), lambda b_, i: (b_, i, 7)),
                  pl.BlockSpec((None, tc, P_BLK), lambda b_, i: (b_, i, 8)),
                  pl.BlockSpec((1, 512), lambda b_, i: (0, 0)),
                  pl.BlockSpec((SMLP_GROUPS, SMLP_CHUNK, SMLP_CHUNK), lambda b_, i: (0, 0, 0)),
                  pl.BlockSpec((SMLP_CHUNK, 512), lambda b_, i: (0, 0))],
        out_specs=pl.BlockSpec((None, tc, 512), lambda b_, i: (b_, i, 0)),
        out_shape=jax.ShapeDtypeStruct((b, s, 512), BF16),
        compiler_params=_params("arbitrary", "arbitrary"),
        name="smlp",
    )(p, p, v_norm, ws, bs_full)


def _merge_kernel(h_ref, oa_ref, om_ref, gf_ref, gb_ref, gr_ref, os_ref, gn_ref,
                  wg_ref, bg_ref, wb_ref, y_ref, og_ref):
    @pl.when(pl.program_id(2) == 0)
    def _():
        o = gf_ref[...] + gb_ref[...]
        r = gr_ref[...].astype(F32)
        gate = r * jax.nn.sigmoid(r)
        for hh in range(GLA_HEADS):
            vs = slice(hh * GLA_DV, (hh + 1) * GLA_DV)
            og_ref[:, vs] = (_rms(o[:, vs], gn_ref[...]) * gate[:, vs]).astype(BF16)

    h = h_ref[...]
    branches = (oa_ref[...], om_ref[...], og_ref[...], os_ref[...])
    y = None
    for n, o in enumerate(branches):
        gate = jax.nn.sigmoid(_dot(h, wg_ref[n]) + bg_ref[n])
        t = gate * _dot(o, wb_ref[n])
        y = t if y is None else y + t
    y_ref[...] = y.astype(BF16)


def _merge(h, o_a, o_m, o_g2, p, o_s, gla_norm, wg, bg, wb):
    b, s, d = h.shape
    tm = min(512, s)
    tn = 512
    row = lambda b_, i, j: (b_, i, 0)
    return pl.pallas_call(
        _merge_kernel,
        grid=(b, s // tm, d // tn),
        in_specs=[pl.BlockSpec((None, tm, d), row),
                  pl.BlockSpec((None, tm, 512), row),
                  pl.BlockSpec((None, tm, 512), row),
                  pl.BlockSpec((None, None, tm, 512), lambda b_, i, j: (0, b_, i, 0)),
                  pl.BlockSpec((None, None, tm, 512), lambda b_, i, j: (1, b_, i, 0)),
                  pl.BlockSpec((None, tm, P_BLK), lambda b_, i, j: (b_, i, 6)),
                  pl.BlockSpec((None, tm, 512), row),
                  pl.BlockSpec((1, GLA_DV), lambda b_, i, j: (0, 0)),
                  pl.BlockSpec((4, d, tn), lambda b_, i, j: (0, 0, j)),
                  pl.BlockSpec((4, 1, tn), lambda b_, i, j: (0, 0, j)),
                  pl.BlockSpec((4, 512, tn), lambda b_, i, j: (0, 0, j))],
        out_specs=pl.BlockSpec((None, tm, tn), lambda b_, i, j: (b_, i, j)),
        out_shape=jax.ShapeDtypeStruct((b, s, d), BF16),
        scratch_shapes=[pltpu.VMEM((tm, 512), BF16)],
        compiler_params=_params("arbitrary", "arbitrary", "arbitrary"),
        name="merge",
    )(h, o_a, o_m, o_g2, o_g2, p, o_s, gla_norm, wg, bg, wb)


def _outproj_kernel(y_ref, x_ref, mod_ref, g_ref, wo_ref, wr_ref, br_ref, xo_ref, h2_ref, lg_ref):
    w_hi, w_lo = _split_bf16(wr_ref[...])
    tm = y_ref.shape[0]
    n_part = 2 if tm % 16 == 0 else 1
    rows = [slice(n * tm // n_part, (n + 1) * tm // n_part) for n in range(n_part)]
    z_next = _dot(y_ref[rows[0], :], wo_ref[...])
    for n, rs in enumerate(rows):
        z = z_next
        if n + 1 < n_part:
            z_next = _dot(y_ref[rows[n + 1], :], wo_ref[...])
        xn = x_ref[rs, :] + mod_ref[2:3, :] * _rms(z, g_ref[1:2, :])
        xo_ref[rs, :] = xn
        h2 = _rms(xn, g_ref[2:3, :]) * (1.0 + mod_ref[4:5, :]) + mod_ref[3:4, :]
        h2_ref[rs, :] = h2
        h_hi, h_lo = _split_bf16(h2)
        lg_ref[rs, :] = _dot(h_hi, w_hi) + _dot(h_lo, w_hi) + _dot(h_hi, w_lo) + br_ref[...]


def _outproj(y, x, mod, norm_g, w_out, wr_pad, br_pad):
    b, s, d = x.shape
    tm = min(256, s)
    row = lambda b_, i: (b_, i, 0)
    return pl.pallas_call(
        _outproj_kernel,
        grid=(b, s // tm),
        in_specs=[pl.BlockSpec((None, tm, d), row),
                  pl.BlockSpec((None, tm, d), row),
                  pl.BlockSpec((None, 8, d), lambda b_, i: (b_, 0, 0)),
                  pl.BlockSpec((4, d), lambda b_, i: (0, 0)),
                  pl.BlockSpec((d, d), lambda b_, i: (0, 0)),
                  pl.BlockSpec((d, LANE), lambda b_, i: (0, 0)),
                  pl.BlockSpec((1, LANE), lambda b_, i: (0, 0))],
        out_specs=[pl.BlockSpec((None, tm, d), row),
                   pl.BlockSpec((None, tm, d), row),
                   pl.BlockSpec((None, tm, LANE), row)],
        out_shape=[jax.ShapeDtypeStruct((b, s, d), F32),
                   jax.ShapeDtypeStruct((b, s, d), F32),
                   jax.ShapeDtypeStruct((b, s, LANE), F32)],
        compiler_params=_params("arbitrary", "arbitrary"),
        name="outproj",
    )(y, x, mod, norm_g, w_out, wr_pad, br_pad)


def _route_kernel(lg_ref, idx_ref, w_ref, rank_ref, cnt_ref, carry_ref):
    i = pl.program_id(0)

    @pl.when(i == 0)
    def _():
        carry_ref[...] = jnp.zeros(carry_ref.shape, F32)

    lg = lg_ref[...]
    tm = lg.shape[0]
    lane = lax.broadcasted_iota(I32, lg.shape, 1).astype(F32)
    vals, idxs = [], []
    onehot = jnp.zeros(lg.shape, F32)
    for _ in range(TOP_K):
        m = jnp.max(lg, axis=-1, keepdims=True)
        sel = jnp.min(jnp.where(lg == m, lane, float(LANE)), axis=-1, keepdims=True)
        hit = lane == sel
        vals.append(m)
        idxs.append(sel)
        onehot = jnp.where(hit, 1.0, onehot)
        lg = jnp.where(hit, NEG_BIG, lg)
    es = [jnp.exp(v - vals[0]) for v in vals]
    den = es[0] + es[1] + es[2] + es[3]
    r = lax.broadcasted_iota(I32, (tm, tm), 0)
    c = lax.broadcasted_iota(I32, (tm, tm), 1)
    lower = (c < r).astype(F32).astype(BF16)
    before = _dot(lower, onehot.astype(BF16)) + carry_ref[...]
    for k in range(TOP_K):
        idx_ref[:, k:k + 1] = idxs[k].astype(I32)
        w_ref[:, k:k + 1] = es[k] / den
        rank_ref[:, k:k + 1] = jnp.sum(jnp.where(lane == idxs[k], before, 0.0), axis=-1,
                                       keepdims=True).astype(I32)
    carry_ref[...] = carry_ref[...] + jnp.sum(onehot, axis=0, keepdims=True)
    cnt_ref[...] = carry_ref[...]


def _route(logits):
    t = logits.shape[0]
    tm = 256
    return pl.pallas_call(
        _route_kernel,
        grid=(t // tm,),
        in_specs=[pl.BlockSpec((tm, LANE), lambda i: (i, 0))],
        out_specs=[pl.BlockSpec((tm, TOP_K), lambda i: (i, 0)),
                   pl.BlockSpec((tm, TOP_K), lambda i: (i, 0)),
                   pl.BlockSpec((tm, TOP_K), lambda i: (i, 0)),
                   pl.BlockSpec((1, LANE), lambda i: (0, 0))],
        out_shape=[jax.ShapeDtypeStruct((t, TOP_K), I32),
                   jax.ShapeDtypeStruct((t, TOP_K), F32),
                   jax.ShapeDtypeStruct((t, TOP_K), I32),
                   jax.ShapeDtypeStruct((1, LANE), F32)],
        scratch_shapes=[pltpu.VMEM((1, LANE), F32)],
        compiler_params=_params("arbitrary"),
        name="route",
    )(logits)


def _dispatch_kernel(pad_ref, dest_hbm, src_ref, *rest, zero_pads):
    xs_out, dest_smem, zero_ref, idx_sem, sem = rest[-5:]
    i = pl.program_id(0)
    tm = src_ref.shape[0]
    cp = pltpu.make_async_copy(dest_hbm.at[i], dest_smem, idx_sem)
    cp.start()

    if zero_pads:
        @pl.when(i == 0)
        def _():
            zero_ref[...] = jnp.zeros(zero_ref.shape, zero_ref.dtype)

            def pad_copy(e, r):
                return pltpu.make_async_copy(zero_ref.at[pl.ds(0, 1)],
                                             xs_out.at[pl.ds(pad_ref[e] + r, 1)], sem)

            def start_pads(e, carry):
                def body(r, c):
                    pad_copy(e, r).start()
                    return c
                return lax.fori_loop(0, pad_ref[N_EXPERTS + e], body, carry)

            def wait_pads(e, carry):
                def body(r, c):
                    pad_copy(e, r).wait()
                    return c
                return lax.fori_loop(0, pad_ref[N_EXPERTS + e], body, carry)

            lax.fori_loop(0, N_EXPERTS, start_pads, 0)
            lax.fori_loop(0, N_EXPERTS, wait_pads, 0)

    cp.wait()

    def row_copy(t, k):
        return pltpu.make_async_copy(src_ref.at[pl.ds(t, 1)],
                                     xs_out.at[pl.ds(dest_smem[t * TOP_K + k], 1)], sem)

    def issue(t, carry):
        for k in range(TOP_K):
            row_copy(t, k).start()
        return carry

    def drain(t, carry):
        for k in range(TOP_K):
            row_copy(t, k).wait()
        return carry

    lax.fori_loop(0, tm, issue, 0, unroll=DMA_ISSUE_UNROLL)
    lax.fori_loop(0, tm, drain, 0, unroll=DMA_ISSUE_UNROLL)


def _dispatch(pads, dest, src, xs, n_rows):
    t, d = src.shape
    tm = min(512, t)
    dest2 = dest.reshape(t // tm, tm * TOP_K)
    first = xs is None
    args = (pads, dest2, src) + (() if first else (xs,))
    return pl.pallas_call(
        functools.partial(_dispatch_kernel, zero_pads=first),
        grid=(t // tm,),
        in_specs=[pl.BlockSpec(memory_space=pltpu.SMEM),
                  pl.BlockSpec(memory_space=pl.ANY),
                  pl.BlockSpec((tm, d), lambda i: (i, 0))] + ([] if first else [pl.BlockSpec(memory_space=pl.ANY)]),
        out_specs=pl.BlockSpec(memory_space=pl.ANY),
        out_shape=jax.ShapeDtypeStruct((n_rows, d), src.dtype),
        scratch_shapes=[pltpu.SMEM((tm * TOP_K,), I32), pltpu.VMEM((8, d), src.dtype),
                        pltpu.SemaphoreType.DMA, pltpu.SemaphoreType.DMA],
        input_output_aliases={} if first else {3: 0},
        compiler_params=_params("arbitrary"),
        name="dispatch",
    )(*args)


def _expert_kernel(te_ref, nu_ref, x_ref, w1_ref, b1_ref, w2_ref, b2_ref, y_ref):
    @pl.when(pl.program_id(0) < nu_ref[0])
    def _():
        z = _dot(x_ref[...].astype(BF16), w1_ref[...]) + b1_ref[...]
        glu = jnp.minimum(z[:, :D_EXPERT], SWIGLU_LIMIT)
        lin = jnp.clip(z[:, D_EXPERT:], -SWIGLU_LIMIT, SWIGLU_LIMIT)
        act = glu * jax.nn.sigmoid(SWIGLU_ALPHA * glu) * (lin + 1.0)
        y_ref[...] = _dot(act.astype(BF16), w2_ref[...]) + b2_ref[...]


def _experts(layer, tile_e, n_used, xs, w1, b1, w2, b2):
    n_rows, d = xs.shape
    tr = EXPERT_TILE
    n_tiles = n_rows // tr
    row = lambda i, te, nu: (jnp.minimum(i, nu[0] - 1), 0)
    wsel = lambda i, te, nu: (layer, te[i], 0, 0)
    return pl.pallas_call(
        _expert_kernel,
        grid_spec=pltpu.PrefetchScalarGridSpec(
            num_scalar_prefetch=2,
            grid=(n_tiles,),
            in_specs=[pl.BlockSpec((tr, d), row),
                      pl.BlockSpec((None, None, d, 2 * D_EXPERT), wsel),
                      pl.BlockSpec((None, None, 1, 2 * D_EXPERT), wsel),
                      pl.BlockSpec((None, None, D_EXPERT, d), wsel),
                      pl.BlockSpec((None, None, 1, d), wsel)],
            out_specs=pl.BlockSpec((tr, d), row)),
        out_shape=jax.ShapeDtypeStruct((n_rows, d), F32),
        compiler_params=_params("arbitrary"),
        name="experts",
    )(tile_e, n_used, xs, w1, b1, w2, b2)


def _combine_kernel(dest_hbm, w_ref, x_ref, mod_ref, g_ref, ys_hbm, o_ref, dest_smem, buf, idx_sem, sem):
    tm = x_ref.shape[0]
    n_step = pl.num_programs(0) * pl.num_programs(1)
    i = pl.program_id(0) * pl.num_programs(1) + pl.program_id(1)
    slot = lax.rem(i, 2)

    def row_copy(sl, t, k):
        return pltpu.make_async_copy(ys_hbm.at[pl.ds(dest_smem[sl, t * TOP_K + k], 1)],
                                     buf.at[sl, k, pl.ds(t, 1)], sem.at[sl])

    def start_gather(step, sl):
        cp = pltpu.make_async_copy(dest_hbm.at[step], dest_smem.at[sl], idx_sem)
        cp.start()
        cp.wait()

        def issue(t, carry):
            for k in range(TOP_K):
                row_copy(sl, t, k).start()
            return carry

        lax.fori_loop(0, tm, issue, 0, unroll=DMA_ISSUE_UNROLL)

    @pl.when(i == 0)
    def _():
        start_gather(0, 0)

    @pl.when(i + 1 < n_step)
    def _():
        start_gather(i + 1, 1 - slot)

    def drain(t, carry):
        for k in range(TOP_K):
            row_copy(slot, t, k).wait()
        return carry

    lax.fori_loop(0, tm, drain, 0, unroll=DMA_ISSUE_UNROLL)
    w = w_ref[...]
    f = w[:, 0:1] * buf[slot, 0]
    for k in range(1, TOP_K):
        f = f + w[:, k:k + 1] * buf[slot, k]
    o_ref[...] = x_ref[...] + mod_ref[5:6, :] * _rms(f, g_ref[3:4, :])


def _combine(dest, w, x, mod, norm_g, ys):
    b, s, d = x.shape
    tm = min(256, s)
    ns = s // tm
    dest2 = dest.reshape(b * ns, tm * TOP_K)
    return pl.pallas_call(
        _combine_kernel,
        grid=(b, ns),
        in_specs=[pl.BlockSpec(memory_space=pl.ANY),
                  pl.BlockSpec((None, tm, TOP_K), lambda b_, i: (b_, i, 0)),
                  pl.BlockSpec((None, tm, d), lambda b_, i: (b_, i, 0)),
                  pl.BlockSpec((None, 8, d), lambda b_, i: (b_, 0, 0)),
                  pl.BlockSpec((4, d), lambda b_, i: (0, 0)),
                  pl.BlockSpec(memory_space=pl.ANY)],
        out_specs=pl.BlockSpec((None, tm, d), lambda b_, i: (b_, i, 0)),
        out_shape=jax.ShapeDtypeStruct((b, s, d), F32),
        scratch_shapes=[pltpu.SMEM((2, tm * TOP_K), I32), pltpu.VMEM((2, TOP_K, tm, d), F32),
                        pltpu.SemaphoreType.DMA, pltpu.SemaphoreType.DMA((2,))],
        compiler_params=_params("arbitrary", "arbitrary"),
        name="combine",
    )(dest2, w.reshape(b, s, TOP_K), x, mod, norm_g, ys)


def _rot_half_cols(w, head_dim):
    d_in, n = w.shape
    w3 = w.reshape(d_in, n // head_dim, head_dim)
    half = head_dim // 2
    return jnp.concatenate([-w3[..., half:], w3[..., :half]], axis=-1).reshape(d_in, n)


def _extended_in_weights(w_in):
    sizes = (512, 128, 128, MLA_Q_RANK, MLA_KV_RANK, MLA_ROPE_DIM, 256, 256, 512,
             2 * GLA_DECAY_RANK, 512, 512, 512)
    cols, off = [], 0
    for n in sizes:
        cols.append(w_in[:, off:off + n])
        off += n
    (a_q, a_k, a_v, m_cq, m_ckv, m_kr, g_q, g_k, g_v, g_lr, g_r, s_u, s_v) = cols
    pad = jnp.zeros((w_in.shape[0], P_BLK - 2 * GLA_DECAY_RANK), w_in.dtype)
    ext = jnp.concatenate(
        [a_q, _rot_half_cols(a_q, SWA_HEAD_DIM),
         a_k, _rot_half_cols(a_k, SWA_HEAD_DIM), a_v, m_kr, _rot_half_cols(m_kr, MLA_ROPE_DIM),
         m_cq, m_ckv, g_q, g_k, g_v, g_r, s_u, s_v, g_lr, pad], axis=1)
    return ext.astype(BF16)


def _rope_tables(n_tok):
    n_rows = n_tok // GRID_W
    rows = jnp.repeat(jnp.arange(n_rows, dtype=F32), GRID_W)
    cols = jnp.tile(jnp.arange(GRID_W, dtype=F32), n_rows)
    n_freq = SWA_HEAD_DIM // 4
    inv_freq = ROPE_BASE ** (-jnp.arange(n_freq, dtype=F32) / n_freq)
    ang = jnp.concatenate([rows[:, None] * inv_freq, cols[:, None] * inv_freq], axis=-1)
    cos = jnp.concatenate([jnp.cos(ang), jnp.cos(ang)], axis=-1)
    sin = jnp.concatenate([jnp.sin(ang), jnp.sin(ang)], axis=-1)
    return jnp.tile(cos, (1, 8)), jnp.tile(sin, (1, 8))


def _moe(layer, h2_parts, logits_parts, w1, b1, w2, b2):
    d = h2_parts[0].shape[1]
    logits = jnp.concatenate(logits_parts, axis=0)
    t_all = logits.shape[0]
    top_idx, top_w, rank, counts = _route(logits)
    counts = counts[0, :N_EXPERTS].astype(I32)
    tr = EXPERT_TILE
    padded = (counts + tr - 1) // tr * tr
    p_end = jnp.cumsum(padded)
    p_start = p_end - padded
    dest = p_start[top_idx] + rank
    n_tiles = -(-t_all * TOP_K // tr) + N_EXPERTS
    tile_start = jnp.arange(n_tiles, dtype=I32) * tr
    tile_e = jnp.minimum(jnp.sum((p_end[None, :] <= tile_start[:, None]).astype(I32), axis=1),
                         N_EXPERTS - 1)
    n_used = (p_end[-1:] // tr).astype(I32)
    pads = jnp.concatenate([p_start + counts, padded - counts]).astype(I32)
    xs = None
    off = 0
    for part in h2_parts:
        n = part.shape[0]
        xs = _dispatch(pads, dest[off:off + n], part, xs, n_tiles * tr)
        off += n
    ys = _experts(layer, tile_e, n_used, xs, w1, b1, w2, b2)
    return dest, top_w, ys


def kernel(x, c, ctx, c_ctx, ada_w, ada_b, norm_g, w_in, attn_sink, mla_q_norm, mla_w_uq, mla_kv_norm, mla_w_ukv, gla_w_decay, gla_b_decay, gla_out_norm, smlp_v_norm, smlp_w_spatial, smlp_b_spatial, w_branch, w_gate, b_gate, w_out, router_w, router_b, expert_w1, expert_b1, expert_w2, expert_b2):
    bsz, seq, d = x.shape
    n_ctx = ctx.shape[1]
    depth = ada_w.shape[0]

    cs = jnp.zeros((8, d), F32).at[:bsz].set(c).at[bsz].set(c_ctx)
    mod_all = _ada(cs, ada_w, ada_b).reshape(depth, 8, N_MOD, d)
    mod_all = jnp.pad(mod_all, ((0, 0), (0, 0), (0, 8 - N_MOD), (0, 0)))

    cos_t, sin_t = _rope_tables(seq)
    cos_c = jnp.ones((bsz * n_ctx, 512), F32)
    sin_c = jnp.zeros((bsz * n_ctx, 512), F32)
    ctx = ctx.reshape(1, bsz * n_ctx, d)

    w1 = expert_w1.astype(BF16)
    b1 = expert_b1.reshape(depth, N_EXPERTS, 1, 2 * D_EXPERT)
    w2 = expert_w2.astype(BF16)
    b2 = expert_b2.reshape(depth, N_EXPERTS, 1, d)

    for l in range(depth):
        last = l == depth - 1
        mod = mod_all[l, :bsz]
        mod_c = mod_all[l, bsz:bsz + 1]
        g_l = norm_g[l]
        w_ext = _extended_in_weights(w_in[l])
        uq = mla_w_uq[l].reshape(MLA_Q_RANK, MLA_HEADS, MLA_NOPE_DIM + MLA_ROPE_DIM)
        uq_rope = uq[:, :, MLA_NOPE_DIM:].reshape(MLA_Q_RANK, MLA_HEADS * MLA_ROPE_DIM)
        wuq_ext = jnp.concatenate(
            [uq[:, :, :MLA_NOPE_DIM].reshape(MLA_Q_RANK, MLA_HEADS * MLA_NOPE_DIM), uq_rope,
             _rot_half_cols(uq_rope, MLA_ROPE_DIM)], axis=1).astype(BF16)
        ukv = mla_w_ukv[l].reshape(MLA_KV_RANK, MLA_HEADS, MLA_NOPE_DIM + MLA_V_DIM)
        wkn = jnp.transpose(ukv[:, :, :MLA_NOPE_DIM], (1, 0, 2)).astype(BF16)
        wv = jnp.transpose(ukv[:, :, MLA_NOPE_DIM:], (1, 0, 2)).astype(BF16)
        qn_g = mla_q_norm[l].reshape(1, MLA_Q_RANK)
        kv_g = mla_kv_norm[l].reshape(1, MLA_KV_RANK)
        wd_pad = jnp.zeros((2, 128, 256), F32)
        for dd in range(2):
            wd_pad = wd_pad.at[dd, dd * GLA_DECAY_RANK:(dd + 1) * GLA_DECAY_RANK].set(gla_w_decay[l, dd])
        wd_pad = wd_pad.astype(BF16)
        bd = gla_b_decay[l].reshape(2, 1, 256)
        gla_norm = gla_out_norm[l].reshape(1, GLA_DV)
        v_norm = smlp_v_norm[l].reshape(1, 512)
        ws = smlp_w_spatial[l].astype(BF16)
        bs_full = jnp.repeat(smlp_b_spatial[l].T, SMLP_GROUP_DIM, axis=1)
        wg = w_gate[l].astype(BF16)
        bg = b_gate[l].reshape(4, 1, d)
        wb = w_branch[l].astype(BF16)
        wo = w_out[l].astype(BF16)
        wr_pad = jnp.pad(router_w[l], ((0, 0), (0, LANE - N_EXPERTS)))
        br_pad = jnp.pad(router_b[l].reshape(1, N_EXPERTS), ((0, 0), (0, LANE - N_EXPERTS)),
                         constant_values=NEG_BIG)
        sink = attn_sink[l]

        h, p = _inproj(x, mod, g_l, w_ext)
        hc, pc1 = _inproj(ctx, mod_c, g_l, w_ext)
        pc = pc1.reshape(bsz, n_ctx, P_WIDTH)
        ka, kcat, q_m = _prep(p, cos_t, sin_t, qn_g, kv_g, wuq_ext, wkn)
        _, kcat_c, q_mc = _prep(pc1, cos_c, sin_c, qn_g, kv_g, wuq_ext, wkn)
        kcat_c = kcat_c.reshape(bsz, n_ctx, MLA_QK_PAD)

        o_a = _swa(sink, p, ka, cos_t, sin_t, pc)
        o_m = _mla(q_m, jnp.concatenate([kcat, kcat_c], axis=1), wv)
        s0 = jnp.zeros((bsz, 2, GLA_HEADS, GLA_DV, GLA_DK), F32)
        og_c, s_ctx = _gla(pc, wd_pad, bd, s0)
        og, _ = _gla(p, wd_pad, bd, s_ctx)
        o_s = _smlp(p, v_norm, ws, bs_full)
        y = _merge(h, o_a, o_m, og, p, o_s, gla_norm, wg, bg, wb)
        x, h2, lg = _outproj(y, x, mod, g_l, wo, wr_pad, br_pad)
        h2_parts = [h2.reshape(bsz * seq, d)]
        lg_parts = [lg.reshape(bsz * seq, LANE)]

        if not last:
            oc_a = _swa_ctx(sink, pc)
            q_mc = q_mc.reshape(MLA_HEADS, bsz, n_ctx, MLA_QK_PAD).transpose(1, 0, 2, 3)
            oc_m = _mla(q_mc, kcat_c, wv)
            oc_s = _smlp(pc, v_norm, ws, bs_full)
            flat = lambda a: a.reshape(1, bsz * n_ctx, a.shape[-1])
            yc = _merge(hc, flat(oc_a), flat(oc_m), og_c.reshape(2, 1, bsz * n_ctx, 512), pc1,
                        flat(oc_s), gla_norm, wg, bg, wb)
            ctx, h2c, lgc = _outproj(yc, ctx, mod_c, g_l, wo, wr_pad, br_pad)
            h2_parts.append(h2c.reshape(bsz * n_ctx, d))
            lg_parts.append(lgc.reshape(bsz * n_ctx, LANE))

        dest, top_w, ys = _moe(l, h2_parts, lg_parts, w1, b1, w2, b2)
        n_lat = bsz * seq
        x = _combine(dest[:n_lat], top_w[:n_lat], x, mod, g_l, ys)
        if not last:
            ctx = _combine(dest[n_lat:], top_w[n_lat:], ctx, mod_c, g_l, ys)
    return x
```

```python
import functools

import jax
import jax.numpy as jnp
from jax import lax
from jax.experimental import pallas as pl
from jax.experimental.pallas import tpu as pltpu

F32 = jnp.float32
BF16 = jnp.bfloat16
I32 = jnp.int32

EPS = 1e-6
ROPE_BASE = 10000.0
GRID_W = 64
N_MOD = 6

SWA_HEADS = 8
SWA_KV_HEADS = 2
SWA_GROUP = SWA_HEADS // SWA_KV_HEADS
SWA_HEAD_DIM = 64
SWA_BLOCK = 128

MLA_HEADS = 4
MLA_Q_RANK = 384
MLA_KV_RANK = 128
MLA_NOPE_DIM = 128
MLA_ROPE_DIM = 64
MLA_V_DIM = 128
MLA_QK_PAD = 256
MLA_QUERY_CHUNK = 512
MLA_QUERY_TILE = 2048
LOG2_E = 1.4426950408889634

GLA_HEADS = 4
GLA_DK = 64
GLA_DV = 128
GLA_DECAY_RANK = 16
GLA_TAU = 16.0
GLA_CHUNK = 64

SMLP_GROUPS = 4
SMLP_CHUNK = 128
SMLP_GROUP_DIM = 128

N_EXPERTS = 32
TOP_K = 4
D_EXPERT = 1024
SWIGLU_LIMIT = 7.0
SWIGLU_ALPHA = 1.702

LANE = 128
NEG_BIG = -1e30
VMEM_LIMIT = 56 * 1024 * 1024

P_BLK = 512
P_NBLK = 10
P_WIDTH = P_BLK * P_NBLK
EXPERT_TILE = 256
ROW_GROUP = 8


def _dot(a, b):
    return jnp.dot(a, b, preferred_element_type=F32)


def _dot_nt(a, b):
    return lax.dot_general(a, b, (((1,), (1,)), ((), ())), preferred_element_type=F32)


def _dot_tn(a, b):
    return lax.dot_general(a, b, (((0,), (0,)), ((), ())), preferred_element_type=F32)


def _split_bf16(a):
    hi = a.astype(BF16)
    lo = (a - hi.astype(F32)).astype(BF16)
    return hi, lo


def _rms(x, g):
    return x * lax.rsqrt(jnp.mean(x * x, axis=-1, keepdims=True) + EPS) * g


def _for_each_row(n_rows, fn):
    def group(u, carry):
        base = pl.multiple_of(u * ROW_GROUP, ROW_GROUP)
        for j in range(ROW_GROUP):
            fn(base + j)
        return carry

    lax.fori_loop(0, n_rows // ROW_GROUP, group, 0)


def _params(*sem):
    return pltpu.CompilerParams(dimension_semantics=sem, vmem_limit_bytes=VMEM_LIMIT)


def _ada_kernel(c_ref, w_ref, b_ref, o_ref):
    c = c_ref[...]
    a_hi, a_lo = _split_bf16(c * jax.nn.sigmoid(c))
    w_hi, w_lo = _split_bf16(w_ref[...])
    o_ref[...] = _dot(a_hi, w_hi) + _dot(a_lo, w_hi) + _dot(a_hi, w_lo) + b_ref[...]


def _ada(cs, ada_w, ada_b):
    n_layer, d, n = ada_w.shape
    tn = 1024
    return pl.pallas_call(
        _ada_kernel,
        grid=(n_layer, n // tn),
        in_specs=[pl.BlockSpec((8, d), lambda l, j: (0, 0)),
                  pl.BlockSpec((None, d, tn), lambda l, j: (l, 0, j)),
                  pl.BlockSpec((None, 1, tn), lambda l, j: (l, 0, j))],
        out_specs=pl.BlockSpec((None, 8, tn), lambda l, j: (l, 0, j)),
        out_shape=jax.ShapeDtypeStruct((n_layer, 8, n), F32),
        compiler_params=_params("arbitrary", "arbitrary"),
        name="ada",
    )(cs, ada_w, ada_b.reshape(n_layer, 1, n))


def _inproj_kernel(x_ref, mod_ref, g_ref, w_ref, h_ref, p_ref):
    y = _rms(x_ref[...], g_ref[0:1, :])
    hb = (y * (1.0 + mod_ref[1:2, :]) + mod_ref[0:1, :]).astype(BF16)
    h_ref[...] = hb
    p_ref[...] = _dot(hb, w_ref[...]).astype(BF16)


def _inproj(x, mod, norm_g, w_ext):
    b, s, d = x.shape
    tm = min(256, s)
    return pl.pallas_call(
        _inproj_kernel,
        grid=(b, s // tm),
        in_specs=[pl.BlockSpec((None, tm, d), lambda b_, i: (b_, i, 0)),
                  pl.BlockSpec((None, 8, d), lambda b_, i: (b_, 0, 0)),
                  pl.BlockSpec((4, d), lambda b_, i: (0, 0)),
                  pl.BlockSpec((d, P_WIDTH), lambda b_, i: (0, 0), pipeline_mode=pl.Buffered(1))],
        out_specs=[pl.BlockSpec((None, tm, d), lambda b_, i: (b_, i, 0)),
                   pl.BlockSpec((None, tm, P_WIDTH), lambda b_, i: (b_, i, 0))],
        out_shape=[jax.ShapeDtypeStruct((b, s, d), BF16),
                   jax.ShapeDtypeStruct((b, s, P_WIDTH), BF16)],
        compiler_params=_params("arbitrary", "arbitrary"),
        name="inproj",
    )(x, mod, norm_g, w_ext)


def _prep_kernel(kblk_ref, cblk_ref, cos_ref, sin_ref, qn_g_ref, kv_g_ref, wuq_ref, wkn_ref,
                 ka_ref, kcat_ref, q_ref):
    cos = cos_ref[...]
    sin = sin_ref[...]
    kb = kblk_ref[...].astype(F32)
    ka_ref[...] = (kb[:, 0:128] * cos[:, 0:128] + kb[:, 128:256] * sin[:, 0:128]).astype(BF16)
    kr = kb[:, 384:448] * cos[:, 0:64] + kb[:, 448:512] * sin[:, 0:64]
    cb = cblk_ref[...].astype(F32)
    ckvn = _rms(cb[:, MLA_Q_RANK:], kv_g_ref[...])
    tm = kb.shape[0]
    kcat_ref[:, 0:128] = ckvn.astype(BF16)
    kcat_ref[:, 128:192] = kr.astype(BF16)
    kcat_ref[:, 192:256] = jnp.zeros((tm, 64), BF16)
    cqn = _rms(cb[:, :MLA_Q_RANK], qn_g_ref[...]).astype(BF16)
    qa = _dot(cqn, wuq_ref[...])
    qr = qa[:, 512:768] * cos[:, 0:256] + qa[:, 768:1024] * sin[:, 0:256]
    scale = (MLA_NOPE_DIM + MLA_ROPE_DIM) ** -0.5 * LOG2_E
    for h in range(MLA_HEADS):
        qn = qa[:, h * 128:(h + 1) * 128].astype(BF16)
        q_ref[h, :, 0:128] = (_dot_nt(qn, wkn_ref[h]) * scale).astype(BF16)
        q_ref[h, :, 128:192] = (qr[:, h * 64:(h + 1) * 64] * scale).astype(BF16)
        q_ref[h, :, 192:256] = jnp.zeros((tm, 64), BF16)


def _prep(p, cos_t, sin_t, qn_g, kv_g, wuq_ext, wkn):
    b, s, _ = p.shape
    tm = min(512, s)
    return pl.pallas_call(
        _prep_kernel,
        grid=(b, s // tm),
        in_specs=[pl.BlockSpec((None, tm, P_BLK), lambda b_, i: (b_, i, 2)),
                  pl.BlockSpec((None, tm, P_BLK), lambda b_, i: (b_, i, 3)),
                  pl.BlockSpec((tm, 512), lambda b_, i: (i, 0)),
                  pl.BlockSpec((tm, 512), lambda b_, i: (i, 0)),
                  pl.BlockSpec((1, MLA_Q_RANK), lambda b_, i: (0, 0)),
                  pl.BlockSpec((1, MLA_KV_RANK), lambda b_, i: (0, 0)),
                  pl.BlockSpec((MLA_Q_RANK, 1024), lambda b_, i: (0, 0)),
                  pl.BlockSpec((MLA_HEADS, MLA_KV_RANK, MLA_NOPE_DIM), lambda b_, i: (0, 0, 0))],
        out_specs=[pl.BlockSpec((None, tm, 128), lambda b_, i: (b_, i, 0)),
                   pl.BlockSpec((None, tm, MLA_QK_PAD), lambda b_, i: (b_, i, 0)),
                   pl.BlockSpec((None, MLA_HEADS, tm, MLA_QK_PAD), lambda b_, i: (b_, 0, i, 0))],
        out_shape=[jax.ShapeDtypeStruct((b, s, 128), BF16),
                   jax.ShapeDtypeStruct((b, s, MLA_QK_PAD), BF16),
                   jax.ShapeDtypeStruct((b, MLA_HEADS, s, MLA_QK_PAD), BF16)],
        compiler_params=_params("arbitrary", "arbitrary"),
        name="prep",
    )(p, p, cos_t, sin_t, qn_g, kv_g, wuq_ext, wkn)


def _swa_group(q, sink_ref, g, k_all, v_all, mask, o_ref):
    heads = range(g * SWA_GROUP, (g + 1) * SWA_GROUP)
    n_q = q.shape[0]
    qs = jnp.concatenate([q[:, h * SWA_HEAD_DIM:(h + 1) * SWA_HEAD_DIM] for h in heads], axis=0)
    sink = jnp.concatenate([jnp.full((n_q, 1), sink_ref[h], F32) for h in heads], axis=0)
    s = _dot_nt(qs, k_all)
    if mask is not None:
        s = s + jnp.concatenate([mask] * SWA_GROUP, axis=0)
    m = jnp.maximum(jnp.max(s, axis=-1, keepdims=True), sink)
    e = jnp.exp(s - m)
    den = jnp.sum(e, axis=-1, keepdims=True) + jnp.exp(sink - m)
    o = _dot(e.astype(BF16), v_all) / den
    for n, h in enumerate(heads):
        o_ref[:, h * SWA_HEAD_DIM:(h + 1) * SWA_HEAD_DIM] = o[n * n_q:(n + 1) * n_q, :].astype(BF16)


def _swa_kernel(sink_ref, q_ref, qr_ref, cos_ref, sin_ref, kp_ref, kc_ref, kn_ref,
                vp_ref, vc_ref, vn_ref, kx_ref, vx_ref, o_ref):
    i = pl.program_id(1)
    nb = pl.num_programs(1)
    scale = SWA_HEAD_DIM ** -0.5
    q = ((q_ref[...].astype(F32) * cos_ref[...] + qr_ref[...].astype(F32) * sin_ref[...])
         * scale).astype(BF16)
    n_ctx = kx_ref.shape[0]
    n_keys = 3 * SWA_BLOCK + n_ctx
    row = lax.broadcasted_iota(I32, (SWA_BLOCK, n_keys), 0)
    col = lax.broadcasted_iota(I32, (SWA_BLOCK, n_keys), 1)
    dist = col - row
    lo = jnp.where(i > 0, 0, n_keys)
    hi = jnp.where(i < nb - 1, 2 * SWA_BLOCK, -n_keys)
    bias_prev = jnp.where(dist >= lo, 0.0, NEG_BIG)
    bias_next = jnp.where(dist <= hi, 0.0, NEG_BIG)
    in_next = (col >= 2 * SWA_BLOCK) & (col < 3 * SWA_BLOCK)
    mask = jnp.where(col < SWA_BLOCK, bias_prev, jnp.where(in_next, bias_next, 0.0))
    for g in range(SWA_KV_HEADS):
        gs = slice(g * SWA_HEAD_DIM, (g + 1) * SWA_HEAD_DIM)
        k_all = jnp.concatenate([kp_ref[:, gs], kc_ref[:, gs], kn_ref[:, gs], kx_ref[:, gs]], axis=0)
        v_all = jnp.concatenate([vp_ref[:, gs], vc_ref[:, gs], vn_ref[:, gs], vx_ref[:, gs]], axis=0)
        _swa_group(q, sink_ref, g, k_all, v_all, mask, o_ref)


def _swa_ctx_kernel(sink_ref, q_ref, kx_ref, vx_ref, o_ref):
    scale = SWA_HEAD_DIM ** -0.5
    q = (q_ref[...].astype(F32) * scale).astype(BF16)
    for g in range(SWA_KV_HEADS):
        gs = slice(g * SWA_HEAD_DIM, (g + 1) * SWA_HEAD_DIM)
        _swa_group(q, sink_ref, g, kx_ref[:, gs], vx_ref[:, gs], None, o_ref)


def _swa(sink, p, ka, cos_t, sin_t, pc):
    b, s, _ = p.shape
    n_ctx = pc.shape[1]
    nb = s // SWA_BLOCK
    blk = SWA_BLOCK
    prev = lambda b_, i: (b_, jnp.maximum(i - 1, 0), 0)
    cur = lambda b_, i: (b_, i, 0)
    nxt = lambda b_, i: (b_, jnp.minimum(i + 1, nb - 1), 0)
    vcol = (2 * P_BLK + 256) // 128
    vprev = lambda b_, i: (b_, jnp.maximum(i - 1, 0), vcol)
    vcur = lambda b_, i: (b_, i, vcol)
    vnxt = lambda b_, i: (b_, jnp.minimum(i + 1, nb - 1), vcol)
    return pl.pallas_call(
        _swa_kernel,
        grid=(b, nb),
        in_specs=[pl.BlockSpec(memory_space=pltpu.SMEM),
                  pl.BlockSpec((None, blk, P_BLK), lambda b_, i: (b_, i, 0)),
                  pl.BlockSpec((None, blk, P_BLK), lambda b_, i: (b_, i, 1)),
                  pl.BlockSpec((blk, 512), lambda b_, i: (i, 0)),
                  pl.BlockSpec((blk, 512), lambda b_, i: (i, 0)),
                  pl.BlockSpec((None, blk, 128), prev),
                  pl.BlockSpec((None, blk, 128), cur),
                  pl.BlockSpec((None, blk, 128), nxt),
                  pl.BlockSpec((None, blk, 128), vprev),
                  pl.BlockSpec((None, blk, 128), vcur),
                  pl.BlockSpec((None, blk, 128), vnxt),
                  pl.BlockSpec((None, n_ctx, 128), lambda b_, i: (b_, 0, 2 * P_BLK // 128)),
                  pl.BlockSpec((None, n_ctx, 128), lambda b_, i: (b_, 0, vcol))],
        out_specs=pl.BlockSpec((None, blk, 512), lambda b_, i: (b_, i, 0)),
        out_shape=jax.ShapeDtypeStruct((b, s, 512), BF16),
        compiler_params=_params("arbitrary", "arbitrary"),
        name="swa",
    )(sink, p, p, cos_t, sin_t, ka, ka, ka, p, p, p, pc, pc)


def _swa_ctx(sink, pc):
    b, n_ctx, _ = pc.shape
    blk = SWA_BLOCK
    vcol = (2 * P_BLK + 256) // 128
    return pl.pallas_call(
        _swa_ctx_kernel,
        grid=(b, n_ctx // blk),
        in_specs=[pl.BlockSpec(memory_space=pltpu.SMEM),
                  pl.BlockSpec((None, blk, P_BLK), lambda b_, i: (b_, i, 0)),
                  pl.BlockSpec((None, n_ctx, 128), lambda b_, i: (b_, 0, 2 * P_BLK // 128)),
                  pl.BlockSpec((None, n_ctx, 128), lambda b_, i: (b_, 0, vcol))],
        out_specs=pl.BlockSpec((None, blk, 512), lambda b_, i: (b_, i, 0)),
        out_shape=jax.ShapeDtypeStruct((b, n_ctx, 512), BF16),
        compiler_params=_params("arbitrary", "arbitrary"),
        name="swa_ctx",
    )(sink, pc, pc, pc)


def _mla_kernel(q_ref, kv_ref, vt_ref, wv_ref, o_ref, m_ref, l_ref, acc_ref):
    j = pl.program_id(2)

    @pl.when(j == 0)
    def _():
        m_ref[...] = jnp.full(m_ref.shape, NEG_BIG, F32)
        l_ref[...] = jnp.zeros(l_ref.shape, F32)
        acc_ref[...] = jnp.zeros(acc_ref.shape, F32)

    nh, tq, dq = q_ref.shape
    kv = kv_ref[...]
    vt = vt_ref[...]
    qc = min(MLA_QUERY_CHUNK, tq)
    per_head = tq // qc
    n_chunk = nh * per_head

    def scores(c):
        h, r = divmod(c, per_head)
        return _dot_nt(kv, q_ref[h, r * qc:(r + 1) * qc, :])

    def accumulate(pending):
        cols, alpha, p = pending
        acc_ref[:, cols] = alpha * acc_ref[:, cols] + _dot(vt, p)

    s_next = scores(0)
    pending = None
    for c in range(n_chunk):
        s = s_next
        if c + 1 < n_chunk:
            s_next = scores(c + 1)
        if pending is not None:
            accumulate(pending)
        cols = slice(c * qc, (c + 1) * qc)
        m_prev = m_ref[:, cols]
        m_new = jnp.maximum(m_prev, jnp.max(s, axis=0, keepdims=True))
        alpha = jnp.exp2(m_prev - m_new)
        p = jnp.exp2(s - m_new)
        l_ref[:, cols] = alpha * l_ref[:, cols] + jnp.sum(p, axis=0, keepdims=True)
        m_ref[:, cols] = m_new
        pending = (cols, alpha, p.astype(BF16))
    accumulate(pending)

    @pl.when(j == pl.num_programs(2) - 1)
    def _():
        o = (acc_ref[...] / l_ref[...]).astype(BF16)
        for h in range(nh):
            o_ref[:, h * MLA_V_DIM:(h + 1) * MLA_V_DIM] = _dot_tn(
                o[:, h * tq:(h + 1) * tq], wv_ref[h]).astype(BF16)


def _mla(q, kcat, wv):
    b, nh, sq, dq = q.shape
    sk = kcat.shape[1]
    tq = min(MLA_QUERY_TILE, sq)
    tk = next(t for t in (1280, 640, 512, 384, 256, 128) if sk % t == 0)
    vt = jnp.swapaxes(kcat[:, :, :MLA_KV_RANK], 1, 2)
    return pl.pallas_call(
        _mla_kernel,
        grid=(b, sq // tq, sk // tk),
        in_specs=[pl.BlockSpec((None, nh, tq, dq), lambda b_, i, j: (b_, 0, i, 0)),
                  pl.BlockSpec((None, tk, dq), lambda b_, i, j: (b_, j, 0)),
                  pl.BlockSpec((None, MLA_KV_RANK, tk), lambda b_, i, j: (b_, 0, j)),
                  pl.BlockSpec((nh, MLA_KV_RANK, MLA_V_DIM), lambda b_, i, j: (0, 0, 0))],
        out_specs=pl.BlockSpec((None, tq, nh * MLA_V_DIM), lambda b_, i, j: (b_, i, 0)),
        out_shape=jax.ShapeDtypeStruct((b, sq, nh * MLA_V_DIM), BF16),
        scratch_shapes=[pltpu.VMEM((1, nh * tq), F32), pltpu.VMEM((1, nh * tq), F32),
                        pltpu.VMEM((MLA_KV_RANK, nh * tq), F32)],
        compiler_params=_params("arbitrary", "arbitrary", "arbitrary"),
        name="mla",
    )(q, kcat, vt, wv)


def _gla_kernel(qkf_ref, vf_ref, lrf_ref, qkb_ref, vb_ref, lrb_ref, wd_ref, bd_ref, s0_ref,
                of_ref, ob_ref, sfin_ref, la_ref, st_ref):
    i = pl.program_id(1)
    qk_refs, v_refs, lr_refs, o_refs = (qkf_ref, qkb_ref), (vf_ref, vb_ref), (lrf_ref, lrb_ref), (of_ref, ob_ref)
    tc = qkf_ref.shape[0]
    n_chunk = tc // GLA_CHUNK
    nk = GLA_HEADS * GLA_DK

    @pl.when(i == 0)
    def _():
        st_ref[...] = s0_ref[...]

    row = lax.broadcasted_iota(I32, (GLA_CHUNK, GLA_CHUNK), 0)
    col = lax.broadcasted_iota(I32, (GLA_CHUNK, GLA_CHUNK), 1)
    tris = (col <= row, col >= row)
    tri_bs = tuple(jnp.where(t, 1.0, 0.0).astype(BF16) for t in tris)
    for d in range(2):
        z = _dot(lr_refs[d][...], wd_ref[d]) + bd_ref[d]
        la_ref[d] = jax.nn.log_sigmoid(z) * (1.0 / GLA_TAU)

    states = [[st_ref[d, h] for h in range(GLA_HEADS)] for d in range(2)]
    for c in range(n_chunk):
        for d in range(2):
            cc = c if d == 0 else n_chunk - 1 - c
            rows = slice(cc * GLA_CHUNK, (cc + 1) * GLA_CHUNK)
            la = la_ref[d, rows, :]
            la_hi, la_lo = _split_bf16(la)
            bc = _dot(tri_bs[d], la_hi) + _dot(tri_bs[d], la_lo)
            bend = jnp.sum(la, axis=0, keepdims=True)
            qk = qk_refs[d][rows, :].astype(F32)
            q = qk[:, 0:nk]
            k = qk[:, nk:2 * nk]
            v = v_refs[d][rows, :]
            qd = (q * (GLA_DK ** -0.5) * jnp.exp(bc)).astype(BF16)
            ki = (k * jnp.exp(-bc)).astype(BF16)
            kd = (k * jnp.exp(bend - bc)).astype(BF16)
            dec = jnp.exp(bend)
            for h in range(GLA_HEADS):
                ks = slice(h * GLA_DK, (h + 1) * GLA_DK)
                vs = slice(h * GLA_DV, (h + 1) * GLA_DV)
                vh = v[:, vs]
                sc = jnp.where(tris[d], _dot_nt(qd[:, ks], ki[:, ks]), 0.0)
                st = states[d][h]
                o = _dot(sc.astype(BF16), vh) + _dot_nt(qd[:, ks], st.astype(BF16))
                o_refs[d][rows, vs] = o
                states[d][h] = st * dec[:, ks] + _dot_tn(vh, kd[:, ks])
    for d in range(2):
        for h in range(GLA_HEADS):
            st_ref[d, h] = states[d][h]

    @pl.when(i == pl.num_programs(1) - 1)
    def _():
        sfin_ref[...] = st_ref[...]


def _gla(p, wd_pad, bd, s0):
    b, s, _ = p.shape
    tc = min(512, s)
    n_step = s // tc
    fwd = lambda col: (lambda b_, i: (b_, i, col))
    bwd = lambda col: (lambda b_, i: (b_, n_step - 1 - i, col))
    lr_col = 9 * P_BLK // 128
    state_spec = pl.BlockSpec((None, 2, GLA_HEADS, GLA_DV, GLA_DK), lambda b_, i: (b_, 0, 0, 0, 0))
    return pl.pallas_call(
        _gla_kernel,
        grid=(b, n_step),
        in_specs=[pl.BlockSpec((None, tc, P_BLK), fwd(4)),
                  pl.BlockSpec((None, tc, P_BLK), fwd(5)),
                  pl.BlockSpec((None, tc, 128), fwd(lr_col)),
                  pl.BlockSpec((None, tc, P_BLK), bwd(4)),
                  pl.BlockSpec((None, tc, P_BLK), bwd(5)),
                  pl.BlockSpec((None, tc, 128), bwd(lr_col)),
                  pl.BlockSpec((2, 128, 256), lambda b_, i: (0, 0, 0)),
                  pl.BlockSpec((2, 1, 256), lambda b_, i: (0, 0, 0)),
                  state_spec],
        out_specs=[pl.BlockSpec((None, tc, 512), fwd(0)),
                   pl.BlockSpec((None, tc, 512), bwd(0)),
                   state_spec],
        out_shape=[jax.ShapeDtypeStruct((b, s, 512), F32),
                   jax.ShapeDtypeStruct((b, s, 512), F32),
                   jax.ShapeDtypeStruct((b, 2, GLA_HEADS, GLA_DV, GLA_DK), F32)],
        scratch_shapes=[pltpu.VMEM((2, tc, 256), F32), pltpu.VMEM((2, GLA_HEADS, GLA_DV, GLA_DK), F32)],
        compiler_params=_params("arbitrary", "arbitrary"),
        name="gla",
    )(p, p, p, p, p, p, wd_pad, bd, s0)


def _smlp_kernel(u_ref, v_ref, g_ref, ws_ref, bs_ref, o_ref):
    n_chunk = u_ref.shape[0] // SMLP_CHUNK
    g = g_ref[...]
    for c in range(n_chunk):
        rs = slice(c * SMLP_CHUNK, (c + 1) * SMLP_CHUNK)
        vb = _rms(jax.nn.gelu(v_ref[rs, :].astype(F32)), g).astype(BF16)
        u = jax.nn.gelu(u_ref[rs, :].astype(F32))
        for k in range(SMLP_GROUPS):
            cs = slice(k * SMLP_GROUP_DIM, (k + 1) * SMLP_GROUP_DIM)
            mixed = _dot(ws_ref[k], vb[:, cs]) + bs_ref[:, cs]
            o_ref[rs, cs] = (u[:, cs] * mixed).astype(BF16)


def _smlp(p, v_norm, ws, bs_full):
    b, s, _ = p.shape
    tc = min(512, s)
    return pl.pallas_call(
        _smlp_kernel,
        grid=(b, s // tc),
        in_specs=[pl.BlockSpec((None, tc, P_BLK), lambda b_, i: (b_, i, 7)),
                  pl.BlockSpec((None, tc, P_BLK), lambda b_, i: (b_, i, 8)),
                  pl.BlockSpec((1, 512), lambda b_, i: (0, 0)),
                  pl.BlockSpec((SMLP_GROUPS, SMLP_CHUNK, SMLP_CHUNK), lambda b_, i: (0, 0, 0)),
                  pl.BlockSpec((SMLP_CHUNK, 512), lambda b_, i: (0, 0))],
        out_specs=pl.BlockSpec((None, tc, 512), lambda b_, i: (b_, i, 0)),
        out_shape=jax.ShapeDtypeStruct((b, s, 512), BF16),
        compiler_params=_params("arbitrary", "arbitrary"),
        name="smlp",
    )(p, p, v_norm, ws, bs_full)


def _merge_kernel(h_ref, oa_ref, om_ref, gf_ref, gb_ref, gr_ref, os_ref, gn_ref,
                  wg_ref, bg_ref, wb_ref, y_ref, og_ref):
    @pl.when(pl.program_id(2) == 0)
    def _():
        o = gf_ref[...] + gb_ref[...]
        r = gr_ref[...].astype(F32)
        gate = r * jax.nn.sigmoid(r)
        for hh in range(GLA_HEADS):
            vs = slice(hh * GLA_DV, (hh + 1) * GLA_DV)
            og_ref[:, vs] = (_rms(o[:, vs], gn_ref[...]) * gate[:, vs]).astype(BF16)

    h = h_ref[...]
    branches = (oa_ref[...], om_ref[...], og_ref[...], os_ref[...])
    y = None
    for n, o in enumerate(branches):
        gate = jax.nn.sigmoid(_dot(h, wg_ref[n]) + bg_ref[n])
        t = gate * _dot(o, wb_ref[n])
        y = t if y is None else y + t
    y_ref[...] = y.astype(BF16)


def _merge(h, o_a, o_m, o_gf, o_gb, p, o_s, gla_norm, wg, bg, wb):
    b, s, d = h.shape
    tm = min(512, s)
    tn = 512
    row = lambda b_, i, j: (b_, i, 0)
    return pl.pallas_call(
        _merge_kernel,
        grid=(b, s // tm, d // tn),
        in_specs=[pl.BlockSpec((None, tm, d), row),
                  pl.BlockSpec((None, tm, 512), row),
                  pl.BlockSpec((None, tm, 512), row),
                  pl.BlockSpec((None, tm, 512), row),
                  pl.BlockSpec((None, tm, 512), row),
                  pl.BlockSpec((None, tm, P_BLK), lambda b_, i, j: (b_, i, 6)),
                  pl.BlockSpec((None, tm, 512), row),
                  pl.BlockSpec((1, GLA_DV), lambda b_, i, j: (0, 0)),
                  pl.BlockSpec((4, d, tn), lambda b_, i, j: (0, 0, j)),
                  pl.BlockSpec((4, 1, tn), lambda b_, i, j: (0, 0, j)),
                  pl.BlockSpec((4, 512, tn), lambda b_, i, j: (0, 0, j))],
        out_specs=pl.BlockSpec((None, tm, tn), lambda b_, i, j: (b_, i, j)),
        out_shape=jax.ShapeDtypeStruct((b, s, d), BF16),
        scratch_shapes=[pltpu.VMEM((tm, 512), BF16)],
        compiler_params=_params("arbitrary", "arbitrary", "arbitrary"),
        name="merge",
    )(h, o_a, o_m, o_gf, o_gb, p, o_s, gla_norm, wg, bg, wb)


def _outproj_kernel(y_ref, x_ref, mod_ref, g_ref, wo_ref, wr_ref, br_ref, xo_ref, h2_ref, lg_ref):
    w_hi, w_lo = _split_bf16(wr_ref[...])
    tm = y_ref.shape[0]
    n_part = 2 if tm % 16 == 0 else 1
    rows = [slice(n * tm // n_part, (n + 1) * tm // n_part) for n in range(n_part)]
    z_next = _dot(y_ref[rows[0], :], wo_ref[...])
    for n, rs in enumerate(rows):
        z = z_next
        if n + 1 < n_part:
            z_next = _dot(y_ref[rows[n + 1], :], wo_ref[...])
        xn = x_ref[rs, :] + mod_ref[2:3, :] * _rms(z, g_ref[1:2, :])
        xo_ref[rs, :] = xn
        h2 = _rms(xn, g_ref[2:3, :]) * (1.0 + mod_ref[4:5, :]) + mod_ref[3:4, :]
        h2_ref[rs, :] = h2
        h_hi, h_lo = _split_bf16(h2)
        lg_ref[rs, :] = _dot(h_hi, w_hi) + _dot(h_lo, w_hi) + _dot(h_hi, w_lo) + br_ref[...]


def _outproj(y, x, mod, norm_g, w_out, wr_pad, br_pad):
    b, s, d = x.shape
    tm = min(256, s)
    row = lambda b_, i: (b_, i, 0)
    return pl.pallas_call(
        _outproj_kernel,
        grid=(b, s // tm),
        in_specs=[pl.BlockSpec((None, tm, d), row),
                  pl.BlockSpec((None, tm, d), row),
                  pl.BlockSpec((None, 8, d), lambda b_, i: (b_, 0, 0)),
                  pl.BlockSpec((4, d), lambda b_, i: (0, 0)),
                  pl.BlockSpec((d, d), lambda b_, i: (0, 0)),
                  pl.BlockSpec((d, LANE), lambda b_, i: (0, 0)),
                  pl.BlockSpec((1, LANE), lambda b_, i: (0, 0))],
        out_specs=[pl.BlockSpec((None, tm, d), row),
                   pl.BlockSpec((None, tm, d), row),
                   pl.BlockSpec((None, tm, LANE), row)],
        out_shape=[jax.ShapeDtypeStruct((b, s, d), F32),
                   jax.ShapeDtypeStruct((b, s, d), F32),
                   jax.ShapeDtypeStruct((b, s, LANE), F32)],
        compiler_params=_params("arbitrary", "arbitrary"),
        name="outproj",
    )(y, x, mod, norm_g, w_out, wr_pad, br_pad)


def _route_kernel(lg_ref, idx_ref, w_ref, rank_ref, cnt_ref, carry_ref):
    i = pl.program_id(0)

    @pl.when(i == 0)
    def _():
        carry_ref[...] = jnp.zeros(carry_ref.shape, F32)

    lg = lg_ref[...]
    tm = lg.shape[0]
    lane = lax.broadcasted_iota(I32, lg.shape, 1).astype(F32)
    vals, idxs = [], []
    onehot = jnp.zeros(lg.shape, F32)
    for _ in range(TOP_K):
        m = jnp.max(lg, axis=-1, keepdims=True)
        sel = jnp.min(jnp.where(lg == m, lane, float(LANE)), axis=-1, keepdims=True)
        hit = lane == sel
        vals.append(m)
        idxs.append(sel)
        onehot = jnp.where(hit, 1.0, onehot)
        lg = jnp.where(hit, NEG_BIG, lg)
    es = [jnp.exp(v - vals[0]) for v in vals]
    den = es[0] + es[1] + es[2] + es[3]
    r = lax.broadcasted_iota(I32, (tm, tm), 0)
    c = lax.broadcasted_iota(I32, (tm, tm), 1)
    lower = (c < r).astype(F32).astype(BF16)
    before = _dot(lower, onehot.astype(BF16)) + carry_ref[...]
    for k in range(TOP_K):
        idx_ref[:, k:k + 1] = idxs[k].astype(I32)
        w_ref[:, k:k + 1] = es[k] / den
        rank_ref[:, k:k + 1] = jnp.sum(jnp.where(lane == idxs[k], before, 0.0), axis=-1,
                                       keepdims=True).astype(I32)
    carry_ref[...] = carry_ref[...] + jnp.sum(onehot, axis=0, keepdims=True)
    cnt_ref[...] = carry_ref[...]


def _route(logits):
    t = logits.shape[0]
    tm = 256
    return pl.pallas_call(
        _route_kernel,
        grid=(t // tm,),
        in_specs=[pl.BlockSpec((tm, LANE), lambda i: (i, 0))],
        out_specs=[pl.BlockSpec((tm, TOP_K), lambda i: (i, 0)),
                   pl.BlockSpec((tm, TOP_K), lambda i: (i, 0)),
                   pl.BlockSpec((tm, TOP_K), lambda i: (i, 0)),
                   pl.BlockSpec((1, LANE), lambda i: (0, 0))],
        out_shape=[jax.ShapeDtypeStruct((t, TOP_K), I32),
                   jax.ShapeDtypeStruct((t, TOP_K), F32),
                   jax.ShapeDtypeStruct((t, TOP_K), I32),
                   jax.ShapeDtypeStruct((1, LANE), F32)],
        scratch_shapes=[pltpu.VMEM((1, LANE), F32)],
        compiler_params=_params("arbitrary"),
        name="route",
    )(logits)


def _dispatch_kernel(pad_ref, dest_hbm, src_ref, *rest, zero_pads):
    xs_out, dest_smem, zero_ref, idx_sem, sem = rest[-5:]
    i = pl.program_id(0)
    tm = src_ref.shape[0]
    cp = pltpu.make_async_copy(dest_hbm.at[i], dest_smem, idx_sem)
    cp.start()

    if zero_pads:
        @pl.when(i == 0)
        def _():
            zero_ref[...] = jnp.zeros(zero_ref.shape, zero_ref.dtype)

            def pad_copy(e, r):
                return pltpu.make_async_copy(zero_ref.at[pl.ds(0, 1)],
                                             xs_out.at[pl.ds(pad_ref[e] + r, 1)], sem)

            def start_pads(e, carry):
                def body(r, c):
                    pad_copy(e, r).start()
                    return c
                return lax.fori_loop(0, pad_ref[N_EXPERTS + e], body, carry)

            def wait_pads(e, carry):
                def body(r, c):
                    pad_copy(e, r).wait()
                    return c
                return lax.fori_loop(0, pad_ref[N_EXPERTS + e], body, carry)

            lax.fori_loop(0, N_EXPERTS, start_pads, 0)
            lax.fori_loop(0, N_EXPERTS, wait_pads, 0)

    cp.wait()

    def row_copy(t, k):
        return pltpu.make_async_copy(src_ref.at[pl.ds(t, 1)],
                                     xs_out.at[pl.ds(dest_smem[t * TOP_K + k], 1)], sem)

    def issue(t):
        for k in range(TOP_K):
            row_copy(t, k).start()

    def drain(t):
        for k in range(TOP_K):
            row_copy(0, k).wait()

    _for_each_row(tm, issue)
    _for_each_row(tm, drain)


def _dispatch(pads, dest, src, xs, n_rows):
    t, d = src.shape
    tm = min(512, t)
    dest2 = dest.reshape(t // tm, tm * TOP_K)
    first = xs is None
    args = (pads, dest2, src) + (() if first else (xs,))
    return pl.pallas_call(
        functools.partial(_dispatch_kernel, zero_pads=first),
        grid=(t // tm,),
        in_specs=[pl.BlockSpec(memory_space=pltpu.SMEM),
                  pl.BlockSpec(memory_space=pl.ANY),
                  pl.BlockSpec((tm, d), lambda i: (i, 0))] + ([] if first else [pl.BlockSpec(memory_space=pl.ANY)]),
        out_specs=pl.BlockSpec(memory_space=pl.ANY),
        out_shape=jax.ShapeDtypeStruct((n_rows, d), src.dtype),
        scratch_shapes=[pltpu.SMEM((tm * TOP_K,), I32), pltpu.VMEM((8, d), src.dtype),
                        pltpu.SemaphoreType.DMA, pltpu.SemaphoreType.DMA],
        input_output_aliases={} if first else {3: 0},
        compiler_params=_params("arbitrary"),
        name="dispatch",
    )(*args)


def _expert_kernel(te_ref, nu_ref, x_ref, w1_ref, b1_ref, w2_ref, b2_ref, y_ref):
    @pl.when(pl.program_id(0) < nu_ref[0])
    def _():
        z = _dot(x_ref[...].astype(BF16), w1_ref[...]) + b1_ref[...]
        glu = jnp.minimum(z[:, :D_EXPERT], SWIGLU_LIMIT)
        lin = jnp.clip(z[:, D_EXPERT:], -SWIGLU_LIMIT, SWIGLU_LIMIT)
        act = glu * jax.nn.sigmoid(SWIGLU_ALPHA * glu) * (lin + 1.0)
        y_ref[...] = _dot(act.astype(BF16), w2_ref[...]) + b2_ref[...]


def _experts(layer, tile_e, n_used, xs, w1, b1, w2, b2):
    n_rows, d = xs.shape
    tr = EXPERT_TILE
    n_tiles = n_rows // tr
    row = lambda i, te, nu: (jnp.minimum(i, nu[0] - 1), 0)
    wsel = lambda i, te, nu: (layer, te[i], 0, 0)
    return pl.pallas_call(
        _expert_kernel,
        grid_spec=pltpu.PrefetchScalarGridSpec(
            num_scalar_prefetch=2,
            grid=(n_tiles,),
            in_specs=[pl.BlockSpec((tr, d), row),
                      pl.BlockSpec((None, None, d, 2 * D_EXPERT), wsel),
                      pl.BlockSpec((None, None, 1, 2 * D_EXPERT), wsel),
                      pl.BlockSpec((None, None, D_EXPERT, d), wsel),
                      pl.BlockSpec((None, None, 1, d), wsel)],
            out_specs=pl.BlockSpec((tr, d), row)),
        out_shape=jax.ShapeDtypeStruct((n_rows, d), F32),
        compiler_params=_params("arbitrary"),
        name="experts",
    )(tile_e, n_used, xs, w1, b1, w2, b2)


def _combine_kernel(dest_hbm, w_ref, x_ref, mod_ref, g_ref, ys_hbm, o_ref,
                    dest0, dest1, buf0, buf1, idx_sem, sem0, sem1):
    dest_smem, buf, sem = (dest0, dest1), (buf0, buf1), (sem0, sem1)
    tm = x_ref.shape[0]
    n_step = pl.num_programs(0) * pl.num_programs(1)
    i = pl.program_id(0) * pl.num_programs(1) + pl.program_id(1)

    def row_copy(sl, t, k):
        return pltpu.make_async_copy(ys_hbm.at[pl.ds(dest_smem[sl][t * TOP_K + k], 1)],
                                     buf[sl].at[k, pl.ds(t, 1)], sem[sl])

    def start_gather(step, sl):
        cp = pltpu.make_async_copy(dest_hbm.at[step], dest_smem[sl], idx_sem)
        cp.start()
        cp.wait()

        def issue(t):
            for k in range(TOP_K):
                row_copy(sl, t, k).start()

        _for_each_row(tm, issue)

    @pl.when(i == 0)
    def _():
        start_gather(0, 0)

    def step(slot):
        @pl.when(i + 1 < n_step)
        def _():
            start_gather(i + 1, 1 - slot)

        def drain(t):
            for k in range(TOP_K):
                row_copy(slot, 0, k).wait()

        _for_each_row(tm, drain)
        w = w_ref[...]
        f = w[:, 0:1] * buf[slot][0]
        for k in range(1, TOP_K):
            f = f + w[:, k:k + 1] * buf[slot][k]
        o_ref[...] = x_ref[...] + mod_ref[5:6, :] * _rms(f, g_ref[3:4, :])

    for slot in range(2):
        pl.when(lax.rem(i, 2) == slot)(functools.partial(step, slot))


def _combine(dest, w, x, mod, norm_g, ys):
    b, s, d = x.shape
    tm = min(256, s)
    ns = s // tm
    dest2 = dest.reshape(b * ns, tm * TOP_K)
    return pl.pallas_call(
        _combine_kernel,
        grid=(b, ns),
        in_specs=[pl.BlockSpec(memory_space=pl.ANY),
                  pl.BlockSpec((None, tm, TOP_K), lambda b_, i: (b_, i, 0)),
                  pl.BlockSpec((None, tm, d), lambda b_, i: (b_, i, 0)),
                  pl.BlockSpec((None, 8, d), lambda b_, i: (b_, 0, 0)),
                  pl.BlockSpec((4, d), lambda b_, i: (0, 0)),
                  pl.BlockSpec(memory_space=pl.ANY)],
        out_specs=pl.BlockSpec((None, tm, d), lambda b_, i: (b_, i, 0)),
        out_shape=jax.ShapeDtypeStruct((b, s, d), F32),
        scratch_shapes=[pltpu.SMEM((tm * TOP_K,), I32), pltpu.SMEM((tm * TOP_K,), I32),
                        pltpu.VMEM((TOP_K, tm, d), F32), pltpu.VMEM((TOP_K, tm, d), F32),
                        pltpu.SemaphoreType.DMA, pltpu.SemaphoreType.DMA, pltpu.SemaphoreType.DMA],
        compiler_params=_params("arbitrary", "arbitrary"),
        name="combine",
    )(dest2, w.reshape(b, s, TOP_K), x, mod, norm_g, ys)


def _rot_half_cols(w, head_dim):
    d_in, n = w.shape
    w3 = w.reshape(d_in, n // head_dim, head_dim)
    half = head_dim // 2
    return jnp.concatenate([-w3[..., half:], w3[..., :half]], axis=-1).reshape(d_in, n)


def _extended_in_weights(w_in):
    sizes = (512, 128, 128, MLA_Q_RANK, MLA_KV_RANK, MLA_ROPE_DIM, 256, 256, 512,
             2 * GLA_DECAY_RANK, 512, 512, 512)
    cols, off = [], 0
    for n in sizes:
        cols.append(w_in[:, off:off + n])
        off += n
    (a_q, a_k, a_v, m_cq, m_ckv, m_kr, g_q, g_k, g_v, g_lr, g_r, s_u, s_v) = cols
    pad = jnp.zeros((w_in.shape[0], P_BLK - 2 * GLA_DECAY_RANK), w_in.dtype)
    ext = jnp.concatenate(
        [a_q, _rot_half_cols(a_q, SWA_HEAD_DIM),
         a_k, _rot_half_cols(a_k, SWA_HEAD_DIM), a_v, m_kr, _rot_half_cols(m_kr, MLA_ROPE_DIM),
         m_cq, m_ckv, g_q, g_k, g_v, g_r, s_u, s_v, g_lr, pad], axis=1)
    return ext.astype(BF16)


def _rope_tables(n_tok):
    n_rows = n_tok // GRID_W
    rows = jnp.repeat(jnp.arange(n_rows, dtype=F32), GRID_W)
    cols = jnp.tile(jnp.arange(GRID_W, dtype=F32), n_rows)
    n_freq = SWA_HEAD_DIM // 4
    inv_freq = ROPE_BASE ** (-jnp.arange(n_freq, dtype=F32) / n_freq)
    ang = jnp.concatenate([rows[:, None] * inv_freq, cols[:, None] * inv_freq], axis=-1)
    cos = jnp.concatenate([jnp.cos(ang), jnp.cos(ang)], axis=-1)
    sin = jnp.concatenate([jnp.sin(ang), jnp.sin(ang)], axis=-1)
    return jnp.tile(cos, (1, 8)), jnp.tile(sin, (1, 8))


def _moe(layer, h2_parts, logits_parts, w1, b1, w2, b2):
    d = h2_parts[0].shape[1]
    logits = jnp.concatenate(logits_parts, axis=0)
    t_all = logits.shape[0]
    top_idx, top_w, rank, counts = _route(logits)
    counts = counts[0, :N_EXPERTS].astype(I32)
    tr = EXPERT_TILE
    padded = (counts + tr - 1) // tr * tr
    p_end = jnp.cumsum(padded)
    p_start = p_end - padded
    dest = p_start[top_idx] + rank
    n_tiles = -(-t_all * TOP_K // tr) + N_EXPERTS
    tile_start = jnp.arange(n_tiles, dtype=I32) * tr
    tile_e = jnp.minimum(jnp.sum((p_end[None, :] <= tile_start[:, None]).astype(I32), axis=1),
                         N_EXPERTS - 1)
    n_used = (p_end[-1:] // tr).astype(I32)
    pads = jnp.concatenate([p_start + counts, padded - counts]).astype(I32)
    xs = None
    off = 0
    for part in h2_parts:
        n = part.shape[0]
        xs = _dispatch(pads, dest[off:off + n], part, xs, n_tiles * tr)
        off += n
    ys = _experts(layer, tile_e, n_used, xs, w1, b1, w2, b2)
    return dest, top_w, ys


def kernel(x, c, ctx, c_ctx, ada_w, ada_b, norm_g, w_in, attn_sink, mla_q_norm, mla_w_uq, mla_kv_norm, mla_w_ukv, gla_w_decay, gla_b_decay, gla_out_norm, smlp_v_norm, smlp_w_spatial, smlp_b_spatial, w_branch, w_gate, b_gate, w_out, router_w, router_b, expert_w1, expert_b1, expert_w2, expert_b2):
    bsz, seq, d = x.shape
    n_ctx = ctx.shape[1]
    depth = ada_w.shape[0]

    cs = jnp.zeros((8, d), F32).at[:bsz].set(c).at[bsz].set(c_ctx)
    mod_all = _ada(cs, ada_w, ada_b).reshape(depth, 8, N_MOD, d)
    mod_all = jnp.pad(mod_all, ((0, 0), (0, 0), (0, 8 - N_MOD), (0, 0)))

    cos_t, sin_t = _rope_tables(seq)
    cos_c = jnp.ones((bsz * n_ctx, 512), F32)
    sin_c = jnp.zeros((bsz * n_ctx, 512), F32)
    ctx = ctx.reshape(1, bsz * n_ctx, d)

    w1 = expert_w1.astype(BF16)
    b1 = expert_b1.reshape(depth, N_EXPERTS, 1, 2 * D_EXPERT)
    w2 = expert_w2.astype(BF16)
    b2 = expert_b2.reshape(depth, N_EXPERTS, 1, d)

    for l in range(depth):
        last = l == depth - 1
        mod = mod_all[l, :bsz]
        mod_c = mod_all[l, bsz:bsz + 1]
        g_l = norm_g[l]
        w_ext = _extended_in_weights(w_in[l])
        uq = mla_w_uq[l].reshape(MLA_Q_RANK, MLA_HEADS, MLA_NOPE_DIM + MLA_ROPE_DIM)
        uq_rope = uq[:, :, MLA_NOPE_DIM:].reshape(MLA_Q_RANK, MLA_HEADS * MLA_ROPE_DIM)
        wuq_ext = jnp.concatenate(
            [uq[:, :, :MLA_NOPE_DIM].reshape(MLA_Q_RANK, MLA_HEADS * MLA_NOPE_DIM), uq_rope,
             _rot_half_cols(uq_rope, MLA_ROPE_DIM)], axis=1).astype(BF16)
        ukv = mla_w_ukv[l].reshape(MLA_KV_RANK, MLA_HEADS, MLA_NOPE_DIM + MLA_V_DIM)
        wkn = jnp.transpose(ukv[:, :, :MLA_NOPE_DIM], (1, 0, 2)).astype(BF16)
        wv = jnp.transpose(ukv[:, :, MLA_NOPE_DIM:], (1, 0, 2)).astype(BF16)
        qn_g = mla_q_norm[l].reshape(1, MLA_Q_RANK)
        kv_g = mla_kv_norm[l].reshape(1, MLA_KV_RANK)
        wd_pad = jnp.zeros((2, 128, 256), F32)
        for dd in range(2):
            wd_pad = wd_pad.at[dd, dd * GLA_DECAY_RANK:(dd + 1) * GLA_DECAY_RANK].set(gla_w_decay[l, dd])
        wd_pad = wd_pad.astype(BF16)
        bd = gla_b_decay[l].reshape(2, 1, 256)
        gla_norm = gla_out_norm[l].reshape(1, GLA_DV)
        v_norm = smlp_v_norm[l].reshape(1, 512)
        ws = smlp_w_spatial[l].astype(BF16)
        bs_full = jnp.repeat(smlp_b_spatial[l].T, SMLP_GROUP_DIM, axis=1)
        wg = w_gate[l].astype(BF16)
        bg = b_gate[l].reshape(4, 1, d)
        wb = w_branch[l].astype(BF16)
        wo = w_out[l].astype(BF16)
        wr_pad = jnp.pad(router_w[l], ((0, 0), (0, LANE - N_EXPERTS)))
        br_pad = jnp.pad(router_b[l].reshape(1, N_EXPERTS), ((0, 0), (0, LANE - N_EXPERTS)),
                         constant_values=NEG_BIG)
        sink = attn_sink[l]

        h, p = _inproj(x, mod, g_l, w_ext)
        hc, pc1 = _inproj(ctx, mod_c, g_l, w_ext)
        pc = pc1.reshape(bsz, n_ctx, P_WIDTH)
        ka, kcat, q_m = _prep(p, cos_t, sin_t, qn_g, kv_g, wuq_ext, wkn)
        _, kcat_c, q_mc = _prep(pc1, cos_c, sin_c, qn_g, kv_g, wuq_ext, wkn)
        kcat_c = kcat_c.reshape(bsz, n_ctx, MLA_QK_PAD)

        o_a = _swa(sink, p, ka, cos_t, sin_t, pc)
        o_m = _mla(q_m, jnp.concatenate([kcat, kcat_c], axis=1), wv)
        s0 = jnp.zeros((bsz, 2, GLA_HEADS, GLA_DV, GLA_DK), F32)
        ogf_c, ogb_c, s_ctx = _gla(pc, wd_pad, bd, s0)
        ogf, ogb, _ = _gla(p, wd_pad, bd, s_ctx)
        o_s = _smlp(p, v_norm, ws, bs_full)
        y = _merge(h, o_a, o_m, ogf, ogb, p, o_s, gla_norm, wg, bg, wb)
        x, h2, lg = _outproj(y, x, mod, g_l, wo, wr_pad, br_pad)
        h2_parts = [h2.reshape(bsz * seq, d)]
        lg_parts = [lg.reshape(bsz * seq, LANE)]

        if not last:
            oc_a = _swa_ctx(sink, pc)
            q_mc = q_mc.reshape(MLA_HEADS, bsz, n_ctx, MLA_QK_PAD).transpose(1, 0, 2, 3)
            oc_m = _mla(q_mc, kcat_c, wv)
            oc_s = _smlp(pc, v_norm, ws, bs_full)
            flat = lambda a: a.reshape(1, bsz * n_ctx, a.shape[-1])
            yc = _merge(hc, flat(oc_a), flat(oc_m), flat(ogf_c), flat(ogb_c), pc1,
                        flat(oc_s), gla_norm, wg, bg, wb)
            ctx, h2c, lgc = _outproj(yc, ctx, mod_c, g_l, wo, wr_pad, br_pad)
            h2_parts.append(h2c.reshape(bsz * n_ctx, d))
            lg_parts.append(lgc.reshape(bsz * n_ctx, LANE))

        dest, top_w, ys = _moe(l, h2_parts, lg_parts, w1, b1, w2, b2)
        n_lat = bsz * seq
        x = _combine(dest[:n_lat], top_w[:n_lat], x, mod, g_l, ys)
        if not last:
            ctx = _combine(dest[n_lat:], top_w[n_lat:], ctx, mod_c, g_l, ys)
    return x
```

```python
import functools

import jax
import jax.numpy as jnp
from jax import lax
from jax.experimental import pallas as pl
from jax.experimental.pallas import tpu as pltpu

F32 = jnp.float32
BF16 = jnp.bfloat16
I32 = jnp.int32

EPS = 1e-6
ROPE_BASE = 10000.0
GRID_W = 64
N_MOD = 6

SWA_HEADS = 8
SWA_KV_HEADS = 2
SWA_GROUP = SWA_HEADS // SWA_KV_HEADS
SWA_HEAD_DIM = 64
SWA_BLOCK = 128

MLA_HEADS = 4
MLA_Q_RANK = 384
MLA_KV_RANK = 128
MLA_NOPE_DIM = 128
MLA_ROPE_DIM = 64
MLA_V_DIM = 128
MLA_QK_PAD = 256
MLA_QUERY_CHUNK = 512
MLA_QUERY_TILE = 2048
LOG2_E = 1.4426950408889634

GLA_HEADS = 4
GLA_DK = 64
GLA_DV = 128
GLA_DECAY_RANK = 16
GLA_TAU = 16.0
GLA_CHUNK = 64

SMLP_GROUPS = 4
SMLP_CHUNK = 128
SMLP_GROUP_DIM = 128

N_EXPERTS = 32
TOP_K = 4
D_EXPERT = 1024
SWIGLU_LIMIT = 7.0
SWIGLU_ALPHA = 1.702

LANE = 128
NEG_BIG = -1e30
VMEM_LIMIT = 56 * 1024 * 1024

P_BLK = 512
P_NBLK = 10
P_WIDTH = P_BLK * P_NBLK
EXPERT_TILE = 256
ROW_GROUP = 8


def _dot(a, b):
    return jnp.dot(a, b, preferred_element_type=F32)


def _dot_nt(a, b):
    return lax.dot_general(a, b, (((1,), (1,)), ((), ())), preferred_element_type=F32)


def _dot_tn(a, b):
    return lax.dot_general(a, b, (((0,), (0,)), ((), ())), preferred_element_type=F32)


def _split_bf16(a):
    hi = a.astype(BF16)
    lo = (a - hi.astype(F32)).astype(BF16)
    return hi, lo


def _rms(x, g):
    return x * lax.rsqrt(jnp.mean(x * x, axis=-1, keepdims=True) + EPS) * g


def _pack_bf16_pair(a, b):
    hi = pltpu.bitcast(a.astype(BF16).astype(F32), jnp.uint32)
    lo = pltpu.bitcast(b.astype(BF16).astype(F32), jnp.uint32)
    return hi | (lo >> 16)


def _unpack_bf16_pair(w):
    a = pltpu.bitcast(w & jnp.uint32(0xFFFF0000), F32)
    b = pltpu.bitcast(w << 16, F32)
    return a, b


def _for_each_row(n_rows, fn):
    def group(u, carry):
        base = pl.multiple_of(u * ROW_GROUP, ROW_GROUP)
        for j in range(ROW_GROUP):
            fn(base + j)
        return carry

    lax.fori_loop(0, n_rows // ROW_GROUP, group, 0)


def _params(*sem):
    return pltpu.CompilerParams(dimension_semantics=sem, vmem_limit_bytes=VMEM_LIMIT)


def _ada_kernel(c_ref, w_ref, b_ref, o_ref):
    c = c_ref[...]
    a_hi, a_lo = _split_bf16(c * jax.nn.sigmoid(c))
    w_hi, w_lo = _split_bf16(w_ref[...])
    o_ref[...] = _dot(a_hi, w_hi) + _dot(a_lo, w_hi) + _dot(a_hi, w_lo) + b_ref[...]


def _ada(cs, ada_w, ada_b):
    n_layer, d, n = ada_w.shape
    tn = 1024
    return pl.pallas_call(
        _ada_kernel,
        grid=(n_layer, n // tn),
        in_specs=[pl.BlockSpec((8, d), lambda l, j: (0, 0)),
                  pl.BlockSpec((None, d, tn), lambda l, j: (l, 0, j)),
                  pl.BlockSpec((None, 1, tn), lambda l, j: (l, 0, j))],
        out_specs=pl.BlockSpec((None, 8, tn), lambda l, j: (l, 0, j)),
        out_shape=jax.ShapeDtypeStruct((n_layer, 8, n), F32),
        compiler_params=_params("arbitrary", "arbitrary"),
        name="ada",
    )(cs, ada_w, ada_b.reshape(n_layer, 1, n))


def _inproj_kernel(x_ref, mod_ref, g_ref, w_ref, h_ref, p_ref):
    y = _rms(x_ref[...], g_ref[0:1, :])
    hb = (y * (1.0 + mod_ref[1:2, :]) + mod_ref[0:1, :]).astype(BF16)
    h_ref[...] = hb
    p_ref[...] = _dot(hb, w_ref[...]).astype(BF16)


def _inproj(x, mod, norm_g, w_ext):
    b, s, d = x.shape
    tm = min(256, s)
    return pl.pallas_call(
        _inproj_kernel,
        grid=(b, s // tm),
        in_specs=[pl.BlockSpec((None, tm, d), lambda b_, i: (b_, i, 0)),
                  pl.BlockSpec((None, 8, d), lambda b_, i: (b_, 0, 0)),
                  pl.BlockSpec((4, d), lambda b_, i: (0, 0)),
                  pl.BlockSpec((d, P_WIDTH), lambda b_, i: (0, 0), pipeline_mode=pl.Buffered(1))],
        out_specs=[pl.BlockSpec((None, tm, d), lambda b_, i: (b_, i, 0)),
                   pl.BlockSpec((None, tm, P_WIDTH), lambda b_, i: (b_, i, 0))],
        out_shape=[jax.ShapeDtypeStruct((b, s, d), BF16),
                   jax.ShapeDtypeStruct((b, s, P_WIDTH), BF16)],
        compiler_params=_params("arbitrary", "arbitrary"),
        name="inproj",
    )(x, mod, norm_g, w_ext)


def _prep_kernel(kblk_ref, cblk_ref, cos_ref, sin_ref, qn_g_ref, kv_g_ref, wuq_ref, wkn_ref,
                 ka_ref, kcat_ref, q_ref):
    cos = cos_ref[...]
    sin = sin_ref[...]
    kb = kblk_ref[...].astype(F32)
    ka_ref[...] = (kb[:, 0:128] * cos[:, 0:128] + kb[:, 128:256] * sin[:, 0:128]).astype(BF16)
    kr = kb[:, 384:448] * cos[:, 0:64] + kb[:, 448:512] * sin[:, 0:64]
    cb = cblk_ref[...].astype(F32)
    ckvn = _rms(cb[:, MLA_Q_RANK:], kv_g_ref[...])
    tm = kb.shape[0]
    kcat_ref[:, 0:128] = ckvn.astype(BF16)
    kcat_ref[:, 128:192] = kr.astype(BF16)
    kcat_ref[:, 192:256] = jnp.zeros((tm, 64), BF16)
    cqn = _rms(cb[:, :MLA_Q_RANK], qn_g_ref[...]).astype(BF16)
    qa = _dot(cqn, wuq_ref[...])
    qr = qa[:, 512:768] * cos[:, 0:256] + qa[:, 768:1024] * sin[:, 0:256]
    scale = (MLA_NOPE_DIM + MLA_ROPE_DIM) ** -0.5 * LOG2_E
    for h in range(MLA_HEADS):
        qn = qa[:, h * 128:(h + 1) * 128].astype(BF16)
        q_ref[h, :, 0:128] = (_dot_nt(qn, wkn_ref[h]) * scale).astype(BF16)
        q_ref[h, :, 128:192] = (qr[:, h * 64:(h + 1) * 64] * scale).astype(BF16)
        q_ref[h, :, 192:256] = jnp.zeros((tm, 64), BF16)


def _prep(p, cos_t, sin_t, qn_g, kv_g, wuq_ext, wkn):
    b, s, _ = p.shape
    tm = min(512, s)
    return pl.pallas_call(
        _prep_kernel,
        grid=(b, s // tm),
        in_specs=[pl.BlockSpec((None, tm, P_BLK), lambda b_, i: (b_, i, 2)),
                  pl.BlockSpec((None, tm, P_BLK), lambda b_, i: (b_, i, 3)),
                  pl.BlockSpec((tm, 512), lambda b_, i: (i, 0)),
                  pl.BlockSpec((tm, 512), lambda b_, i: (i, 0)),
                  pl.BlockSpec((1, MLA_Q_RANK), lambda b_, i: (0, 0)),
                  pl.BlockSpec((1, MLA_KV_RANK), lambda b_, i: (0, 0)),
                  pl.BlockSpec((MLA_Q_RANK, 1024), lambda b_, i: (0, 0)),
                  pl.BlockSpec((MLA_HEADS, MLA_KV_RANK, MLA_NOPE_DIM), lambda b_, i: (0, 0, 0))],
        out_specs=[pl.BlockSpec((None, tm, 128), lambda b_, i: (b_, i, 0)),
                   pl.BlockSpec((None, tm, MLA_QK_PAD), lambda b_, i: (b_, i, 0)),
                   pl.BlockSpec((None, MLA_HEADS, tm, MLA_QK_PAD), lambda b_, i: (b_, 0, i, 0))],
        out_shape=[jax.ShapeDtypeStruct((b, s, 128), BF16),
                   jax.ShapeDtypeStruct((b, s, MLA_QK_PAD), BF16),
                   jax.ShapeDtypeStruct((b, MLA_HEADS, s, MLA_QK_PAD), BF16)],
        compiler_params=_params("arbitrary", "arbitrary"),
        name="prep",
    )(p, p, cos_t, sin_t, qn_g, kv_g, wuq_ext, wkn)


def _swa_group(q, sink_ref, g, k_all, v_all, mask, o_ref):
    heads = range(g * SWA_GROUP, (g + 1) * SWA_GROUP)
    n_q = q.shape[0]
    qs = jnp.concatenate([q[:, h * SWA_HEAD_DIM:(h + 1) * SWA_HEAD_DIM] for h in heads], axis=0)
    sink = jnp.concatenate([jnp.full((n_q, 1), sink_ref[h], F32) for h in heads], axis=0)
    s = _dot_nt(qs, k_all)
    if mask is not None:
        s = s + jnp.concatenate([mask] * SWA_GROUP, axis=0)
    m = jnp.maximum(jnp.max(s, axis=-1, keepdims=True), sink)
    e = jnp.exp(s - m)
    den = jnp.sum(e, axis=-1, keepdims=True) + jnp.exp(sink - m)
    o = _dot(e.astype(BF16), v_all) / den
    for n, h in enumerate(heads):
        o_ref[:, h * SWA_HEAD_DIM:(h + 1) * SWA_HEAD_DIM] = o[n * n_q:(n + 1) * n_q, :].astype(BF16)


def _swa_kernel(sink_ref, q_ref, qr_ref, cos_ref, sin_ref, kp_ref, kc_ref, kn_ref,
                vp_ref, vc_ref, vn_ref, kx_ref, vx_ref, o_ref):
    i = pl.program_id(1)
    nb = pl.num_programs(1)
    scale = SWA_HEAD_DIM ** -0.5
    q = ((q_ref[...].astype(F32) * cos_ref[...] + qr_ref[...].astype(F32) * sin_ref[...])
         * scale).astype(BF16)
    n_ctx = kx_ref.shape[0]
    n_keys = 3 * SWA_BLOCK + n_ctx
    row = lax.broadcasted_iota(I32, (SWA_BLOCK, n_keys), 0)
    col = lax.broadcasted_iota(I32, (SWA_BLOCK, n_keys), 1)
    dist = col - row
    lo = jnp.where(i > 0, 0, n_keys)
    hi = jnp.where(i < nb - 1, 2 * SWA_BLOCK, -n_keys)
    bias_prev = jnp.where(dist >= lo, 0.0, NEG_BIG)
    bias_next = jnp.where(dist <= hi, 0.0, NEG_BIG)
    in_next = (col >= 2 * SWA_BLOCK) & (col < 3 * SWA_BLOCK)
    mask = jnp.where(col < SWA_BLOCK, bias_prev, jnp.where(in_next, bias_next, 0.0))
    for g in range(SWA_KV_HEADS):
        gs = slice(g * SWA_HEAD_DIM, (g + 1) * SWA_HEAD_DIM)
        k_all = jnp.concatenate([kp_ref[:, gs], kc_ref[:, gs], kn_ref[:, gs], kx_ref[:, gs]], axis=0)
        v_all = jnp.concatenate([vp_ref[:, gs], vc_ref[:, gs], vn_ref[:, gs], vx_ref[:, gs]], axis=0)
        _swa_group(q, sink_ref, g, k_all, v_all, mask, o_ref)


def _swa_ctx_kernel(sink_ref, q_ref, kx_ref, vx_ref, o_ref):
    scale = SWA_HEAD_DIM ** -0.5
    q = (q_ref[...].astype(F32) * scale).astype(BF16)
    for g in range(SWA_KV_HEADS):
        gs = slice(g * SWA_HEAD_DIM, (g + 1) * SWA_HEAD_DIM)
        _swa_group(q, sink_ref, g, kx_ref[:, gs], vx_ref[:, gs], None, o_ref)


def _swa(sink, p, ka, cos_t, sin_t, pc):
    b, s, _ = p.shape
    n_ctx = pc.shape[1]
    nb = s // SWA_BLOCK
    blk = SWA_BLOCK
    prev = lambda b_, i: (b_, jnp.maximum(i - 1, 0), 0)
    cur = lambda b_, i: (b_, i, 0)
    nxt = lambda b_, i: (b_, jnp.minimum(i + 1, nb - 1), 0)
    vcol = (2 * P_BLK + 256) // 128
    vprev = lambda b_, i: (b_, jnp.maximum(i - 1, 0), vcol)
    vcur = lambda b_, i: (b_, i, vcol)
    vnxt = lambda b_, i: (b_, jnp.minimum(i + 1, nb - 1), vcol)
    return pl.pallas_call(
        _swa_kernel,
        grid=(b, nb),
        in_specs=[pl.BlockSpec(memory_space=pltpu.SMEM),
                  pl.BlockSpec((None, blk, P_BLK), lambda b_, i: (b_, i, 0)),
                  pl.BlockSpec((None, blk, P_BLK), lambda b_, i: (b_, i, 1)),
                  pl.BlockSpec((blk, 512), lambda b_, i: (i, 0)),
                  pl.BlockSpec((blk, 512), lambda b_, i: (i, 0)),
                  pl.BlockSpec((None, blk, 128), prev),
                  pl.BlockSpec((None, blk, 128), cur),
                  pl.BlockSpec((None, blk, 128), nxt),
                  pl.BlockSpec((None, blk, 128), vprev),
                  pl.BlockSpec((None, blk, 128), vcur),
                  pl.BlockSpec((None, blk, 128), vnxt),
                  pl.BlockSpec((None, n_ctx, 128), lambda b_, i: (b_, 0, 2 * P_BLK // 128)),
                  pl.BlockSpec((None, n_ctx, 128), lambda b_, i: (b_, 0, vcol))],
        out_specs=pl.BlockSpec((None, blk, 512), lambda b_, i: (b_, i, 0)),
        out_shape=jax.ShapeDtypeStruct((b, s, 512), BF16),
        compiler_params=_params("arbitrary", "arbitrary"),
        name="swa",
    )(sink, p, p, cos_t, sin_t, ka, ka, ka, p, p, p, pc, pc)


def _swa_ctx(sink, pc):
    b, n_ctx, _ = pc.shape
    blk = SWA_BLOCK
    vcol = (2 * P_BLK + 256) // 128
    return pl.pallas_call(
        _swa_ctx_kernel,
        grid=(b, n_ctx // blk),
        in_specs=[pl.BlockSpec(memory_space=pltpu.SMEM),
                  pl.BlockSpec((None, blk, P_BLK), lambda b_, i: (b_, i, 0)),
                  pl.BlockSpec((None, n_ctx, 128), lambda b_, i: (b_, 0, 2 * P_BLK // 128)),
                  pl.BlockSpec((None, n_ctx, 128), lambda b_, i: (b_, 0, vcol))],
        out_specs=pl.BlockSpec((None, blk, 512), lambda b_, i: (b_, i, 0)),
        out_shape=jax.ShapeDtypeStruct((b, n_ctx, 512), BF16),
        compiler_params=_params("arbitrary", "arbitrary"),
        name="swa_ctx",
    )(sink, pc, pc, pc)


def _mla_kernel(q_ref, kv_ref, vt_ref, wv_ref, o_ref, m_ref, l_ref, acc_ref):
    j = pl.program_id(2)

    @pl.when(j == 0)
    def _():
        m_ref[...] = jnp.full(m_ref.shape, NEG_BIG, F32)
        l_ref[...] = jnp.zeros(l_ref.shape, F32)
        acc_ref[...] = jnp.zeros(acc_ref.shape, F32)

    nh, tq, dq = q_ref.shape
    kv = kv_ref[...]
    vt = vt_ref[...]
    qc = min(MLA_QUERY_CHUNK, tq)
    per_head = tq // qc
    n_chunk = nh * per_head

    def scores(c):
        h, r = divmod(c, per_head)
        return _dot_nt(kv, q_ref[h, r * qc:(r + 1) * qc, :])

    def accumulate(pending):
        cols, alpha, p = pending
        acc_ref[:, cols] = alpha * acc_ref[:, cols] + _dot(vt, p)

    s_next = scores(0)
    pending = None
    for c in range(n_chunk):
        s = s_next
        if c + 1 < n_chunk:
            s_next = scores(c + 1)
        if pending is not None:
            accumulate(pending)
        cols = slice(c * qc, (c + 1) * qc)
        m_prev = m_ref[:, cols]
        m_new = jnp.maximum(m_prev, jnp.max(s, axis=0, keepdims=True))
        alpha = jnp.exp2(m_prev - m_new)
        p = jnp.exp2(s - m_new)
        l_ref[:, cols] = alpha * l_ref[:, cols] + jnp.sum(p, axis=0, keepdims=True)
        m_ref[:, cols] = m_new
        pending = (cols, alpha, p.astype(BF16))
    accumulate(pending)

    @pl.when(j == pl.num_programs(2) - 1)
    def _():
        o = (acc_ref[...] / l_ref[...]).astype(BF16)
        for h in range(nh):
            o_ref[:, h * MLA_V_DIM:(h + 1) * MLA_V_DIM] = _dot_tn(
                o[:, h * tq:(h + 1) * tq], wv_ref[h]).astype(BF16)


def _mla(q, kcat, wv):
    b, nh, sq, dq = q.shape
    sk = kcat.shape[1]
    tq = min(MLA_QUERY_TILE, sq)
    tk = next(t for t in (1280, 640, 512, 384, 256, 128) if sk % t == 0)
    vt = jnp.swapaxes(kcat[:, :, :MLA_KV_RANK], 1, 2)
    return pl.pallas_call(
        _mla_kernel,
        grid=(b, sq // tq, sk // tk),
        in_specs=[pl.BlockSpec((None, nh, tq, dq), lambda b_, i, j: (b_, 0, i, 0)),
                  pl.BlockSpec((None, tk, dq), lambda b_, i, j: (b_, j, 0)),
                  pl.BlockSpec((None, MLA_KV_RANK, tk), lambda b_, i, j: (b_, 0, j)),
                  pl.BlockSpec((nh, MLA_KV_RANK, MLA_V_DIM), lambda b_, i, j: (0, 0, 0))],
        out_specs=pl.BlockSpec((None, tq, nh * MLA_V_DIM), lambda b_, i, j: (b_, i, 0)),
        out_shape=jax.ShapeDtypeStruct((b, sq, nh * MLA_V_DIM), BF16),
        scratch_shapes=[pltpu.VMEM((1, nh * tq), F32), pltpu.VMEM((1, nh * tq), F32),
                        pltpu.VMEM((MLA_KV_RANK, nh * tq), F32)],
        compiler_params=_params("arbitrary", "arbitrary", "arbitrary"),
        name="mla",
    )(q, kcat, vt, wv)


def _gla_kernel(qkf_ref, vf_ref, lrf_ref, qkb_ref, vb_ref, lrb_ref, wd_ref, bd_ref, s0_ref,
                of_ref, ob_ref, sfin_ref, la_ref, st_ref):
    i = pl.program_id(1)
    qk_refs, v_refs, lr_refs, o_refs = (qkf_ref, qkb_ref), (vf_ref, vb_ref), (lrf_ref, lrb_ref), (of_ref, ob_ref)
    tc = qkf_ref.shape[0]
    n_chunk = tc // GLA_CHUNK
    nk = GLA_HEADS * GLA_DK

    @pl.when(i == 0)
    def _():
        st_ref[...] = s0_ref[...]

    row = lax.broadcasted_iota(I32, (GLA_CHUNK, GLA_CHUNK), 0)
    col = lax.broadcasted_iota(I32, (GLA_CHUNK, GLA_CHUNK), 1)
    tris = (col <= row, col >= row)
    tri_bs = tuple(jnp.where(t, 1.0, 0.0).astype(BF16) for t in tris)
    for d in range(2):
        z = _dot(lr_refs[d][...], wd_ref[d]) + bd_ref[d]
        la_ref[d] = jax.nn.log_sigmoid(z) * (1.0 / GLA_TAU)

    states = [[st_ref[d, h] for h in range(GLA_HEADS)] for d in range(2)]
    for c in range(n_chunk):
        for d in range(2):
            cc = c if d == 0 else n_chunk - 1 - c
            rows = slice(cc * GLA_CHUNK, (cc + 1) * GLA_CHUNK)
            la = la_ref[d, rows, :]
            la_hi, la_lo = _split_bf16(la)
            bc = _dot(tri_bs[d], la_hi) + _dot(tri_bs[d], la_lo)
            bend = jnp.sum(la, axis=0, keepdims=True)
            qk = qk_refs[d][rows, :].astype(F32)
            q = qk[:, 0:nk]
            k = qk[:, nk:2 * nk]
            v = v_refs[d][rows, :]
            qd = (q * (GLA_DK ** -0.5) * jnp.exp(bc)).astype(BF16)
            ki = (k * jnp.exp(-bc)).astype(BF16)
            kd = (k * jnp.exp(bend - bc)).astype(BF16)
            dec = jnp.exp(bend)
            for h in range(GLA_HEADS):
                ks = slice(h * GLA_DK, (h + 1) * GLA_DK)
                vs = slice(h * GLA_DV, (h + 1) * GLA_DV)
                vh = v[:, vs]
                sc = jnp.where(tris[d], _dot_nt(qd[:, ks], ki[:, ks]), 0.0)
                st = states[d][h]
                o = _dot(sc.astype(BF16), vh) + _dot_nt(qd[:, ks], st.astype(BF16))
                o_refs[d][rows, vs] = o
                states[d][h] = st * dec[:, ks] + _dot_tn(vh, kd[:, ks])
    for d in range(2):
        for h in range(GLA_HEADS):
            st_ref[d, h] = states[d][h]

    @pl.when(i == pl.num_programs(1) - 1)
    def _():
        sfin_ref[...] = st_ref[...]


def _gla(p, wd_pad, bd, s0):
    b, s, _ = p.shape
    tc = min(512, s)
    n_step = s // tc
    fwd = lambda col: (lambda b_, i: (b_, i, col))
    bwd = lambda col: (lambda b_, i: (b_, n_step - 1 - i, col))
    lr_col = 9 * P_BLK // 128
    state_spec = pl.BlockSpec((None, 2, GLA_HEADS, GLA_DV, GLA_DK), lambda b_, i: (b_, 0, 0, 0, 0))
    return pl.pallas_call(
        _gla_kernel,
        grid=(b, n_step),
        in_specs=[pl.BlockSpec((None, tc, P_BLK), fwd(4)),
                  pl.BlockSpec((None, tc, P_BLK), fwd(5)),
                  pl.BlockSpec((None, tc, 128), fwd(lr_col)),
                  pl.BlockSpec((None, tc, P_BLK), bwd(4)),
                  pl.BlockSpec((None, tc, P_BLK), bwd(5)),
                  pl.BlockSpec((None, tc, 128), bwd(lr_col)),
                  pl.BlockSpec((2, 128, 256), lambda b_, i: (0, 0, 0)),
                  pl.BlockSpec((2, 1, 256), lambda b_, i: (0, 0, 0)),
                  state_spec],
        out_specs=[pl.BlockSpec((None, tc, 512), fwd(0)),
                   pl.BlockSpec((None, tc, 512), bwd(0)),
                   state_spec],
        out_shape=[jax.ShapeDtypeStruct((b, s, 512), F32),
                   jax.ShapeDtypeStruct((b, s, 512), F32),
                   jax.ShapeDtypeStruct((b, 2, GLA_HEADS, GLA_DV, GLA_DK), F32)],
        scratch_shapes=[pltpu.VMEM((2, tc, 256), F32), pltpu.VMEM((2, GLA_HEADS, GLA_DV, GLA_DK), F32)],
        compiler_params=_params("arbitrary", "arbitrary"),
        name="gla",
    )(p, p, p, p, p, p, wd_pad, bd, s0)


def _smlp_kernel(u_ref, v_ref, g_ref, ws_ref, bs_ref, o_ref):
    n_chunk = u_ref.shape[0] // SMLP_CHUNK
    g = g_ref[...]
    for c in range(n_chunk):
        rs = slice(c * SMLP_CHUNK, (c + 1) * SMLP_CHUNK)
        vb = _rms(jax.nn.gelu(v_ref[rs, :].astype(F32)), g).astype(BF16)
        u = jax.nn.gelu(u_ref[rs, :].astype(F32))
        for k in range(SMLP_GROUPS):
            cs = slice(k * SMLP_GROUP_DIM, (k + 1) * SMLP_GROUP_DIM)
            mixed = _dot(ws_ref[k], vb[:, cs]) + bs_ref[:, cs]
            o_ref[rs, cs] = (u[:, cs] * mixed).astype(BF16)


def _smlp(p, v_norm, ws, bs_full):
    b, s, _ = p.shape
    tc = min(512, s)
    return pl.pallas_call(
        _smlp_kernel,
        grid=(b, s // tc),
        in_specs=[pl.BlockSpec((None, tc, P_BLK), lambda b_, i: (b_, i, 7)),
                  pl.BlockSpec((None, tc, P_BLK), lambda b_, i: (b_, i, 8)),
                  pl.BlockSpec((1, 512), lambda b_, i: (0, 0)),
                  pl.BlockSpec((SMLP_GROUPS, SMLP_CHUNK, SMLP_CHUNK), lambda b_, i: (0, 0, 0)),
                  pl.BlockSpec((SMLP_CHUNK, 512), lambda b_, i: (0, 0))],
        out_specs=pl.BlockSpec((None, tc, 512), lambda b_, i: (b_, i, 0)),
        out_shape=jax.ShapeDtypeStruct((b, s, 512), BF16),
        compiler_params=_params("arbitrary", "arbitrary"),
        name="smlp",
    )(p, p, v_norm, ws, bs_full)


def _merge_kernel(h_ref, oa_ref, om_ref, gf_ref, gb_ref, gr_ref, os_ref, gn_ref,
                  wg_ref, bg_ref, wb_ref, y_ref, og_ref):
    @pl.when(pl.program_id(2) == 0)
    def _():
        o = gf_ref[...] + gb_ref[...]
        r = gr_ref[...].astype(F32)
        gate = r * jax.nn.sigmoid(r)
        for hh in range(GLA_HEADS):
            vs = slice(hh * GLA_DV, (hh + 1) * GLA_DV)
            og_ref[:, vs] = (_rms(o[:, vs], gn_ref[...]) * gate[:, vs]).astype(BF16)

    h = h_ref[...]
    branches = (oa_ref[...], om_ref[...], og_ref[...], os_ref[...])
    y = None
    for n, o in enumerate(branches):
        gate = jax.nn.sigmoid(_dot(h, wg_ref[n]) + bg_ref[n])
        t = gate * _dot(o, wb_ref[n])
        y = t if y is None else y + t
    y_ref[...] = y.astype(BF16)


def _merge(h, o_a, o_m, o_gf, o_gb, p, o_s, gla_norm, wg, bg, wb):
    b, s, d = h.shape
    tm = min(512, s)
    tn = 512
    row = lambda b_, i, j: (b_, i, 0)
    return pl.pallas_call(
        _merge_kernel,
        grid=(b, s // tm, d // tn),
        in_specs=[pl.BlockSpec((None, tm, d), row),
                  pl.BlockSpec((None, tm, 512), row),
                  pl.BlockSpec((None, tm, 512), row),
                  pl.BlockSpec((None, tm, 512), row),
                  pl.BlockSpec((None, tm, 512), row),
                  pl.BlockSpec((None, tm, P_BLK), lambda b_, i, j: (b_, i, 6)),
                  pl.BlockSpec((None, tm, 512), row),
                  pl.BlockSpec((1, GLA_DV), lambda b_, i, j: (0, 0)),
                  pl.BlockSpec((4, d, tn), lambda b_, i, j: (0, 0, j)),
                  pl.BlockSpec((4, 1, tn), lambda b_, i, j: (0, 0, j)),
                  pl.BlockSpec((4, 512, tn), lambda b_, i, j: (0, 0, j))],
        out_specs=pl.BlockSpec((None, tm, tn), lambda b_, i, j: (b_, i, j)),
        out_shape=jax.ShapeDtypeStruct((b, s, d), BF16),
        scratch_shapes=[pltpu.VMEM((tm, 512), BF16)],
        compiler_params=_params("arbitrary", "arbitrary", "arbitrary"),
        name="merge",
    )(h, o_a, o_m, o_gf, o_gb, p, o_s, gla_norm, wg, bg, wb)


def _outproj_kernel(y_ref, x_ref, mod_ref, g_ref, wo_ref, wr_ref, br_ref, xo_ref, h2_ref, lg_ref):
    w_hi, w_lo = _split_bf16(wr_ref[...])
    tm = y_ref.shape[0]
    n_part = 2 if tm % 16 == 0 else 1
    rows = [slice(n * tm // n_part, (n + 1) * tm // n_part) for n in range(n_part)]
    z_next = _dot(y_ref[rows[0], :], wo_ref[...])
    for n, rs in enumerate(rows):
        z = z_next
        if n + 1 < n_part:
            z_next = _dot(y_ref[rows[n + 1], :], wo_ref[...])
        xn = x_ref[rs, :] + mod_ref[2:3, :] * _rms(z, g_ref[1:2, :])
        xo_ref[rs, :] = xn
        h2 = _rms(xn, g_ref[2:3, :]) * (1.0 + mod_ref[4:5, :]) + mod_ref[3:4, :]
        half = h2.shape[1] // 2
        h2_ref[rs, :] = _pack_bf16_pair(h2[:, :half], h2[:, half:])
        h_hi, h_lo = _split_bf16(h2)
        lg_ref[rs, :] = _dot(h_hi, w_hi) + _dot(h_lo, w_hi) + _dot(h_hi, w_lo) + br_ref[...]


def _outproj(y, x, mod, norm_g, w_out, wr_pad, br_pad):
    b, s, d = x.shape
    tm = min(256, s)
    row = lambda b_, i: (b_, i, 0)
    return pl.pallas_call(
        _outproj_kernel,
        grid=(b, s // tm),
        in_specs=[pl.BlockSpec((None, tm, d), row),
                  pl.BlockSpec((None, tm, d), row),
                  pl.BlockSpec((None, 8, d), lambda b_, i: (b_, 0, 0)),
                  pl.BlockSpec((4, d), lambda b_, i: (0, 0)),
                  pl.BlockSpec((d, d), lambda b_, i: (0, 0)),
                  pl.BlockSpec((d, LANE), lambda b_, i: (0, 0)),
                  pl.BlockSpec((1, LANE), lambda b_, i: (0, 0))],
        out_specs=[pl.BlockSpec((None, tm, d), row),
                   pl.BlockSpec((None, tm, d // 2), row),
                   pl.BlockSpec((None, tm, LANE), row)],
        out_shape=[jax.ShapeDtypeStruct((b, s, d), F32),
                   jax.ShapeDtypeStruct((b, s, d // 2), jnp.uint32),
                   jax.ShapeDtypeStruct((b, s, LANE), F32)],
        compiler_params=_params("arbitrary", "arbitrary"),
        name="outproj",
    )(y, x, mod, norm_g, w_out, wr_pad, br_pad)


def _route_kernel(lg_ref, idx_ref, w_ref, rank_ref, cnt_ref, carry_ref):
    i = pl.program_id(0)

    @pl.when(i == 0)
    def _():
        carry_ref[...] = jnp.zeros(carry_ref.shape, F32)

    lg = lg_ref[...]
    tm = lg.shape[0]
    lane = lax.broadcasted_iota(I32, lg.shape, 1).astype(F32)
    vals, idxs = [], []
    onehot = jnp.zeros(lg.shape, F32)
    for _ in range(TOP_K):
        m = jnp.max(lg, axis=-1, keepdims=True)
        sel = jnp.min(jnp.where(lg == m, lane, float(LANE)), axis=-1, keepdims=True)
        hit = lane == sel
        vals.append(m)
        idxs.append(sel)
        onehot = jnp.where(hit, 1.0, onehot)
        lg = jnp.where(hit, NEG_BIG, lg)
    es = [jnp.exp(v - vals[0]) for v in vals]
    den = es[0] + es[1] + es[2] + es[3]
    r = lax.broadcasted_iota(I32, (tm, tm), 0)
    c = lax.broadcasted_iota(I32, (tm, tm), 1)
    lower = (c < r).astype(F32).astype(BF16)
    before = _dot(lower, onehot.astype(BF16)) + carry_ref[...]
    for k in range(TOP_K):
        idx_ref[:, k:k + 1] = idxs[k].astype(I32)
        w_ref[:, k:k + 1] = es[k] / den
        rank_ref[:, k:k + 1] = jnp.sum(jnp.where(lane == idxs[k], before, 0.0), axis=-1,
                                       keepdims=True).astype(I32)
    carry_ref[...] = carry_ref[...] + jnp.sum(onehot, axis=0, keepdims=True)
    cnt_ref[...] = carry_ref[...]


def _route(logits):
    t = logits.shape[0]
    tm = 256
    return pl.pallas_call(
        _route_kernel,
        grid=(t // tm,),
        in_specs=[pl.BlockSpec((tm, LANE), lambda i: (i, 0))],
        out_specs=[pl.BlockSpec((tm, TOP_K), lambda i: (i, 0)),
                   pl.BlockSpec((tm, TOP_K), lambda i: (i, 0)),
                   pl.BlockSpec((tm, TOP_K), lambda i: (i, 0)),
                   pl.BlockSpec((1, LANE), lambda i: (0, 0))],
        out_shape=[jax.ShapeDtypeStruct((t, TOP_K), I32),
                   jax.ShapeDtypeStruct((t, TOP_K), F32),
                   jax.ShapeDtypeStruct((t, TOP_K), I32),
                   jax.ShapeDtypeStruct((1, LANE), F32)],
        scratch_shapes=[pltpu.VMEM((1, LANE), F32)],
        compiler_params=_params("arbitrary"),
        name="route",
    )(logits)


def _dispatch_kernel(pad_ref, dest_hbm, src_ref, *rest, zero_pads):
    xs_out, dest_smem, zero_ref, idx_sem, sem = rest[-5:]
    i = pl.program_id(0)
    tm = src_ref.shape[0]
    cp = pltpu.make_async_copy(dest_hbm.at[i], dest_smem, idx_sem)
    cp.start()

    if zero_pads:
        @pl.when(i == 0)
        def _():
            zero_ref[...] = jnp.zeros(zero_ref.shape, zero_ref.dtype)

            def pad_copy(e, r):
                return pltpu.make_async_copy(zero_ref.at[pl.ds(0, 1)],
                                             xs_out.at[pl.ds(pad_ref[e] + r, 1)], sem)

            def start_pads(e, carry):
                def body(r, c):
                    pad_copy(e, r).start()
                    return c
                return lax.fori_loop(0, pad_ref[N_EXPERTS + e], body, carry)

            def wait_pads(e, carry):
                def body(r, c):
                    pad_copy(e, r).wait()
                    return c
                return lax.fori_loop(0, pad_ref[N_EXPERTS + e], body, carry)

            lax.fori_loop(0, N_EXPERTS, start_pads, 0)
            lax.fori_loop(0, N_EXPERTS, wait_pads, 0)

    cp.wait()

    def row_copy(t, k):
        return pltpu.make_async_copy(src_ref.at[pl.ds(t, 1)],
                                     xs_out.at[pl.ds(dest_smem[t * TOP_K + k], 1)], sem)

    def issue(t):
        for k in range(TOP_K):
            row_copy(t, k).start()

    def drain(t):
        for k in range(TOP_K):
            row_copy(0, k).wait()

    _for_each_row(tm, issue)
    _for_each_row(tm, drain)


def _dispatch(pads, dest, src, xs, n_rows):
    t, d = src.shape
    tm = min(1024, t)
    dest2 = dest.reshape(t // tm, tm * TOP_K)
    first = xs is None
    args = (pads, dest2, src) + (() if first else (xs,))
    return pl.pallas_call(
        functools.partial(_dispatch_kernel, zero_pads=first),
        grid=(t // tm,),
        in_specs=[pl.BlockSpec(memory_space=pltpu.SMEM),
                  pl.BlockSpec(memory_space=pl.ANY),
                  pl.BlockSpec((tm, d), lambda i: (i, 0))] + ([] if first else [pl.BlockSpec(memory_space=pl.ANY)]),
        out_specs=pl.BlockSpec(memory_space=pl.ANY),
        out_shape=jax.ShapeDtypeStruct((n_rows, d), src.dtype),
        scratch_shapes=[pltpu.SMEM((tm * TOP_K,), I32), pltpu.VMEM((8, d), src.dtype),
                        pltpu.SemaphoreType.DMA, pltpu.SemaphoreType.DMA],
        input_output_aliases={} if first else {3: 0},
        compiler_params=_params("arbitrary"),
        name="dispatch",
    )(*args)


def _expert_kernel(te_ref, nu_ref, x_ref, w1_ref, b1_ref, w2_ref, b2_ref, y_ref):
    @pl.when(pl.program_id(0) < nu_ref[0])
    def _():
        x_a, x_b = _unpack_bf16_pair(x_ref[...])
        half = x_a.shape[1]
        z = (_dot(x_a.astype(BF16), w1_ref[:half, :]) + _dot(x_b.astype(BF16), w1_ref[half:, :])
             + b1_ref[...])
        glu = jnp.minimum(z[:, :D_EXPERT], SWIGLU_LIMIT)
        lin = jnp.clip(z[:, D_EXPERT:], -SWIGLU_LIMIT, SWIGLU_LIMIT)
        act = glu * jax.nn.sigmoid(SWIGLU_ALPHA * glu) * (lin + 1.0)
        y = _dot(act.astype(BF16), w2_ref[...]) + b2_ref[...]
        y_ref[...] = _pack_bf16_pair(y[:, :half], y[:, half:])


def _experts(layer, tile_e, n_used, xs, w1, b1, w2, b2):
    n_rows, half = xs.shape
    d = 2 * half
    tr = EXPERT_TILE
    n_tiles = n_rows // tr
    row = lambda i, te, nu: (jnp.minimum(i, nu[0] - 1), 0)
    wsel = lambda i, te, nu: (layer, te[i], 0, 0)
    return pl.pallas_call(
        _expert_kernel,
        grid_spec=pltpu.PrefetchScalarGridSpec(
            num_scalar_prefetch=2,
            grid=(n_tiles,),
            in_specs=[pl.BlockSpec((tr, half), row),
                      pl.BlockSpec((None, None, d, 2 * D_EXPERT), wsel),
                      pl.BlockSpec((None, None, 1, 2 * D_EXPERT), wsel),
                      pl.BlockSpec((None, None, D_EXPERT, d), wsel),
                      pl.BlockSpec((None, None, 1, d), wsel)],
            out_specs=pl.BlockSpec((tr, half), row)),
        out_shape=jax.ShapeDtypeStruct((n_rows, half), jnp.uint32),
        compiler_params=_params("arbitrary"),
        name="experts",
    )(tile_e, n_used, xs, w1, b1, w2, b2)


def _combine_kernel(dest_hbm, w_ref, x_ref, mod_ref, g_ref, ys_hbm, o_ref,
                    dest0, dest1, buf0, buf1, idx_sem, sem0, sem1):
    dest_smem, buf, sem = (dest0, dest1), (buf0, buf1), (sem0, sem1)
    tm = x_ref.shape[0]
    n_step = pl.num_programs(0) * pl.num_programs(1)
    i = pl.program_id(0) * pl.num_programs(1) + pl.program_id(1)

    def row_copy(sl, t, k):
        return pltpu.make_async_copy(ys_hbm.at[pl.ds(dest_smem[sl][t * TOP_K + k], 1)],
                                     buf[sl].at[k, pl.ds(t, 1)], sem[sl])

    def start_gather(step, sl):
        cp = pltpu.make_async_copy(dest_hbm.at[step], dest_smem[sl], idx_sem)
        cp.start()
        cp.wait()

        def issue(t):
            for k in range(TOP_K):
                row_copy(sl, t, k).start()

        _for_each_row(tm, issue)

    @pl.when(i == 0)
    def _():
        start_gather(0, 0)

    def step(slot):
        @pl.when(i + 1 < n_step)
        def _():
            start_gather(i + 1, 1 - slot)

        def drain(t):
            for k in range(TOP_K):
                row_copy(slot, 0, k).wait()

        _for_each_row(tm, drain)
        w = w_ref[...]
        f_a, f_b = None, None
        for k in range(TOP_K):
            y_a, y_b = _unpack_bf16_pair(buf[slot][k])
            f_a = w[:, k:k + 1] * y_a if f_a is None else f_a + w[:, k:k + 1] * y_a
            f_b = w[:, k:k + 1] * y_b if f_b is None else f_b + w[:, k:k + 1] * y_b
        half = f_a.shape[1]
        ms = (jnp.sum(f_a * f_a, axis=-1, keepdims=True)
              + jnp.sum(f_b * f_b, axis=-1, keepdims=True)) * (1.0 / (2 * half))
        inv = lax.rsqrt(ms + EPS)
        o_ref[:, :half] = x_ref[:, :half] + mod_ref[5:6, :half] * (f_a * inv * g_ref[3:4, :half])
        o_ref[:, half:] = x_ref[:, half:] + mod_ref[5:6, half:] * (f_b * inv * g_ref[3:4, half:])

    for slot in range(2):
        pl.when(lax.rem(i, 2) == slot)(functools.partial(step, slot))


def _combine(dest, w, x, mod, norm_g, ys):
    b, s, d = x.shape
    tm = min(256, s)
    ns = s // tm
    dest2 = dest.reshape(b * ns, tm * TOP_K)
    return pl.pallas_call(
        _combine_kernel,
        grid=(b, ns),
        in_specs=[pl.BlockSpec(memory_space=pl.ANY),
                  pl.BlockSpec((None, tm, TOP_K), lambda b_, i: (b_, i, 0)),
                  pl.BlockSpec((None, tm, d), lambda b_, i: (b_, i, 0)),
                  pl.BlockSpec((None, 8, d), lambda b_, i: (b_, 0, 0)),
                  pl.BlockSpec((4, d), lambda b_, i: (0, 0)),
                  pl.BlockSpec(memory_space=pl.ANY)],
        out_specs=pl.BlockSpec((None, tm, d), lambda b_, i: (b_, i, 0)),
        out_shape=jax.ShapeDtypeStruct((b, s, d), F32),
        scratch_shapes=[pltpu.SMEM((tm * TOP_K,), I32), pltpu.SMEM((tm * TOP_K,), I32),
                        pltpu.VMEM((TOP_K, tm, d // 2), jnp.uint32),
                        pltpu.VMEM((TOP_K, tm, d // 2), jnp.uint32),
                        pltpu.SemaphoreType.DMA, pltpu.SemaphoreType.DMA, pltpu.SemaphoreType.DMA],
        compiler_params=_params("arbitrary", "arbitrary"),
        name="combine",
    )(dest2, w.reshape(b, s, TOP_K), x, mod, norm_g, ys)


def _rot_half_cols(w, head_dim):
    d_in, n = w.shape
    w3 = w.reshape(d_in, n // head_dim, head_dim)
    half = head_dim // 2
    return jnp.concatenate([-w3[..., half:], w3[..., :half]], axis=-1).reshape(d_in, n)


def _extended_in_weights(w_in):
    sizes = (512, 128, 128, MLA_Q_RANK, MLA_KV_RANK, MLA_ROPE_DIM, 256, 256, 512,
             2 * GLA_DECAY_RANK, 512, 512, 512)
    cols, off = [], 0
    for n in sizes:
        cols.append(w_in[:, off:off + n])
        off += n
    (a_q, a_k, a_v, m_cq, m_ckv, m_kr, g_q, g_k, g_v, g_lr, g_r, s_u, s_v) = cols
    pad = jnp.zeros((w_in.shape[0], P_BLK - 2 * GLA_DECAY_RANK), w_in.dtype)
    ext = jnp.concatenate(
        [a_q, _rot_half_cols(a_q, SWA_HEAD_DIM),
         a_k, _rot_half_cols(a_k, SWA_HEAD_DIM), a_v, m_kr, _rot_half_cols(m_kr, MLA_ROPE_DIM),
         m_cq, m_ckv, g_q, g_k, g_v, g_r, s_u, s_v, g_lr, pad], axis=1)
    return ext.astype(BF16)


def _rope_tables(n_tok):
    n_rows = n_tok // GRID_W
    rows = jnp.repeat(jnp.arange(n_rows, dtype=F32), GRID_W)
    cols = jnp.tile(jnp.arange(GRID_W, dtype=F32), n_rows)
    n_freq = SWA_HEAD_DIM // 4
    inv_freq = ROPE_BASE ** (-jnp.arange(n_freq, dtype=F32) / n_freq)
    ang = jnp.concatenate([rows[:, None] * inv_freq, cols[:, None] * inv_freq], axis=-1)
    cos = jnp.concatenate([jnp.cos(ang), jnp.cos(ang)], axis=-1)
    sin = jnp.concatenate([jnp.sin(ang), jnp.sin(ang)], axis=-1)
    return jnp.tile(cos, (1, 8)), jnp.tile(sin, (1, 8))


def _moe(layer, h2_parts, logits_parts, w1, b1, w2, b2):
    logits = jnp.concatenate(logits_parts, axis=0)
    t_all = logits.shape[0]
    top_idx, top_w, rank, counts = _route(logits)
    counts = counts[0, :N_EXPERTS].astype(I32)
    tr = EXPERT_TILE
    padded = (counts + tr - 1) // tr * tr
    p_end = jnp.cumsum(padded)
    p_start = p_end - padded
    dest = p_start[top_idx] + rank
    n_tiles = -(-t_all * TOP_K // tr) + N_EXPERTS
    tile_start = jnp.arange(n_tiles, dtype=I32) * tr
    tile_e = jnp.minimum(jnp.sum((p_end[None, :] <= tile_start[:, None]).astype(I32), axis=1),
                         N_EXPERTS - 1)
    n_used = (p_end[-1:] // tr).astype(I32)
    pads = jnp.concatenate([p_start + counts, padded - counts]).astype(I32)
    xs = None
    off = 0
    for part in h2_parts:
        n = part.shape[0]
        xs = _dispatch(pads, dest[off:off + n], part, xs, n_tiles * tr)
        off += n
    ys = _experts(layer, tile_e, n_used, xs, w1, b1, w2, b2)
    return dest, top_w, ys


def kernel(x, c, ctx, c_ctx, ada_w, ada_b, norm_g, w_in, attn_sink, mla_q_norm, mla_w_uq, mla_kv_norm, mla_w_ukv, gla_w_decay, gla_b_decay, gla_out_norm, smlp_v_norm, smlp_w_spatial, smlp_b_spatial, w_branch, w_gate, b_gate, w_out, router_w, router_b, expert_w1, expert_b1, expert_w2, expert_b2):
    bsz, seq, d = x.shape
    n_ctx = ctx.shape[1]
    depth = ada_w.shape[0]

    cs = jnp.zeros((8, d), F32).at[:bsz].set(c).at[bsz].set(c_ctx)
    mod_all = _ada(cs, ada_w, ada_b).reshape(depth, 8, N_MOD, d)
    mod_all = jnp.pad(mod_all, ((0, 0), (0, 0), (0, 8 - N_MOD), (0, 0)))

    cos_t, sin_t = _rope_tables(seq)
    cos_c = jnp.ones((bsz * n_ctx, 512), F32)
    sin_c = jnp.zeros((bsz * n_ctx, 512), F32)
    ctx = ctx.reshape(1, bsz * n_ctx, d)

    w1 = expert_w1.astype(BF16)
    b1 = expert_b1.reshape(depth, N_EXPERTS, 1, 2 * D_EXPERT)
    w2 = expert_w2.astype(BF16)
    b2 = expert_b2.reshape(depth, N_EXPERTS, 1, d)

    for l in range(depth):
        last = l == depth - 1
        mod = mod_all[l, :bsz]
        mod_c = mod_all[l, bsz:bsz + 1]
        g_l = norm_g[l]
        w_ext = _extended_in_weights(w_in[l])
        uq = mla_w_uq[l].reshape(MLA_Q_RANK, MLA_HEADS, MLA_NOPE_DIM + MLA_ROPE_DIM)
        uq_rope = uq[:, :, MLA_NOPE_DIM:].reshape(MLA_Q_RANK, MLA_HEADS * MLA_ROPE_DIM)
        wuq_ext = jnp.concatenate(
            [uq[:, :, :MLA_NOPE_DIM].reshape(MLA_Q_RANK, MLA_HEADS * MLA_NOPE_DIM), uq_rope,
             _rot_half_cols(uq_rope, MLA_ROPE_DIM)], axis=1).astype(BF16)
        ukv = mla_w_ukv[l].reshape(MLA_KV_RANK, MLA_HEADS, MLA_NOPE_DIM + MLA_V_DIM)
        wkn = jnp.transpose(ukv[:, :, :MLA_NOPE_DIM], (1, 0, 2)).astype(BF16)
        wv = jnp.transpose(ukv[:, :, MLA_NOPE_DIM:], (1, 0, 2)).astype(BF16)
        qn_g = mla_q_norm[l].reshape(1, MLA_Q_RANK)
        kv_g = mla_kv_norm[l].reshape(1, MLA_KV_RANK)
        wd_pad = jnp.zeros((2, 128, 256), F32)
        for dd in range(2):
            wd_pad = wd_pad.at[dd, dd * GLA_DECAY_RANK:(dd + 1) * GLA_DECAY_RANK].set(gla_w_decay[l, dd])
        wd_pad = wd_pad.astype(BF16)
        bd = gla_b_decay[l].reshape(2, 1, 256)
        gla_norm = gla_out_norm[l].reshape(1, GLA_DV)
        v_norm = smlp_v_norm[l].reshape(1, 512)
        ws = smlp_w_spatial[l].astype(BF16)
        bs_full = jnp.repeat(smlp_b_spatial[l].T, SMLP_GROUP_DIM, axis=1)
        wg = w_gate[l].astype(BF16)
        bg = b_gate[l].reshape(4, 1, d)
        wb = w_branch[l].astype(BF16)
        wo = w_out[l].astype(BF16)
        wr_pad = jnp.pad(router_w[l], ((0, 0), (0, LANE - N_EXPERTS)))
        br_pad = jnp.pad(router_b[l].reshape(1, N_EXPERTS), ((0, 0), (0, LANE - N_EXPERTS)),
                         constant_values=NEG_BIG)
        sink = attn_sink[l]

        h, p = _inproj(x, mod, g_l, w_ext)
        hc, pc1 = _inproj(ctx, mod_c, g_l, w_ext)
        pc = pc1.reshape(bsz, n_ctx, P_WIDTH)
        ka, kcat, q_m = _prep(p, cos_t, sin_t, qn_g, kv_g, wuq_ext, wkn)
        _, kcat_c, q_mc = _prep(pc1, cos_c, sin_c, qn_g, kv_g, wuq_ext, wkn)
        kcat_c = kcat_c.reshape(bsz, n_ctx, MLA_QK_PAD)

        o_a = _swa(sink, p, ka, cos_t, sin_t, pc)
        o_m = _mla(q_m, jnp.concatenate([kcat, kcat_c], axis=1), wv)
        s0 = jnp.zeros((bsz, 2, GLA_HEADS, GLA_DV, GLA_DK), F32)
        ogf_c, ogb_c, s_ctx = _gla(pc, wd_pad, bd, s0)
        ogf, ogb, _ = _gla(p, wd_pad, bd, s_ctx)
        o_s = _smlp(p, v_norm, ws, bs_full)
        y = _merge(h, o_a, o_m, ogf, ogb, p, o_s, gla_norm, wg, bg, wb)
        x, h2, lg = _outproj(y, x, mod, g_l, wo, wr_pad, br_pad)
        h2_parts = [h2.reshape(bsz * seq, d // 2)]
        lg_parts = [lg.reshape(bsz * seq, LANE)]

        if not last:
            oc_a = _swa_ctx(sink, pc)
            q_mc = q_mc.reshape(MLA_HEADS, bsz, n_ctx, MLA_QK_PAD).transpose(1, 0, 2, 3)
            oc_m = _mla(q_mc, kcat_c, wv)
            oc_s = _smlp(pc, v_norm, ws, bs_full)
            flat = lambda a: a.reshape(1, bsz * n_ctx, a.shape[-1])
            yc = _merge(hc, flat(oc_a), flat(oc_m), flat(ogf_c), flat(ogb_c), pc1,
                        flat(oc_s), gla_norm, wg, bg, wb)
            ctx, h2c, lgc = _outproj(yc, ctx, mod_c, g_l, wo, wr_pad, br_pad)
            h2_parts.append(h2c.reshape(bsz * n_ctx, d // 2))
            lg_parts.append(lgc.reshape(bsz * n_ctx, LANE))

        dest, top_w, ys = _moe(l, h2_parts, lg_parts, w1, b1, w2, b2)
        n_lat = bsz * seq
        x = _combine(dest[:n_lat], top_w[:n_lat], x, mod, g_l, ys)
        if not last:
            ctx = _combine(dest[n_lat:], top_w[n_lat:], ctx, mod_c, g_l, ys)
    return x
```

```python
import functools

import jax
import jax.numpy as jnp
from jax import lax
from jax.experimental import pallas as pl
from jax.experimental.pallas import tpu as pltpu

F32 = jnp.float32
BF16 = jnp.bfloat16
I32 = jnp.int32

EPS = 1e-6
ROPE_BASE = 10000.0
GRID_W = 64
N_MOD = 6

SWA_HEADS = 8
SWA_KV_HEADS = 2
SWA_GROUP = SWA_HEADS // SWA_KV_HEADS
SWA_HEAD_DIM = 64
SWA_BLOCK = 128

MLA_HEADS = 4
MLA_Q_RANK = 384
MLA_KV_RANK = 128
MLA_NOPE_DIM = 128
MLA_ROPE_DIM = 64
MLA_V_DIM = 128
MLA_QK_PAD = 256
MLA_QUERY_CHUNK = 512
MLA_QUERY_TILE = 2048
LOG2_E = 1.4426950408889634

GLA_HEADS = 4
GLA_DK = 64
GLA_DV = 128
GLA_DECAY_RANK = 16
GLA_TAU = 16.0
GLA_CHUNK = 64

SMLP_GROUPS = 4
SMLP_CHUNK = 128
SMLP_GROUP_DIM = 128

N_EXPERTS = 32
TOP_K = 4
D_EXPERT = 1024
SWIGLU_LIMIT = 7.0
SWIGLU_ALPHA = 1.702

LANE = 128
NEG_BIG = -1e30
VMEM_LIMIT = 56 * 1024 * 1024

P_BLK = 512
P_NBLK = 10
P_WIDTH = P_BLK * P_NBLK
EXPERT_TILE = 256
ROW_GROUP = 8


def _dot(a, b):
    return jnp.dot(a, b, preferred_element_type=F32)


def _dot_nt(a, b):
    return lax.dot_general(a, b, (((1,), (1,)), ((), ())), preferred_element_type=F32)


def _dot_tn(a, b):
    return lax.dot_general(a, b, (((0,), (0,)), ((), ())), preferred_element_type=F32)


def _split_bf16(a):
    hi = a.astype(BF16)
    lo = (a - hi.astype(F32)).astype(BF16)
    return hi, lo


def _rms(x, g):
    return x * lax.rsqrt(jnp.mean(x * x, axis=-1, keepdims=True) + EPS) * g


def _pack_bf16_pair(a, b):
    hi = pltpu.bitcast(a.astype(BF16).astype(F32), jnp.uint32)
    lo = pltpu.bitcast(b.astype(BF16).astype(F32), jnp.uint32)
    return hi | (lo >> 16)


def _unpack_bf16_pair(w):
    a = pltpu.bitcast(w & jnp.uint32(0xFFFF0000), F32)
    b = pltpu.bitcast(w << 16, F32)
    return a, b


def _for_each_row(n_rows, fn):
    def group(u, carry):
        base = pl.multiple_of(u * ROW_GROUP, ROW_GROUP)
        for j in range(ROW_GROUP):
            fn(base + j)
        return carry

    lax.fori_loop(0, n_rows // ROW_GROUP, group, 0)


def _params(*sem):
    return pltpu.CompilerParams(dimension_semantics=sem, vmem_limit_bytes=VMEM_LIMIT)


def _ada_kernel(c_ref, w_ref, b_ref, o_ref):
    c = c_ref[...]
    a_hi, a_lo = _split_bf16(c * jax.nn.sigmoid(c))
    w_hi, w_lo = _split_bf16(w_ref[...])
    o_ref[...] = _dot(a_hi, w_hi) + _dot(a_lo, w_hi) + _dot(a_hi, w_lo) + b_ref[...]


def _ada(cs, ada_w, ada_b):
    n_layer, d, n = ada_w.shape
    tn = 1024
    return pl.pallas_call(
        _ada_kernel,
        grid=(n_layer, n // tn),
        in_specs=[pl.BlockSpec((8, d), lambda l, j: (0, 0)),
                  pl.BlockSpec((None, d, tn), lambda l, j: (l, 0, j)),
                  pl.BlockSpec((None, 1, tn), lambda l, j: (l, 0, j))],
        out_specs=pl.BlockSpec((None, 8, tn), lambda l, j: (l, 0, j)),
        out_shape=jax.ShapeDtypeStruct((n_layer, 8, n), F32),
        compiler_params=_params("arbitrary", "arbitrary"),
        name="ada",
    )(cs, ada_w, ada_b.reshape(n_layer, 1, n))


def _inproj_kernel(x_ref, mod_ref, g_ref, w_ref, h_ref, p_ref):
    y = _rms(x_ref[...], g_ref[0:1, :])
    hb = (y * (1.0 + mod_ref[1:2, :]) + mod_ref[0:1, :]).astype(BF16)
    h_ref[...] = hb
    p_ref[...] = _dot(hb, w_ref[...]).astype(BF16)


def _inproj(x, mod, norm_g, w_ext):
    b, s, d = x.shape
    tm = min(256, s)
    return pl.pallas_call(
        _inproj_kernel,
        grid=(b, s // tm),
        in_specs=[pl.BlockSpec((None, tm, d), lambda b_, i: (b_, i, 0)),
                  pl.BlockSpec((None, 8, d), lambda b_, i: (b_, 0, 0)),
                  pl.BlockSpec((4, d), lambda b_, i: (0, 0)),
                  pl.BlockSpec((d, P_WIDTH), lambda b_, i: (0, 0), pipeline_mode=pl.Buffered(1))],
        out_specs=[pl.BlockSpec((None, tm, d), lambda b_, i: (b_, i, 0)),
                   pl.BlockSpec((None, tm, P_WIDTH), lambda b_, i: (b_, i, 0))],
        out_shape=[jax.ShapeDtypeStruct((b, s, d), BF16),
                   jax.ShapeDtypeStruct((b, s, P_WIDTH), BF16)],
        compiler_params=_params("arbitrary", "arbitrary"),
        name="inproj",
    )(x, mod, norm_g, w_ext)


def _prep_kernel(kblk_ref, cblk_ref, cos_ref, sin_ref, qn_g_ref, kv_g_ref, wuq_ref, wkn_ref,
                 ka_ref, kcat_ref, q_ref):
    cos = cos_ref[...]
    sin = sin_ref[...]
    kb = kblk_ref[...].astype(F32)
    ka_ref[...] = (kb[:, 0:128] * cos[:, 0:128] + kb[:, 128:256] * sin[:, 0:128]).astype(BF16)
    kr = kb[:, 384:448] * cos[:, 0:64] + kb[:, 448:512] * sin[:, 0:64]
    cb = cblk_ref[...].astype(F32)
    ckvn = _rms(cb[:, MLA_Q_RANK:], kv_g_ref[...])
    tm = kb.shape[0]
    kcat_ref[:, 0:128] = ckvn.astype(BF16)
    kcat_ref[:, 128:192] = kr.astype(BF16)
    kcat_ref[:, 192:256] = jnp.zeros((tm, 64), BF16)
    cqn = _rms(cb[:, :MLA_Q_RANK], qn_g_ref[...]).astype(BF16)
    qa = _dot(cqn, wuq_ref[...])
    qr = qa[:, 512:768] * cos[:, 0:256] + qa[:, 768:1024] * sin[:, 0:256]
    scale = (MLA_NOPE_DIM + MLA_ROPE_DIM) ** -0.5 * LOG2_E
    for h in range(MLA_HEADS):
        qn = qa[:, h * 128:(h + 1) * 128].astype(BF16)
        q_ref[h, :, 0:128] = (_dot_nt(qn, wkn_ref[h]) * scale).astype(BF16)
        q_ref[h, :, 128:192] = (qr[:, h * 64:(h + 1) * 64] * scale).astype(BF16)
        q_ref[h, :, 192:256] = jnp.zeros((tm, 64), BF16)


def _prep(p, cos_t, sin_t, qn_g, kv_g, wuq_ext, wkn):
    b, s, _ = p.shape
    tm = min(512, s)
    return pl.pallas_call(
        _prep_kernel,
        grid=(b, s // tm),
        in_specs=[pl.BlockSpec((None, tm, P_BLK), lambda b_, i: (b_, i, 2)),
                  pl.BlockSpec((None, tm, P_BLK), lambda b_, i: (b_, i, 3)),
                  pl.BlockSpec((tm, 512), lambda b_, i: (i, 0)),
                  pl.BlockSpec((tm, 512), lambda b_, i: (i, 0)),
                  pl.BlockSpec((1, MLA_Q_RANK), lambda b_, i: (0, 0)),
                  pl.BlockSpec((1, MLA_KV_RANK), lambda b_, i: (0, 0)),
                  pl.BlockSpec((MLA_Q_RANK, 1024), lambda b_, i: (0, 0)),
                  pl.BlockSpec((MLA_HEADS, MLA_KV_RANK, MLA_NOPE_DIM), lambda b_, i: (0, 0, 0))],
        out_specs=[pl.BlockSpec((None, tm, 128), lambda b_, i: (b_, i, 0)),
                   pl.BlockSpec((None, tm, MLA_QK_PAD), lambda b_, i: (b_, i, 0)),
                   pl.BlockSpec((None, MLA_HEADS, tm, MLA_QK_PAD), lambda b_, i: (b_, 0, i, 0))],
        out_shape=[jax.ShapeDtypeStruct((b, s, 128), BF16),
                   jax.ShapeDtypeStruct((b, s, MLA_QK_PAD), BF16),
                   jax.ShapeDtypeStruct((b, MLA_HEADS, s, MLA_QK_PAD), BF16)],
        compiler_params=_params("arbitrary", "arbitrary"),
        name="prep",
    )(p, p, cos_t, sin_t, qn_g, kv_g, wuq_ext, wkn)


def _swa_group(q, sink_ref, g, k_all, v_all, mask, o_ref, rows):
    heads = range(g * SWA_GROUP, (g + 1) * SWA_GROUP)
    n_q = q.shape[0]
    qs = jnp.concatenate([q[:, h * SWA_HEAD_DIM:(h + 1) * SWA_HEAD_DIM] for h in heads], axis=0)
    sink = jnp.concatenate([jnp.full((n_q, 1), sink_ref[h], F32) for h in heads], axis=0)
    s = _dot_nt(qs, k_all)
    if mask is not None:
        s = s + jnp.concatenate([mask] * SWA_GROUP, axis=0)
    m = jnp.maximum(jnp.max(s, axis=-1, keepdims=True), sink)
    e = jnp.exp(s - m)
    den = jnp.sum(e, axis=-1, keepdims=True) + jnp.exp(sink - m)
    o = _dot(e.astype(BF16), v_all) / den
    for n, h in enumerate(heads):
        o_ref[rows, h * SWA_HEAD_DIM:(h + 1) * SWA_HEAD_DIM] = o[n * n_q:(n + 1) * n_q, :].astype(BF16)


def _swa_kernel(sink_ref, q_ref, qr_ref, cos_ref, sin_ref, kp_ref, kc_ref, kn_ref,
                vp_ref, vc_ref, vn_ref, kx_ref, vx_ref, o_ref):
    i = pl.program_id(1)
    n_pair = pl.num_programs(1)
    blk = SWA_BLOCK
    scale = SWA_HEAD_DIM ** -0.5
    q = ((q_ref[...].astype(F32) * cos_ref[...] + qr_ref[...].astype(F32) * sin_ref[...])
         * scale).astype(BF16)
    n_ctx = kx_ref.shape[0]
    n_keys = 3 * blk + n_ctx
    row = lax.broadcasted_iota(I32, (blk, n_keys), 0)
    col = lax.broadcasted_iota(I32, (blk, n_keys), 1)
    dist = col - row
    in_next = (col >= 2 * blk) & (col < 3 * blk)
    for sub in range(2):
        rows = slice(sub * blk, (sub + 1) * blk)
        if sub == 0:
            lo = jnp.where(i > 0, 0, n_keys)
            hi = 2 * blk
        else:
            lo = 0
            hi = jnp.where(i < n_pair - 1, 2 * blk, -n_keys)
        bias_prev = jnp.where(dist >= lo, 0.0, NEG_BIG)
        bias_next = jnp.where(dist <= hi, 0.0, NEG_BIG)
        mask = jnp.where(col < blk, bias_prev, jnp.where(in_next, bias_next, 0.0))
        for g in range(SWA_KV_HEADS):
            gs = slice(g * SWA_HEAD_DIM, (g + 1) * SWA_HEAD_DIM)
            own = [(kc_ref[0:blk, gs], vc_ref[0:blk, gs]),
                   (kc_ref[blk:2 * blk, gs], vc_ref[blk:2 * blk, gs])]
            if sub == 0:
                win = [(kp_ref[:, gs], vp_ref[:, gs])] + own
            else:
                win = own + [(kn_ref[:, gs], vn_ref[:, gs])]
            win = win + [(kx_ref[:, gs], vx_ref[:, gs])]
            k_all = jnp.concatenate([k for k, _ in win], axis=0)
            v_all = jnp.concatenate([v for _, v in win], axis=0)
            _swa_group(q[rows, :], sink_ref, g, k_all, v_all, mask, o_ref, rows)


def _swa_ctx_kernel(sink_ref, q_ref, kx_ref, vx_ref, o_ref):
    scale = SWA_HEAD_DIM ** -0.5
    q = (q_ref[...].astype(F32) * scale).astype(BF16)
    for g in range(SWA_KV_HEADS):
        gs = slice(g * SWA_HEAD_DIM, (g + 1) * SWA_HEAD_DIM)
        _swa_group(q, sink_ref, g, kx_ref[:, gs], vx_ref[:, gs], None, o_ref, slice(None))


def _swa(sink, p, ka, cos_t, sin_t, pc):
    b, s, _ = p.shape
    n_ctx = pc.shape[1]
    blk = SWA_BLOCK
    nb = s // blk
    pair = 2 * blk
    vcol = (2 * P_BLK + 256) // 128
    prev = lambda col: (lambda b_, i: (b_, jnp.maximum(2 * i - 1, 0), col))
    cur = lambda col: (lambda b_, i: (b_, i, col))
    nxt = lambda col: (lambda b_, i: (b_, jnp.minimum(2 * i + 2, nb - 1), col))
    return pl.pallas_call(
        _swa_kernel,
        grid=(b, nb // 2),
        in_specs=[pl.BlockSpec(memory_space=pltpu.SMEM),
                  pl.BlockSpec((None, pair, P_BLK), lambda b_, i: (b_, i, 0)),
                  pl.BlockSpec((None, pair, P_BLK), lambda b_, i: (b_, i, 1)),
                  pl.BlockSpec((pair, 512), lambda b_, i: (i, 0)),
                  pl.BlockSpec((pair, 512), lambda b_, i: (i, 0)),
                  pl.BlockSpec((None, blk, 128), prev(0)),
                  pl.BlockSpec((None, pair, 128), cur(0)),
                  pl.BlockSpec((None, blk, 128), nxt(0)),
                  pl.BlockSpec((None, blk, 128), prev(vcol)),
                  pl.BlockSpec((None, pair, 128), cur(vcol)),
                  pl.BlockSpec((None, blk, 128), nxt(vcol)),
                  pl.BlockSpec((None, n_ctx, 128), lambda b_, i: (b_, 0, 2 * P_BLK // 128)),
                  pl.BlockSpec((None, n_ctx, 128), lambda b_, i: (b_, 0, vcol))],
        out_specs=pl.BlockSpec((None, pair, 512), lambda b_, i: (b_, i, 0)),
        out_shape=jax.ShapeDtypeStruct((b, s, 512), BF16),
        compiler_params=_params("arbitrary", "arbitrary"),
        name="swa",
    )(sink, p, p, cos_t, sin_t, ka, ka, ka, p, p, p, pc, pc)


def _swa_ctx(sink, pc):
    b, n_ctx, _ = pc.shape
    blk = SWA_BLOCK
    vcol = (2 * P_BLK + 256) // 128
    return pl.pallas_call(
        _swa_ctx_kernel,
        grid=(b, n_ctx // blk),
        in_specs=[pl.BlockSpec(memory_space=pltpu.SMEM),
                  pl.BlockSpec((None, blk, P_BLK), lambda b_, i: (b_, i, 0)),
                  pl.BlockSpec((None, n_ctx, 128), lambda b_, i: (b_, 0, 2 * P_BLK // 128)),
                  pl.BlockSpec((None, n_ctx, 128), lambda b_, i: (b_, 0, vcol))],
        out_specs=pl.BlockSpec((None, blk, 512), lambda b_, i: (b_, i, 0)),
        out_shape=jax.ShapeDtypeStruct((b, n_ctx, 512), BF16),
        compiler_params=_params("arbitrary", "arbitrary"),
        name="swa_ctx",
    )(sink, pc, pc, pc)


def _mla_kernel(q_ref, kv_ref, vt_ref, wv_ref, o_ref, m_ref, l_ref, acc_ref):
    j = pl.program_id(2)

    @pl.when(j == 0)
    def _():
        m_ref[...] = jnp.full(m_ref.shape, NEG_BIG, F32)
        l_ref[...] = jnp.zeros(l_ref.shape, F32)
        acc_ref[...] = jnp.zeros(acc_ref.shape, F32)

    nh, tq, dq = q_ref.shape
    kv = kv_ref[...]
    vt = vt_ref[...]
    qc = min(MLA_QUERY_CHUNK, tq)
    per_head = tq // qc
    n_chunk = nh * per_head

    def scores(c):
        h, r = divmod(c, per_head)
        return _dot_nt(kv, q_ref[h, r * qc:(r + 1) * qc, :])

    def accumulate(pending):
        cols, alpha, p = pending
        acc_ref[:, cols] = alpha * acc_ref[:, cols] + _dot(vt, p)

    s_next = scores(0)
    pending = None
    for c in range(n_chunk):
        s = s_next
        if c + 1 < n_chunk:
            s_next = scores(c + 1)
        if pending is not None:
            accumulate(pending)
        cols = slice(c * qc, (c + 1) * qc)
        m_prev = m_ref[:, cols]
        m_new = jnp.maximum(m_prev, jnp.max(s, axis=0, keepdims=True))
        alpha = jnp.exp2(m_prev - m_new)
        p = jnp.exp2(s - m_new)
        l_ref[:, cols] = alpha * l_ref[:, cols] + jnp.sum(p, axis=0, keepdims=True)
        m_ref[:, cols] = m_new
        pending = (cols, alpha, p.astype(BF16))
    accumulate(pending)

    @pl.when(j == pl.num_programs(2) - 1)
    def _():
        o = (acc_ref[...] / l_ref[...]).astype(BF16)
        for h in range(nh):
            o_ref[:, h * MLA_V_DIM:(h + 1) * MLA_V_DIM] = _dot_tn(
                o[:, h * tq:(h + 1) * tq], wv_ref[h]).astype(BF16)


def _mla(q, kcat, wv):
    b, nh, sq, dq = q.shape
    sk = kcat.shape[1]
    tq = min(MLA_QUERY_TILE, sq)
    tk = next(t for t in (1280, 640, 512, 384, 256, 128) if sk % t == 0)
    vt = jnp.swapaxes(kcat[:, :, :MLA_KV_RANK], 1, 2)
    return pl.pallas_call(
        _mla_kernel,
        grid=(b, sq // tq, sk // tk),
        in_specs=[pl.BlockSpec((None, nh, tq, dq), lambda b_, i, j: (b_, 0, i, 0)),
                  pl.BlockSpec((None, tk, dq), lambda b_, i, j: (b_, j, 0)),
                  pl.BlockSpec((None, MLA_KV_RANK, tk), lambda b_, i, j: (b_, 0, j)),
                  pl.BlockSpec((nh, MLA_KV_RANK, MLA_V_DIM), lambda b_, i, j: (0, 0, 0))],
        out_specs=pl.BlockSpec((None, tq, nh * MLA_V_DIM), lambda b_, i, j: (b_, i, 0)),
        out_shape=jax.ShapeDtypeStruct((b, sq, nh * MLA_V_DIM), BF16),
        scratch_shapes=[pltpu.VMEM((1, nh * tq), F32), pltpu.VMEM((1, nh * tq), F32),
                        pltpu.VMEM((MLA_KV_RANK, nh * tq), F32)],
        compiler_params=_params("arbitrary", "arbitrary", "arbitrary"),
        name="mla",
    )(q, kcat, vt, wv)


def _gla_kernel(qkf_ref, vf_ref, lrf_ref, qkb_ref, vb_ref, lrb_ref, wd_ref, bd_ref, s0_ref,
                of_ref, ob_ref, sfin_ref, la_ref, st_ref):
    i = pl.program_id(1)
    qk_refs, v_refs, lr_refs, o_refs = (qkf_ref, qkb_ref), (vf_ref, vb_ref), (lrf_ref, lrb_ref), (of_ref, ob_ref)
    tc = qkf_ref.shape[0]
    n_chunk = tc // GLA_CHUNK
    nk = GLA_HEADS * GLA_DK

    @pl.when(i == 0)
    def _():
        st_ref[...] = s0_ref[...]

    row = lax.broadcasted_iota(I32, (GLA_CHUNK, GLA_CHUNK), 0)
    col = lax.broadcasted_iota(I32, (GLA_CHUNK, GLA_CHUNK), 1)
    tris = (col <= row, col >= row)
    tri_bs = tuple(jnp.where(t, 1.0, 0.0).astype(BF16) for t in tris)
    for d in range(2):
        z = _dot(lr_refs[d][...], wd_ref[d]) + bd_ref[d]
        la_ref[d] = jax.nn.log_sigmoid(z) * (1.0 / GLA_TAU)

    states = [[st_ref[d, h] for h in range(GLA_HEADS)] for d in range(2)]
    for c in range(n_chunk):
        for d in range(2):
            cc = c if d == 0 else n_chunk - 1 - c
            rows = slice(cc * GLA_CHUNK, (cc + 1) * GLA_CHUNK)
            la = la_ref[d, rows, :]
            la_hi, la_lo = _split_bf16(la)
            bc = _dot(tri_bs[d], la_hi) + _dot(tri_bs[d], la_lo)
            bend = jnp.sum(la, axis=0, keepdims=True)
            qk = qk_refs[d][rows, :].astype(F32)
            q = qk[:, 0:nk]
            k = qk[:, nk:2 * nk]
            v = v_refs[d][rows, :]
            qd = (q * (GLA_DK ** -0.5) * jnp.exp(bc)).astype(BF16)
            ki = (k * jnp.exp(-bc)).astype(BF16)
            kd = (k * jnp.exp(bend - bc)).astype(BF16)
            dec = jnp.exp(bend)
            for h in range(GLA_HEADS):
                ks = slice(h * GLA_DK, (h + 1) * GLA_DK)
                vs = slice(h * GLA_DV, (h + 1) * GLA_DV)
                vh = v[:, vs]
                sc = jnp.where(tris[d], _dot_nt(qd[:, ks], ki[:, ks]), 0.0)
                st = states[d][h]
                o = _dot(sc.astype(BF16), vh) + _dot_nt(qd[:, ks], st.astype(BF16))
                o_refs[d][rows, vs] = o
                states[d][h] = st * dec[:, ks] + _dot_tn(vh, kd[:, ks])
    for d in range(2):
        for h in range(GLA_HEADS):
            st_ref[d, h] = states[d][h]

    @pl.when(i == pl.num_programs(1) - 1)
    def _():
        sfin_ref[...] = st_ref[...]


def _gla(p, wd_pad, bd, s0):
    b, s, _ = p.shape
    tc = min(512, s)
    n_step = s // tc
    fwd = lambda col: (lambda b_, i: (b_, i, col))
    bwd = lambda col: (lambda b_, i: (b_, n_step - 1 - i, col))
    lr_col = 9 * P_BLK // 128
    state_spec = pl.BlockSpec((None, 2, GLA_HEADS, GLA_DV, GLA_DK), lambda b_, i: (b_, 0, 0, 0, 0))
    return pl.pallas_call(
        _gla_kernel,
        grid=(b, n_step),
        in_specs=[pl.BlockSpec((None, tc, P_BLK), fwd(4)),
                  pl.BlockSpec((None, tc, P_BLK), fwd(5)),
                  pl.BlockSpec((None, tc, 128), fwd(lr_col)),
                  pl.BlockSpec((None, tc, P_BLK), bwd(4)),
                  pl.BlockSpec((None, tc, P_BLK), bwd(5)),
                  pl.BlockSpec((None, tc, 128), bwd(lr_col)),
                  pl.BlockSpec((2, 128, 256), lambda b_, i: (0, 0, 0)),
                  pl.BlockSpec((2, 1, 256), lambda b_, i: (0, 0, 0)),
                  state_spec],
        out_specs=[pl.BlockSpec((None, tc, 512), fwd(0)),
                   pl.BlockSpec((None, tc, 512), bwd(0)),
                   state_spec],
        out_shape=[jax.ShapeDtypeStruct((b, s, 512), F32),
                   jax.ShapeDtypeStruct((b, s, 512), F32),
                   jax.ShapeDtypeStruct((b, 2, GLA_HEADS, GLA_DV, GLA_DK), F32)],
        scratch_shapes=[pltpu.VMEM((2, tc, 256), F32), pltpu.VMEM((2, GLA_HEADS, GLA_DV, GLA_DK), F32)],
        compiler_params=_params("arbitrary", "arbitrary"),
        name="gla",
    )(p, p, p, p, p, p, wd_pad, bd, s0)


def _smlp_kernel(u_ref, v_ref, g_ref, ws_ref, bs_ref, o_ref):
    n_chunk = u_ref.shape[0] // SMLP_CHUNK
    g = g_ref[...]
    for c in range(n_chunk):
        rs = slice(c * SMLP_CHUNK, (c + 1) * SMLP_CHUNK)
        vb = _rms(jax.nn.gelu(v_ref[rs, :].astype(F32)), g).astype(BF16)
        u = jax.nn.gelu(u_ref[rs, :].astype(F32))
        for k in range(SMLP_GROUPS):
            cs = slice(k * SMLP_GROUP_DIM, (k + 1) * SMLP_GROUP_DIM)
            mixed = _dot(ws_ref[k], vb[:, cs]) + bs_ref[:, cs]
            o_ref[rs, cs] = (u[:, cs] * mixed).astype(BF16)


def _smlp(p, v_norm, ws, bs_full):
    b, s, _ = p.shape
    tc = min(512, s)
    return pl.pallas_call(
        _smlp_kernel,
        grid=(b, s // tc),
        in_specs=[pl.BlockSpec((None, tc, P_BLK), lambda b_, i: (b_, i, 7)),
                  pl.BlockSpec((None, tc, P_BLK), lambda b_, i: (b_, i, 8)),
                  pl.BlockSpec((1, 512), lambda b_, i: (0, 0)),
                  pl.BlockSpec((SMLP_GROUPS, SMLP_CHUNK, SMLP_CHUNK), lambda b_, i: (0, 0, 0)),
                  pl.BlockSpec((SMLP_CHUNK, 512), lambda b_, i: (0, 0))],
        out_specs=pl.BlockSpec((None, tc, 512), lambda b_, i: (b_, i, 0)),
        out_shape=jax.ShapeDtypeStruct((b, s, 512), BF16),
        compiler_params=_params("arbitrary", "arbitrary"),
        name="smlp",
    )(p, p, v_norm, ws, bs_full)


def _merge_kernel(h_ref, oa_ref, om_ref, gf_ref, gb_ref, gr_ref, os_ref, gn_ref,
                  wg_ref, bg_ref, wb_ref, y_ref, og_ref):
    @pl.when(pl.program_id(2) == 0)
    def _():
        o = gf_ref[...] + gb_ref[...]
        r = gr_ref[...].astype(F32)
        gate = r * jax.nn.sigmoid(r)
        for hh in range(GLA_HEADS):
            vs = slice(hh * GLA_DV, (hh + 1) * GLA_DV)
            og_ref[:, vs] = (_rms(o[:, vs], gn_ref[...]) * gate[:, vs]).astype(BF16)

    h = h_ref[...]
    branches = (oa_ref[...], om_ref[...], og_ref[...], os_ref[...])
    y = None
    for n, o in enumerate(branches):
        gate = jax.nn.sigmoid(_dot(h, wg_ref[n]) + bg_ref[n])
        t = gate * _dot(o, wb_ref[n])
        y = t if y is None else y + t
    y_ref[...] = y.astype(BF16)


def _merge(h, o_a, o_m, o_gf, o_gb, p, o_s, gla_norm, wg, bg, wb):
    b, s, d = h.shape
    tm = min(512, s)
    tn = 512
    row = lambda b_, i, j: (b_, i, 0)
    return pl.pallas_call(
        _merge_kernel,
        grid=(b, s // tm, d // tn),
        in_specs=[pl.BlockSpec((None, tm, d), row),
                  pl.BlockSpec((None, tm, 512), row),
                  pl.BlockSpec((None, tm, 512), row),
                  pl.BlockSpec((None, tm, 512), row),
                  pl.BlockSpec((None, tm, 512), row),
                  pl.BlockSpec((None, tm, P_BLK), lambda b_, i, j: (b_, i, 6)),
                  pl.BlockSpec((None, tm, 512), row),
                  pl.BlockSpec((1, GLA_DV), lambda b_, i, j: (0, 0)),
                  pl.BlockSpec((4, d, tn), lambda b_, i, j: (0, 0, j)),
                  pl.BlockSpec((4, 1, tn), lambda b_, i, j: (0, 0, j)),
                  pl.BlockSpec((4, 512, tn), lambda b_, i, j: (0, 0, j))],
        out_specs=pl.BlockSpec((None, tm, tn), lambda b_, i, j: (b_, i, j)),
        out_shape=jax.ShapeDtypeStruct((b, s, d), BF16),
        scratch_shapes=[pltpu.VMEM((tm, 512), BF16)],
        compiler_params=_params("arbitrary", "arbitrary", "arbitrary"),
        name="merge",
    )(h, o_a, o_m, o_gf, o_gb, p, o_s, gla_norm, wg, bg, wb)


def _outproj_kernel(y_ref, x_ref, mod_ref, g_ref, wo_ref, wr_ref, br_ref, xo_ref, h2_ref, lg_ref):
    w_hi, w_lo = _split_bf16(wr_ref[...])
    w_cat = jnp.concatenate([w_hi, w_lo], axis=1)
    tm = y_ref.shape[0]
    n_part = 2 if tm % 16 == 0 else 1
    rows = [slice(n * tm // n_part, (n + 1) * tm // n_part) for n in range(n_part)]
    z_next = _dot(y_ref[rows[0], :], wo_ref[...])
    for n, rs in enumerate(rows):
        z = z_next
        if n + 1 < n_part:
            z_next = _dot(y_ref[rows[n + 1], :], wo_ref[...])
        xn = x_ref[rs, :] + mod_ref[2:3, :] * _rms(z, g_ref[1:2, :])
        xo_ref[rs, :] = xn
        h2 = _rms(xn, g_ref[2:3, :]) * (1.0 + mod_ref[4:5, :]) + mod_ref[3:4, :]
        half = h2.shape[1] // 2
        h2_ref[rs, :] = _pack_bf16_pair(h2[:, :half], h2[:, half:])
        h_hi, h_lo = _split_bf16(h2)
        hh = _dot(h_hi, w_cat)
        lg_ref[rs, :] = hh[:, :LANE] + hh[:, LANE:] + _dot(h_lo, w_hi) + br_ref[...]


def _outproj(y, x, mod, norm_g, w_out, wr_pad, br_pad):
    b, s, d = x.shape
    tm = min(256, s)
    row = lambda b_, i: (b_, i, 0)
    return pl.pallas_call(
        _outproj_kernel,
        grid=(b, s // tm),
        in_specs=[pl.BlockSpec((None, tm, d), row),
                  pl.BlockSpec((None, tm, d), row),
                  pl.BlockSpec((None, 8, d), lambda b_, i: (b_, 0, 0)),
                  pl.BlockSpec((4, d), lambda b_, i: (0, 0)),
                  pl.BlockSpec((d, d), lambda b_, i: (0, 0)),
                  pl.BlockSpec((d, LANE), lambda b_, i: (0, 0)),
                  pl.BlockSpec((1, LANE), lambda b_, i: (0, 0))],
        out_specs=[pl.BlockSpec((None, tm, d), row),
                   pl.BlockSpec((None, tm, d // 2), row),
                   pl.BlockSpec((None, tm, LANE), row)],
        out_shape=[jax.ShapeDtypeStruct((b, s, d), F32),
                   jax.ShapeDtypeStruct((b, s, d // 2), jnp.uint32),
                   jax.ShapeDtypeStruct((b, s, LANE), F32)],
        compiler_params=_params("arbitrary", "arbitrary"),
        name="outproj",
    )(y, x, mod, norm_g, w_out, wr_pad, br_pad)


def _route_kernel(lg_ref, idx_ref, w_ref, rank_ref, cnt_ref, carry_ref):
    i = pl.program_id(0)

    @pl.when(i == 0)
    def _():
        carry_ref[...] = jnp.zeros(carry_ref.shape, F32)

    lg = lg_ref[...]
    tm = lg.shape[0]
    lane = lax.broadcasted_iota(I32, lg.shape, 1).astype(F32)
    vals, idxs = [], []
    onehot = jnp.zeros(lg.shape, F32)
    for _ in range(TOP_K):
        m = jnp.max(lg, axis=-1, keepdims=True)
        sel = jnp.min(jnp.where(lg == m, lane, float(LANE)), axis=-1, keepdims=True)
        hit = lane == sel
        vals.append(m)
        idxs.append(sel)
        onehot = jnp.where(hit, 1.0, onehot)
        lg = jnp.where(hit, NEG_BIG, lg)
    es = [jnp.exp(v - vals[0]) for v in vals]
    den = es[0] + es[1] + es[2] + es[3]
    r = lax.broadcasted_iota(I32, (tm, tm), 0)
    c = lax.broadcasted_iota(I32, (tm, tm), 1)
    lower = (c < r).astype(F32).astype(BF16)
    before = _dot(lower, onehot.astype(BF16)) + carry_ref[...]
    for k in range(TOP_K):
        idx_ref[:, k:k + 1] = idxs[k].astype(I32)
        w_ref[:, k:k + 1] = es[k] / den
        rank_ref[:, k:k + 1] = jnp.sum(jnp.where(lane == idxs[k], before, 0.0), axis=-1,
                                       keepdims=True).astype(I32)
    carry_ref[...] = carry_ref[...] + jnp.sum(onehot, axis=0, keepdims=True)
    cnt_ref[...] = carry_ref[...]


def _route(logits):
    t = logits.shape[0]
    tm = 256
    return pl.pallas_call(
        _route_kernel,
        grid=(t // tm,),
        in_specs=[pl.BlockSpec((tm, LANE), lambda i: (i, 0))],
        out_specs=[pl.BlockSpec((tm, TOP_K), lambda i: (i, 0)),
                   pl.BlockSpec((tm, TOP_K), lambda i: (i, 0)),
                   pl.BlockSpec((tm, TOP_K), lambda i: (i, 0)),
                   pl.BlockSpec((1, LANE), lambda i: (0, 0))],
        out_shape=[jax.ShapeDtypeStruct((t, TOP_K), I32),
                   jax.ShapeDtypeStruct((t, TOP_K), F32),
                   jax.ShapeDtypeStruct((t, TOP_K), I32),
                   jax.ShapeDtypeStruct((1, LANE), F32)],
        scratch_shapes=[pltpu.VMEM((1, LANE), F32)],
        compiler_params=_params("arbitrary"),
        name="route",
    )(logits)


def _dispatch_kernel(pad_ref, dest_hbm, src_ref, *rest, zero_pads):
    xs_out, dest_smem, zero_ref, idx_sem, sem = rest[-5:]
    i = pl.program_id(0)
    tm = src_ref.shape[0]
    cp = pltpu.make_async_copy(dest_hbm.at[i], dest_smem, idx_sem)
    cp.start()

    if zero_pads:
        @pl.when(i == 0)
        def _():
            zero_ref[...] = jnp.zeros(zero_ref.shape, zero_ref.dtype)

            def pad_copy(e, r):
                return pltpu.make_async_copy(zero_ref.at[pl.ds(0, 1)],
                                             xs_out.at[pl.ds(pad_ref[e] + r, 1)], sem)

            def start_pads(e, carry):
                def body(r, c):
                    pad_copy(e, r).start()
                    return c
                return lax.fori_loop(0, pad_ref[N_EXPERTS + e], body, carry)

            def wait_pads(e, carry):
                def body(r, c):
                    pad_copy(e, r).wait()
                    return c
                return lax.fori_loop(0, pad_ref[N_EXPERTS + e], body, carry)

            lax.fori_loop(0, N_EXPERTS, start_pads, 0)
            lax.fori_loop(0, N_EXPERTS, wait_pads, 0)

    cp.wait()

    def row_copy(t, k):
        return pltpu.make_async_copy(src_ref.at[pl.ds(t, 1)],
                                     xs_out.at[pl.ds(dest_smem[t * TOP_K + k], 1)], sem)

    def issue(t):
        for k in range(TOP_K):
            row_copy(t, k).start()

    def drain(t):
        for k in range(TOP_K):
            row_copy(0, k).wait()

    _for_each_row(tm, issue)
    _for_each_row(tm, drain)


def _dispatch(pads, dest, src, xs, n_rows):
    t, d = src.shape
    tm = min(1024, t)
    dest2 = dest.reshape(t // tm, tm * TOP_K)
    first = xs is None
    args = (pads, dest2, src) + (() if first else (xs,))
    return pl.pallas_call(
        functools.partial(_dispatch_kernel, zero_pads=first),
        grid=(t // tm,),
        in_specs=[pl.BlockSpec(memory_space=pltpu.SMEM),
                  pl.BlockSpec(memory_space=pl.ANY),
                  pl.BlockSpec((tm, d), lambda i: (i, 0))] + ([] if first else [pl.BlockSpec(memory_space=pl.ANY)]),
        out_specs=pl.BlockSpec(memory_space=pl.ANY),
        out_shape=jax.ShapeDtypeStruct((n_rows, d), src.dtype),
        scratch_shapes=[pltpu.SMEM((tm * TOP_K,), I32), pltpu.VMEM((8, d), src.dtype),
                        pltpu.SemaphoreType.DMA, pltpu.SemaphoreType.DMA],
        input_output_aliases={} if first else {3: 0},
        compiler_params=_params("arbitrary"),
        name="dispatch",
    )(*args)


def _expert_kernel(te_ref, nu_ref, x_ref, w1_ref, b1_ref, w2_ref, b2_ref, y_ref):
    @pl.when(pl.program_id(0) < nu_ref[0])
    def _():
        x_a, x_b = _unpack_bf16_pair(x_ref[...])
        half = x_a.shape[1]
        z = (_dot(x_a.astype(BF16), w1_ref[:half, :]) + _dot(x_b.astype(BF16), w1_ref[half:, :])
             + b1_ref[...])
        glu = jnp.minimum(z[:, :D_EXPERT], SWIGLU_LIMIT)
        lin = jnp.clip(z[:, D_EXPERT:], -SWIGLU_LIMIT, SWIGLU_LIMIT)
        act = glu * jax.nn.sigmoid(SWIGLU_ALPHA * glu) * (lin + 1.0)
        y = _dot(act.astype(BF16), w2_ref[...]) + b2_ref[...]
        y_ref[...] = _pack_bf16_pair(y[:, :half], y[:, half:])


def _experts(layer, tile_e, n_used, xs, w1, b1, w2, b2):
    n_rows, half = xs.shape
    d = 2 * half
    tr = EXPERT_TILE
    n_tiles = n_rows // tr
    row = lambda i, te, nu: (jnp.minimum(i, nu[0] - 1), 0)
    wsel = lambda i, te, nu: (layer, te[i], 0, 0)
    return pl.pallas_call(
        _expert_kernel,
        grid_spec=pltpu.PrefetchScalarGridSpec(
            num_scalar_prefetch=2,
            grid=(n_tiles,),
            in_specs=[pl.BlockSpec((tr, half), row),
                      pl.BlockSpec((None, None, d, 2 * D_EXPERT), wsel),
                      pl.BlockSpec((None, None, 1, 2 * D_EXPERT), wsel),
                      pl.BlockSpec((None, None, D_EXPERT, d), wsel),
                      pl.BlockSpec((None, None, 1, d), wsel)],
            out_specs=pl.BlockSpec((tr, half), row)),
        out_shape=jax.ShapeDtypeStruct((n_rows, half), jnp.uint32),
        compiler_params=_params("arbitrary"),
        name="experts",
    )(tile_e, n_used, xs, w1, b1, w2, b2)


def _combine_kernel(dest_hbm, w_ref, x_ref, mod_ref, g_ref, ys_hbm, o_ref,
                    dest0, dest1, buf0, buf1, idx_sem, sem0, sem1):
    dest_smem, buf, sem = (dest0, dest1), (buf0, buf1), (sem0, sem1)
    tm = x_ref.shape[0]
    n_step = pl.num_programs(0) * pl.num_programs(1)
    i = pl.program_id(0) * pl.num_programs(1) + pl.program_id(1)

    def row_copy(sl, t, k):
        return pltpu.make_async_copy(ys_hbm.at[pl.ds(dest_smem[sl][t * TOP_K + k], 1)],
                                     buf[sl].at[k, pl.ds(t, 1)], sem[sl])

    def start_gather(step, sl):
        cp = pltpu.make_async_copy(dest_hbm.at[step], dest_smem[sl], idx_sem)
        cp.start()
        cp.wait()

        def issue(t):
            for k in range(TOP_K):
                row_copy(sl, t, k).start()

        _for_each_row(tm, issue)

    @pl.when(i == 0)
    def _():
        start_gather(0, 0)

    def step(slot):
        @pl.when(i + 1 < n_step)
        def _():
            start_gather(i + 1, 1 - slot)

        def drain(t):
            for k in range(TOP_K):
                row_copy(slot, 0, k).wait()

        _for_each_row(tm, drain)
        w = w_ref[...]
        f_a, f_b = None, None
        for k in range(TOP_K):
            y_a, y_b = _unpack_bf16_pair(buf[slot][k])
            f_a = w[:, k:k + 1] * y_a if f_a is None else f_a + w[:, k:k + 1] * y_a
            f_b = w[:, k:k + 1] * y_b if f_b is None else f_b + w[:, k:k + 1] * y_b
        half = f_a.shape[1]
        ms = (jnp.sum(f_a * f_a, axis=-1, keepdims=True)
              + jnp.sum(f_b * f_b, axis=-1, keepdims=True)) * (1.0 / (2 * half))
        inv = lax.rsqrt(ms + EPS)
        o_ref[:, :half] = x_ref[:, :half] + mod_ref[5:6, :half] * (f_a * inv * g_ref[3:4, :half])
        o_ref[:, half:] = x_ref[:, half:] + mod_ref[5:6, half:] * (f_b * inv * g_ref[3:4, half:])

    for slot in range(2):
        pl.when(lax.rem(i, 2) == slot)(functools.partial(step, slot))


def _combine(dest, w, x, mod, norm_g, ys):
    b, s, d = x.shape
    tm = min(256, s)
    ns = s // tm
    dest2 = dest.reshape(b * ns, tm * TOP_K)
    return pl.pallas_call(
        _combine_kernel,
        grid=(b, ns),
        in_specs=[pl.BlockSpec(memory_space=pl.ANY),
                  pl.BlockSpec((None, tm, TOP_K), lambda b_, i: (b_, i, 0)),
                  pl.BlockSpec((None, tm, d), lambda b_, i: (b_, i, 0)),
                  pl.BlockSpec((None, 8, d), lambda b_, i: (b_, 0, 0)),
                  pl.BlockSpec((4, d), lambda b_, i: (0, 0)),
                  pl.BlockSpec(memory_space=pl.ANY)],
        out_specs=pl.BlockSpec((None, tm, d), lambda b_, i: (b_, i, 0)),
        out_shape=jax.ShapeDtypeStruct((b, s, d), F32),
        scratch_shapes=[pltpu.SMEM((tm * TOP_K,), I32), pltpu.SMEM((tm * TOP_K,), I32),
                        pltpu.VMEM((TOP_K, tm, d // 2), jnp.uint32),
                        pltpu.VMEM((TOP_K, tm, d // 2), jnp.uint32),
                        pltpu.SemaphoreType.DMA, pltpu.SemaphoreType.DMA, pltpu.SemaphoreType.DMA],
        compiler_params=_params("arbitrary", "arbitrary"),
        name="combine",
    )(dest2, w.reshape(b, s, TOP_K), x, mod, norm_g, ys)


def _rot_half_cols(w, head_dim):
    d_in, n = w.shape
    w3 = w.reshape(d_in, n // head_dim, head_dim)
    half = head_dim // 2
    return jnp.concatenate([-w3[..., half:], w3[..., :half]], axis=-1).reshape(d_in, n)


def _extended_in_weights(w_in):
    sizes = (512, 128, 128, MLA_Q_RANK, MLA_KV_RANK, MLA_ROPE_DIM, 256, 256, 512,
             2 * GLA_DECAY_RANK, 512, 512, 512)
    cols, off = [], 0
    for n in sizes:
        cols.append(w_in[:, off:off + n])
        off += n
    (a_q, a_k, a_v, m_cq, m_ckv, m_kr, g_q, g_k, g_v, g_lr, g_r, s_u, s_v) = cols
    pad = jnp.zeros((w_in.shape[0], P_BLK - 2 * GLA_DECAY_RANK), w_in.dtype)
    ext = jnp.concatenate(
        [a_q, _rot_half_cols(a_q, SWA_HEAD_DIM),
         a_k, _rot_half_cols(a_k, SWA_HEAD_DIM), a_v, m_kr, _rot_half_cols(m_kr, MLA_ROPE_DIM),
         m_cq, m_ckv, g_q, g_k, g_v, g_r, s_u, s_v, g_lr, pad], axis=1)
    return ext.astype(BF16)


def _rope_tables(n_tok):
    n_rows = n_tok // GRID_W
    rows = jnp.repeat(jnp.arange(n_rows, dtype=F32), GRID_W)
    cols = jnp.tile(jnp.arange(GRID_W, dtype=F32), n_rows)
    n_freq = SWA_HEAD_DIM // 4
    inv_freq = ROPE_BASE ** (-jnp.arange(n_freq, dtype=F32) / n_freq)
    ang = jnp.concatenate([rows[:, None] * inv_freq, cols[:, None] * inv_freq], axis=-1)
    cos = jnp.concatenate([jnp.cos(ang), jnp.cos(ang)], axis=-1)
    sin = jnp.concatenate([jnp.sin(ang), jnp.sin(ang)], axis=-1)
    return jnp.tile(cos, (1, 8)), jnp.tile(sin, (1, 8))


def _moe(layer, h2_parts, logits_parts, w1, b1, w2, b2):
    logits = jnp.concatenate(logits_parts, axis=0)
    t_all = logits.shape[0]
    top_idx, top_w, rank, counts = _route(logits)
    counts = counts[0, :N_EXPERTS].astype(I32)
    tr = EXPERT_TILE
    padded = (counts + tr - 1) // tr * tr
    p_end = jnp.cumsum(padded)
    p_start = p_end - padded
    dest = p_start[top_idx] + rank
    n_tiles = -(-t_all * TOP_K // tr) + N_EXPERTS
    tile_start = jnp.arange(n_tiles, dtype=I32) * tr
    tile_e = jnp.minimum(jnp.sum((p_end[None, :] <= tile_start[:, None]).astype(I32), axis=1),
                         N_EXPERTS - 1)
    n_used = (p_end[-1:] // tr).astype(I32)
    pads = jnp.concatenate([p_start + counts, padded - counts]).astype(I32)
    xs = None
    off = 0
    for part in h2_parts:
        n = part.shape[0]
        xs = _dispatch(pads, dest[off:off + n], part, xs, n_tiles * tr)
        off += n
    ys = _experts(layer, tile_e, n_used, xs, w1, b1, w2, b2)
    return dest, top_w, ys


def kernel(x, c, ctx, c_ctx, ada_w, ada_b, norm_g, w_in, attn_sink, mla_q_norm, mla_w_uq, mla_kv_norm, mla_w_ukv, gla_w_decay, gla_b_decay, gla_out_norm, smlp_v_norm, smlp_w_spatial, smlp_b_spatial, w_branch, w_gate, b_gate, w_out, router_w, router_b, expert_w1, expert_b1, expert_w2, expert_b2):
    bsz, seq, d = x.shape
    n_ctx = ctx.shape[1]
    depth = ada_w.shape[0]

    cs = jnp.zeros((8, d), F32).at[:bsz].set(c).at[bsz].set(c_ctx)
    mod_all = _ada(cs, ada_w, ada_b).reshape(depth, 8, N_MOD, d)
    mod_all = jnp.pad(mod_all, ((0, 0), (0, 0), (0, 8 - N_MOD), (0, 0)))

    cos_t, sin_t = _rope_tables(seq)
    cos_c = jnp.ones((bsz * n_ctx, 512), F32)
    sin_c = jnp.zeros((bsz * n_ctx, 512), F32)
    ctx = ctx.reshape(1, bsz * n_ctx, d)

    w1 = expert_w1.astype(BF16)
    b1 = expert_b1.reshape(depth, N_EXPERTS, 1, 2 * D_EXPERT)
    w2 = expert_w2.astype(BF16)
    b2 = expert_b2.reshape(depth, N_EXPERTS, 1, d)

    for l in range(depth):
        last = l == depth - 1
        mod = mod_all[l, :bsz]
        mod_c = mod_all[l, bsz:bsz + 1]
        g_l = norm_g[l]
        w_ext = _extended_in_weights(w_in[l])
        uq = mla_w_uq[l].reshape(MLA_Q_RANK, MLA_HEADS, MLA_NOPE_DIM + MLA_ROPE_DIM)
        uq_rope = uq[:, :, MLA_NOPE_DIM:].reshape(MLA_Q_RANK, MLA_HEADS * MLA_ROPE_DIM)
        wuq_ext = jnp.concatenate(
            [uq[:, :, :MLA_NOPE_DIM].reshape(MLA_Q_RANK, MLA_HEADS * MLA_NOPE_DIM), uq_rope,
             _rot_half_cols(uq_rope, MLA_ROPE_DIM)], axis=1).astype(BF16)
        ukv = mla_w_ukv[l].reshape(MLA_KV_RANK, MLA_HEADS, MLA_NOPE_DIM + MLA_V_DIM)
        wkn = jnp.transpose(ukv[:, :, :MLA_NOPE_DIM], (1, 0, 2)).astype(BF16)
        wv = jnp.transpose(ukv[:, :, MLA_NOPE_DIM:], (1, 0, 2)).astype(BF16)
        qn_g = mla_q_norm[l].reshape(1, MLA_Q_RANK)
        kv_g = mla_kv_norm[l].reshape(1, MLA_KV_RANK)
        wd_pad = jnp.zeros((2, 128, 256), F32)
        for dd in range(2):
            wd_pad = wd_pad.at[dd, dd * GLA_DECAY_RANK:(dd + 1) * GLA_DECAY_RANK].set(gla_w_decay[l, dd])
        wd_pad = wd_pad.astype(BF16)
        bd = gla_b_decay[l].reshape(2, 1, 256)
        gla_norm = gla_out_norm[l].reshape(1, GLA_DV)
        v_norm = smlp_v_norm[l].reshape(1, 512)
        ws = smlp_w_spatial[l].astype(BF16)
        bs_full = jnp.repeat(smlp_b_spatial[l].T, SMLP_GROUP_DIM, axis=1)
        wg = w_gate[l].astype(BF16)
        bg = b_gate[l].reshape(4, 1, d)
        wb = w_branch[l].astype(BF16)
        wo = w_out[l].astype(BF16)
        wr_pad = jnp.pad(router_w[l], ((0, 0), (0, LANE - N_EXPERTS)))
        br_pad = jnp.pad(router_b[l].reshape(1, N_EXPERTS), ((0, 0), (0, LANE - N_EXPERTS)),
                         constant_values=NEG_BIG)
        sink = attn_sink[l]

        h, p = _inproj(x, mod, g_l, w_ext)
        hc, pc1 = _inproj(ctx, mod_c, g_l, w_ext)
        pc = pc1.reshape(bsz, n_ctx, P_WIDTH)
        ka, kcat, q_m = _prep(p, cos_t, sin_t, qn_g, kv_g, wuq_ext, wkn)
        _, kcat_c, q_mc = _prep(pc1, cos_c, sin_c, qn_g, kv_g, wuq_ext, wkn)
        kcat_c = kcat_c.reshape(bsz, n_ctx, MLA_QK_PAD)

        o_a = _swa(sink, p, ka, cos_t, sin_t, pc)
        o_m = _mla(q_m, jnp.concatenate([kcat, kcat_c], axis=1), wv)
        s0 = jnp.zeros((bsz, 2, GLA_HEADS, GLA_DV, GLA_DK), F32)
        ogf_c, ogb_c, s_ctx = _gla(pc, wd_pad, bd, s0)
        ogf, ogb, _ = _gla(p, wd_pad, bd, s_ctx)
        o_s = _smlp(p, v_norm, ws, bs_full)
        y = _merge(h, o_a, o_m, ogf, ogb, p, o_s, gla_norm, wg, bg, wb)
        x, h2, lg = _outproj(y, x, mod, g_l, wo, wr_pad, br_pad)
        h2_parts = [h2.reshape(bsz * seq, d // 2)]
        lg_parts = [lg.reshape(bsz * seq, LANE)]

        if not last:
            oc_a = _swa_ctx(sink, pc)
            q_mc = q_mc.reshape(MLA_HEADS, bsz, n_ctx, MLA_QK_PAD).transpose(1, 0, 2, 3)
            oc_m = _mla(q_mc, kcat_c, wv)
            oc_s = _smlp(pc, v_norm, ws, bs_full)
            flat = lambda a: a.reshape(1, bsz * n_ctx, a.shape[-1])
            yc = _merge(hc, flat(oc_a), flat(oc_m), flat(ogf_c), flat(ogb_c), pc1,
                        flat(oc_s), gla_norm, wg, bg, wb)
            ctx, h2c, lgc = _outproj(yc, ctx, mod_c, g_l, wo, wr_pad, br_pad)
            h2_parts.append(h2c.reshape(bsz * n_ctx, d // 2))
            lg_parts.append(lgc.reshape(bsz * n_ctx, LANE))

        dest, top_w, ys = _moe(l, h2_parts, lg_parts, w1, b1, w2, b2)
        n_lat = bsz * seq
        x = _combine(dest[:n_lat], top_w[:n_lat], x, mod, g_l, ys)
        if not last:
            ctx = _combine(dest[n_lat:], top_w[n_lat:], ctx, mod_c, g_l, ys)
    return x
```

```python
import functools

import jax
import jax.numpy as jnp
from jax import lax
from jax.experimental import pallas as pl
from jax.experimental.pallas import tpu as pltpu

F32 = jnp.float32
BF16 = jnp.bfloat16
I32 = jnp.int32

EPS = 1e-6
ROPE_BASE = 10000.0
GRID_W = 64
N_MOD = 6

SWA_HEADS = 8
SWA_KV_HEADS = 2
SWA_GROUP = SWA_HEADS // SWA_KV_HEADS
SWA_HEAD_DIM = 64
SWA_BLOCK = 128

MLA_HEADS = 4
MLA_Q_RANK = 384
MLA_KV_RANK = 128
MLA_NOPE_DIM = 128
MLA_ROPE_DIM = 64
MLA_V_DIM = 128
MLA_QK_PAD = 256
MLA_QUERY_CHUNK = 512
MLA_QUERY_TILE = 2048
LOG2_E = 1.4426950408889634

GLA_HEADS = 4
GLA_DK = 64
GLA_DV = 128
GLA_DECAY_RANK = 16
GLA_TAU = 16.0
GLA_CHUNK = 64

SMLP_GROUPS = 4
SMLP_CHUNK = 128
SMLP_GROUP_DIM = 128

N_EXPERTS = 32
TOP_K = 4
D_EXPERT = 1024
SWIGLU_LIMIT = 7.0
SWIGLU_ALPHA = 1.702

LANE = 128
NEG_BIG = -1e30
VMEM_LIMIT = 56 * 1024 * 1024

P_BLK = 512
P_NBLK = 10
P_WIDTH = P_BLK * P_NBLK
EXPERT_TILE = 256
ROW_GROUP = 8


def _dot(a, b):
    return jnp.dot(a, b, preferred_element_type=F32)


def _dot_nt(a, b):
    return lax.dot_general(a, b, (((1,), (1,)), ((), ())), preferred_element_type=F32)


def _dot_tn(a, b):
    return lax.dot_general(a, b, (((0,), (0,)), ((), ())), preferred_element_type=F32)


def _split_bf16(a):
    hi = a.astype(BF16)
    lo = (a - hi.astype(F32)).astype(BF16)
    return hi, lo


def _rms(x, g):
    return x * lax.rsqrt(jnp.mean(x * x, axis=-1, keepdims=True) + EPS) * g


def _pack_bf16_pair(a, b):
    hi = pltpu.bitcast(a.astype(BF16).astype(F32), jnp.uint32)
    lo = pltpu.bitcast(b.astype(BF16).astype(F32), jnp.uint32)
    return hi | (lo >> 16)


def _unpack_bf16_pair(w):
    a = pltpu.bitcast(w & jnp.uint32(0xFFFF0000), F32)
    b = pltpu.bitcast(w << 16, F32)
    return a, b


def _for_each_row(n_rows, fn):
    def group(u, carry):
        base = pl.multiple_of(u * ROW_GROUP, ROW_GROUP)
        for j in range(ROW_GROUP):
            fn(base + j)
        return carry

    lax.fori_loop(0, n_rows // ROW_GROUP, group, 0)


def _params(*sem):
    return pltpu.CompilerParams(dimension_semantics=sem, vmem_limit_bytes=VMEM_LIMIT)


def _ada_kernel(c_ref, w_ref, b_ref, o_ref):
    c = c_ref[...]
    a_hi, a_lo = _split_bf16(c * jax.nn.sigmoid(c))
    w_hi, w_lo = _split_bf16(w_ref[...])
    o_ref[...] = _dot(a_hi, w_hi) + _dot(a_lo, w_hi) + _dot(a_hi, w_lo) + b_ref[...]


def _ada(cs, ada_w, ada_b):
    n_layer, d, n = ada_w.shape
    tn = 1024
    return pl.pallas_call(
        _ada_kernel,
        grid=(n_layer, n // tn),
        in_specs=[pl.BlockSpec((8, d), lambda l, j: (0, 0)),
                  pl.BlockSpec((None, d, tn), lambda l, j: (l, 0, j)),
                  pl.BlockSpec((None, 1, tn), lambda l, j: (l, 0, j))],
        out_specs=pl.BlockSpec((None, 8, tn), lambda l, j: (l, 0, j)),
        out_shape=jax.ShapeDtypeStruct((n_layer, 8, n), F32),
        compiler_params=_params("arbitrary", "arbitrary"),
        name="ada",
    )(cs, ada_w, ada_b.reshape(n_layer, 1, n))


def _inproj_kernel(x_ref, mod_ref, g_ref, w_ref, h_ref, p_ref):
    y = _rms(x_ref[...], g_ref[0:1, :])
    hb = (y * (1.0 + mod_ref[1:2, :]) + mod_ref[0:1, :]).astype(BF16)
    h_ref[...] = hb
    p_ref[...] = _dot(hb, w_ref[...]).astype(BF16)


def _inproj(x, mod, norm_g, w_ext):
    b, s, d = x.shape
    tm = min(256, s)
    return pl.pallas_call(
        _inproj_kernel,
        grid=(b, s // tm),
        in_specs=[pl.BlockSpec((None, tm, d), lambda b_, i: (b_, i, 0)),
                  pl.BlockSpec((None, 8, d), lambda b_, i: (b_, 0, 0)),
                  pl.BlockSpec((4, d), lambda b_, i: (0, 0)),
                  pl.BlockSpec((d, P_WIDTH), lambda b_, i: (0, 0), pipeline_mode=pl.Buffered(1))],
        out_specs=[pl.BlockSpec((None, tm, d), lambda b_, i: (b_, i, 0)),
                   pl.BlockSpec((None, tm, P_WIDTH), lambda b_, i: (b_, i, 0))],
        out_shape=[jax.ShapeDtypeStruct((b, s, d), BF16),
                   jax.ShapeDtypeStruct((b, s, P_WIDTH), BF16)],
        compiler_params=_params("arbitrary", "arbitrary"),
        name="inproj",
    )(x, mod, norm_g, w_ext)


def _prep_kernel(kblk_ref, cblk_ref, cos_ref, sin_ref, qn_g_ref, kv_g_ref, wuq_ref, wkn_ref,
                 ka_ref, kcat_ref, q_ref):
    cos = cos_ref[...]
    sin = sin_ref[...]
    kb = kblk_ref[...].astype(F32)
    ka_ref[...] = (kb[:, 0:128] * cos[:, 0:128] + kb[:, 128:256] * sin[:, 0:128]).astype(BF16)
    kr = kb[:, 384:448] * cos[:, 0:64] + kb[:, 448:512] * sin[:, 0:64]
    cb = cblk_ref[...].astype(F32)
    ckvn = _rms(cb[:, MLA_Q_RANK:], kv_g_ref[...])
    tm = kb.shape[0]
    kcat_ref[:, 0:128] = ckvn.astype(BF16)
    kcat_ref[:, 128:192] = kr.astype(BF16)
    kcat_ref[:, 192:256] = jnp.zeros((tm, 64), BF16)
    cqn = _rms(cb[:, :MLA_Q_RANK], qn_g_ref[...]).astype(BF16)
    qa = _dot(cqn, wuq_ref[...])
    qr = qa[:, 512:768] * cos[:, 0:256] + qa[:, 768:1024] * sin[:, 0:256]
    scale = (MLA_NOPE_DIM + MLA_ROPE_DIM) ** -0.5 * LOG2_E
    for h in range(MLA_HEADS):
        qn = qa[:, h * 128:(h + 1) * 128].astype(BF16)
        q_ref[h, :, 0:128] = (_dot_nt(qn, wkn_ref[h]) * scale).astype(BF16)
        q_ref[h, :, 128:192] = (qr[:, h * 64:(h + 1) * 64] * scale).astype(BF16)
        q_ref[h, :, 192:256] = jnp.zeros((tm, 64), BF16)


def _prep(p, cos_t, sin_t, qn_g, kv_g, wuq_ext, wkn):
    b, s, _ = p.shape
    tm = min(512, s)
    return pl.pallas_call(
        _prep_kernel,
        grid=(b, s // tm),
        in_specs=[pl.BlockSpec((None, tm, P_BLK), lambda b_, i: (b_, i, 2)),
                  pl.BlockSpec((None, tm, P_BLK), lambda b_, i: (b_, i, 3)),
                  pl.BlockSpec((tm, 512), lambda b_, i: (i, 0)),
                  pl.BlockSpec((tm, 512), lambda b_, i: (i, 0)),
                  pl.BlockSpec((1, MLA_Q_RANK), lambda b_, i: (0, 0)),
                  pl.BlockSpec((1, MLA_KV_RANK), lambda b_, i: (0, 0)),
                  pl.BlockSpec((MLA_Q_RANK, 1024), lambda b_, i: (0, 0)),
                  pl.BlockSpec((MLA_HEADS, MLA_KV_RANK, MLA_NOPE_DIM), lambda b_, i: (0, 0, 0))],
        out_specs=[pl.BlockSpec((None, tm, 128), lambda b_, i: (b_, i, 0)),
                   pl.BlockSpec((None, tm, MLA_QK_PAD), lambda b_, i: (b_, i, 0)),
                   pl.BlockSpec((None, MLA_HEADS, tm, MLA_QK_PAD), lambda b_, i: (b_, 0, i, 0))],
        out_shape=[jax.ShapeDtypeStruct((b, s, 128), BF16),
                   jax.ShapeDtypeStruct((b, s, MLA_QK_PAD), BF16),
                   jax.ShapeDtypeStruct((b, MLA_HEADS, s, MLA_QK_PAD), BF16)],
        compiler_params=_params("arbitrary", "arbitrary"),
        name="prep",
    )(p, p, cos_t, sin_t, qn_g, kv_g, wuq_ext, wkn)


def _swa_group(q, sink_ref, g, k_all, v_all, mask, o_ref, rows):
    heads = range(g * SWA_GROUP, (g + 1) * SWA_GROUP)
    n_q = q.shape[0]
    qs = jnp.concatenate([q[:, h * SWA_HEAD_DIM:(h + 1) * SWA_HEAD_DIM] for h in heads], axis=0)
    sink = jnp.concatenate([jnp.full((n_q, 1), sink_ref[h], F32) for h in heads], axis=0)
    s = _dot_nt(qs, k_all)
    if mask is not None:
        s = s + jnp.concatenate([mask] * SWA_GROUP, axis=0)
    m = jnp.maximum(jnp.max(s, axis=-1, keepdims=True), sink)
    e = jnp.exp(s - m)
    den = jnp.sum(e, axis=-1, keepdims=True) + jnp.exp(sink - m)
    o = _dot(e.astype(BF16), v_all) / den
    for n, h in enumerate(heads):
        o_ref[rows, h * SWA_HEAD_DIM:(h + 1) * SWA_HEAD_DIM] = o[n * n_q:(n + 1) * n_q, :].astype(BF16)


def _swa_kernel(sink_ref, q_ref, qr_ref, cos_ref, sin_ref, kp_ref, kc_ref, kn_ref,
                vp_ref, vc_ref, vn_ref, kx_ref, vx_ref, o_ref):
    i = pl.program_id(1)
    n_pair = pl.num_programs(1)
    blk = SWA_BLOCK
    scale = SWA_HEAD_DIM ** -0.5
    q = ((q_ref[...].astype(F32) * cos_ref[...] + qr_ref[...].astype(F32) * sin_ref[...])
         * scale).astype(BF16)
    n_ctx = kx_ref.shape[0]
    n_keys = 3 * blk + n_ctx
    row = lax.broadcasted_iota(I32, (blk, n_keys), 0)
    col = lax.broadcasted_iota(I32, (blk, n_keys), 1)
    dist = col - row
    in_next = (col >= 2 * blk) & (col < 3 * blk)
    for sub in range(2):
        rows = slice(sub * blk, (sub + 1) * blk)
        if sub == 0:
            lo = jnp.where(i > 0, 0, n_keys)
            hi = 2 * blk
        else:
            lo = 0
            hi = jnp.where(i < n_pair - 1, 2 * blk, -n_keys)
        bias_prev = jnp.where(dist >= lo, 0.0, NEG_BIG)
        bias_next = jnp.where(dist <= hi, 0.0, NEG_BIG)
        mask = jnp.where(col < blk, bias_prev, jnp.where(in_next, bias_next, 0.0))
        for g in range(SWA_KV_HEADS):
            gs = slice(g * SWA_HEAD_DIM, (g + 1) * SWA_HEAD_DIM)
            own = [(kc_ref[0:blk, gs], vc_ref[0:blk, gs]),
                   (kc_ref[blk:2 * blk, gs], vc_ref[blk:2 * blk, gs])]
            if sub == 0:
                win = [(kp_ref[:, gs], vp_ref[:, gs])] + own
            else:
                win = own + [(kn_ref[:, gs], vn_ref[:, gs])]
            win = win + [(kx_ref[:, gs], vx_ref[:, gs])]
            k_all = jnp.concatenate([k for k, _ in win], axis=0)
            v_all = jnp.concatenate([v for _, v in win], axis=0)
            _swa_group(q[rows, :], sink_ref, g, k_all, v_all, mask, o_ref, rows)


def _swa_ctx_kernel(sink_ref, q_ref, kx_ref, vx_ref, o_ref):
    scale = SWA_HEAD_DIM ** -0.5
    q = (q_ref[...].astype(F32) * scale).astype(BF16)
    for g in range(SWA_KV_HEADS):
        gs = slice(g * SWA_HEAD_DIM, (g + 1) * SWA_HEAD_DIM)
        _swa_group(q, sink_ref, g, kx_ref[:, gs], vx_ref[:, gs], None, o_ref, slice(None))


def _swa(sink, p, ka, cos_t, sin_t, pc):
    b, s, _ = p.shape
    n_ctx = pc.shape[1]
    blk = SWA_BLOCK
    nb = s // blk
    pair = 2 * blk
    vcol = (2 * P_BLK + 256) // 128
    prev = lambda col: (lambda b_, i: (b_, jnp.maximum(2 * i - 1, 0), col))
    cur = lambda col: (lambda b_, i: (b_, i, col))
    nxt = lambda col: (lambda b_, i: (b_, jnp.minimum(2 * i + 2, nb - 1), col))
    return pl.pallas_call(
        _swa_kernel,
        grid=(b, nb // 2),
        in_specs=[pl.BlockSpec(memory_space=pltpu.SMEM),
                  pl.BlockSpec((None, pair, P_BLK), lambda b_, i: (b_, i, 0)),
                  pl.BlockSpec((None, pair, P_BLK), lambda b_, i: (b_, i, 1)),
                  pl.BlockSpec((pair, 512), lambda b_, i: (i, 0)),
                  pl.BlockSpec((pair, 512), lambda b_, i: (i, 0)),
                  pl.BlockSpec((None, blk, 128), prev(0)),
                  pl.BlockSpec((None, pair, 128), cur(0)),
                  pl.BlockSpec((None, blk, 128), nxt(0)),
                  pl.BlockSpec((None, blk, 128), prev(vcol)),
                  pl.BlockSpec((None, pair, 128), cur(vcol)),
                  pl.BlockSpec((None, blk, 128), nxt(vcol)),
                  pl.BlockSpec((None, n_ctx, 128), lambda b_, i: (b_, 0, 2 * P_BLK // 128)),
                  pl.BlockSpec((None, n_ctx, 128), lambda b_, i: (b_, 0, vcol))],
        out_specs=pl.BlockSpec((None, pair, 512), lambda b_, i: (b_, i, 0)),
        out_shape=jax.ShapeDtypeStruct((b, s, 512), BF16),
        compiler_params=_params("arbitrary", "arbitrary"),
        name="swa",
    )(sink, p, p, cos_t, sin_t, ka, ka, ka, p, p, p, pc, pc)


def _swa_ctx(sink, pc):
    b, n_ctx, _ = pc.shape
    blk = SWA_BLOCK
    vcol = (2 * P_BLK + 256) // 128
    return pl.pallas_call(
        _swa_ctx_kernel,
        grid=(b, n_ctx // blk),
        in_specs=[pl.BlockSpec(memory_space=pltpu.SMEM),
                  pl.BlockSpec((None, blk, P_BLK), lambda b_, i: (b_, i, 0)),
                  pl.BlockSpec((None, n_ctx, 128), lambda b_, i: (b_, 0, 2 * P_BLK // 128)),
                  pl.BlockSpec((None, n_ctx, 128), lambda b_, i: (b_, 0, vcol))],
        out_specs=pl.BlockSpec((None, blk, 512), lambda b_, i: (b_, i, 0)),
        out_shape=jax.ShapeDtypeStruct((b, n_ctx, 512), BF16),
        compiler_params=_params("arbitrary", "arbitrary"),
        name="swa_ctx",
    )(sink, pc, pc, pc)


def _mla_kernel(q_ref, kv_ref, vt_ref, wv_ref, o_ref, m_ref, l_ref, acc_ref):
    j = pl.program_id(2)

    @pl.when(j == 0)
    def _():
        m_ref[...] = jnp.full(m_ref.shape, NEG_BIG, F32)
        l_ref[...] = jnp.zeros(l_ref.shape, F32)
        acc_ref[...] = jnp.zeros(acc_ref.shape, F32)

    nh, tq, dq = q_ref.shape
    kv = kv_ref[...]
    vt = vt_ref[...]
    qc = min(MLA_QUERY_CHUNK, tq)
    per_head = tq // qc
    n_chunk = nh * per_head

    def scores(c):
        h, r = divmod(c, per_head)
        return _dot_nt(kv, q_ref[h, r * qc:(r + 1) * qc, :])

    def accumulate(pending):
        cols, alpha, p = pending
        acc_ref[:, cols] = alpha * acc_ref[:, cols] + _dot(vt, p)

    s_next = scores(0)
    pending = None
    for c in range(n_chunk):
        s = s_next
        if c + 1 < n_chunk:
            s_next = scores(c + 1)
        if pending is not None:
            accumulate(pending)
        cols = slice(c * qc, (c + 1) * qc)
        m_prev = m_ref[:, cols]
        m_new = jnp.maximum(m_prev, jnp.max(s, axis=0, keepdims=True))
        alpha = jnp.exp2(m_prev - m_new)
        p = jnp.exp2(s - m_new)
        l_ref[:, cols] = alpha * l_ref[:, cols] + jnp.sum(p, axis=0, keepdims=True)
        m_ref[:, cols] = m_new
        pending = (cols, alpha, p.astype(BF16))
    accumulate(pending)

    @pl.when(j == pl.num_programs(2) - 1)
    def _():
        o = (acc_ref[...] / l_ref[...]).astype(BF16)
        for h in range(nh):
            o_ref[:, h * MLA_V_DIM:(h + 1) * MLA_V_DIM] = _dot_tn(
                o[:, h * tq:(h + 1) * tq], wv_ref[h]).astype(BF16)


def _mla(q, kcat, wv):
    b, nh, sq, dq = q.shape
    sk = kcat.shape[1]
    tq = min(MLA_QUERY_TILE, sq)
    tk = next(t for t in (1280, 640, 512, 384, 256, 128) if sk % t == 0)
    vt = jnp.swapaxes(kcat[:, :, :MLA_KV_RANK], 1, 2)
    return pl.pallas_call(
        _mla_kernel,
        grid=(b, sq // tq, sk // tk),
        in_specs=[pl.BlockSpec((None, nh, tq, dq), lambda b_, i, j: (b_, 0, i, 0)),
                  pl.BlockSpec((None, tk, dq), lambda b_, i, j: (b_, j, 0)),
                  pl.BlockSpec((None, MLA_KV_RANK, tk), lambda b_, i, j: (b_, 0, j)),
                  pl.BlockSpec((nh, MLA_KV_RANK, MLA_V_DIM), lambda b_, i, j: (0, 0, 0))],
        out_specs=pl.BlockSpec((None, tq, nh * MLA_V_DIM), lambda b_, i, j: (b_, i, 0)),
        out_shape=jax.ShapeDtypeStruct((b, sq, nh * MLA_V_DIM), BF16),
        scratch_shapes=[pltpu.VMEM((1, nh * tq), F32), pltpu.VMEM((1, nh * tq), F32),
                        pltpu.VMEM((MLA_KV_RANK, nh * tq), F32)],
        compiler_params=_params("arbitrary", "arbitrary", "arbitrary"),
        name="mla",
    )(q, kcat, vt, wv)


def _gla_kernel(qkf_ref, vf_ref, lrf_ref, qkb_ref, vb_ref, lrb_ref, wd_ref, bd_ref, s0_ref,
                of_ref, ob_ref, sfin_ref, la_ref, st_ref):
    i = pl.program_id(1)
    qk_refs, v_refs, lr_refs, o_refs = (qkf_ref, qkb_ref), (vf_ref, vb_ref), (lrf_ref, lrb_ref), (of_ref, ob_ref)
    tc = qkf_ref.shape[0]
    n_chunk = tc // GLA_CHUNK
    nk = GLA_HEADS * GLA_DK

    @pl.when(i == 0)
    def _():
        st_ref[...] = s0_ref[...]

    row = lax.broadcasted_iota(I32, (GLA_CHUNK, GLA_CHUNK), 0)
    col = lax.broadcasted_iota(I32, (GLA_CHUNK, GLA_CHUNK), 1)
    tris = (col <= row, col >= row)
    tri_bs = tuple(jnp.where(t, 1.0, 0.0).astype(BF16) for t in tris)
    for d in range(2):
        z = _dot(lr_refs[d][...], wd_ref[d]) + bd_ref[d]
        la_ref[d] = jax.nn.log_sigmoid(z) * (1.0 / GLA_TAU)

    states = [[st_ref[d, h] for h in range(GLA_HEADS)] for d in range(2)]
    for c in range(n_chunk):
        for d in range(2):
            cc = c if d == 0 else n_chunk - 1 - c
            rows = slice(cc * GLA_CHUNK, (cc + 1) * GLA_CHUNK)
            la = la_ref[d, rows, :]
            la_hi, la_lo = _split_bf16(la)
            bc = _dot(tri_bs[d], la_hi) + _dot(tri_bs[d], la_lo)
            bend = jnp.sum(la, axis=0, keepdims=True)
            qk = qk_refs[d][rows, :].astype(F32)
            q = qk[:, 0:nk]
            k = qk[:, nk:2 * nk]
            v = v_refs[d][rows, :]
            qd = (q * (GLA_DK ** -0.5) * jnp.exp(bc)).astype(BF16)
            ki = (k * jnp.exp(-bc)).astype(BF16)
            kd = (k * jnp.exp(bend - bc)).astype(BF16)
            dec = jnp.exp(bend)
            for h in range(GLA_HEADS):
                ks = slice(h * GLA_DK, (h + 1) * GLA_DK)
                vs = slice(h * GLA_DV, (h + 1) * GLA_DV)
                vh = v[:, vs]
                sc = jnp.where(tris[d], _dot_nt(qd[:, ks], ki[:, ks]), 0.0)
                st = states[d][h]
                o = _dot(sc.astype(BF16), vh) + _dot_nt(qd[:, ks], st.astype(BF16))
                o_refs[d][rows, vs] = o
                states[d][h] = st * dec[:, ks] + _dot_tn(vh, kd[:, ks])
    for d in range(2):
        for h in range(GLA_HEADS):
            st_ref[d, h] = states[d][h]

    @pl.when(i == pl.num_programs(1) - 1)
    def _():
        sfin_ref[...] = st_ref[...]


def _gla(p, wd_pad, bd, s0):
    b, s, _ = p.shape
    tc = min(512, s)
    n_step = s // tc
    fwd = lambda col: (lambda b_, i: (b_, i, col))
    bwd = lambda col: (lambda b_, i: (b_, n_step - 1 - i, col))
    lr_col = 9 * P_BLK // 128
    state_spec = pl.BlockSpec((None, 2, GLA_HEADS, GLA_DV, GLA_DK), lambda b_, i: (b_, 0, 0, 0, 0))
    return pl.pallas_call(
        _gla_kernel,
        grid=(b, n_step),
        in_specs=[pl.BlockSpec((None, tc, P_BLK), fwd(4)),
                  pl.BlockSpec((None, tc, P_BLK), fwd(5)),
                  pl.BlockSpec((None, tc, 128), fwd(lr_col)),
                  pl.BlockSpec((None, tc, P_BLK), bwd(4)),
                  pl.BlockSpec((None, tc, P_BLK), bwd(5)),
                  pl.BlockSpec((None, tc, 128), bwd(lr_col)),
                  pl.BlockSpec((2, 128, 256), lambda b_, i: (0, 0, 0)),
                  pl.BlockSpec((2, 1, 256), lambda b_, i: (0, 0, 0)),
                  state_spec],
        out_specs=[pl.BlockSpec((None, tc, 512), fwd(0)),
                   pl.BlockSpec((None, tc, 512), bwd(0)),
                   state_spec],
        out_shape=[jax.ShapeDtypeStruct((b, s, 512), F32),
                   jax.ShapeDtypeStruct((b, s, 512), F32),
                   jax.ShapeDtypeStruct((b, 2, GLA_HEADS, GLA_DV, GLA_DK), F32)],
        scratch_shapes=[pltpu.VMEM((2, tc, 256), F32), pltpu.VMEM((2, GLA_HEADS, GLA_DV, GLA_DK), F32)],
        compiler_params=_params("arbitrary", "arbitrary"),
        name="gla",
    )(p, p, p, p, p, p, wd_pad, bd, s0)


def _smlp_kernel(u_ref, v_ref, g_ref, ws_ref, bs_ref, o_ref):
    n_chunk = u_ref.shape[0] // SMLP_CHUNK
    g = g_ref[...]
    for c in range(n_chunk):
        rs = slice(c * SMLP_CHUNK, (c + 1) * SMLP_CHUNK)
        vb = _rms(jax.nn.gelu(v_ref[rs, :].astype(F32)), g).astype(BF16)
        u = jax.nn.gelu(u_ref[rs, :].astype(F32))
        for k in range(SMLP_GROUPS):
            cs = slice(k * SMLP_GROUP_DIM, (k + 1) * SMLP_GROUP_DIM)
            mixed = _dot(ws_ref[k], vb[:, cs]) + bs_ref[:, cs]
            o_ref[rs, cs] = (u[:, cs] * mixed).astype(BF16)


def _smlp(p, v_norm, ws, bs_full):
    b, s, _ = p.shape
    tc = min(512, s)
    return pl.pallas_call(
        _smlp_kernel,
        grid=(b, s // tc),
        in_specs=[pl.BlockSpec((None, tc, P_BLK), lambda b_, i: (b_, i, 7)),
                  pl.BlockSpec((None, tc, P_BLK), lambda b_, i: (b_, i, 8)),
                  pl.BlockSpec((1, 512), lambda b_, i: (0, 0)),
                  pl.BlockSpec((SMLP_GROUPS, SMLP_CHUNK, SMLP_CHUNK), lambda b_, i: (0, 0, 0)),
                  pl.BlockSpec((SMLP_CHUNK, 512), lambda b_, i: (0, 0))],
        out_specs=pl.BlockSpec((None, tc, 512), lambda b_, i: (b_, i, 0)),
        out_shape=jax.ShapeDtypeStruct((b, s, 512), BF16),
        compiler_params=_params("arbitrary", "arbitrary"),
        name="smlp",
    )(p, p, v_norm, ws, bs_full)


def _merge_kernel(h_ref, oa_ref, om_ref, gf_ref, gb_ref, gr_ref, os_ref, gn_ref,
                  wg_ref, bg_ref, wb_ref, y_ref, og_ref):
    @pl.when(pl.program_id(2) == 0)
    def _():
        o = gf_ref[...] + gb_ref[...]
        r = gr_ref[...].astype(F32)
        gate = r * jax.nn.sigmoid(r)
        for hh in range(GLA_HEADS):
            vs = slice(hh * GLA_DV, (hh + 1) * GLA_DV)
            og_ref[:, vs] = (_rms(o[:, vs], gn_ref[...]) * gate[:, vs]).astype(BF16)

    h = h_ref[...]
    branches = (oa_ref[...], om_ref[...], og_ref[...], os_ref[...])
    y = None
    for n, o in enumerate(branches):
        gate = jax.nn.sigmoid(_dot(h, wg_ref[n]) + bg_ref[n])
        t = gate * _dot(o, wb_ref[n])
        y = t if y is None else y + t
    y_ref[...] = y.astype(BF16)


def _merge(h, o_a, o_m, o_gf, o_gb, p, o_s, gla_norm, wg, bg, wb):
    b, s, d = h.shape
    tm = min(512, s)
    tn = 512
    row = lambda b_, i, j: (b_, i, 0)
    return pl.pallas_call(
        _merge_kernel,
        grid=(b, s // tm, d // tn),
        in_specs=[pl.BlockSpec((None, tm, d), row),
                  pl.BlockSpec((None, tm, 512), row),
                  pl.BlockSpec((None, tm, 512), row),
                  pl.BlockSpec((None, tm, 512), row),
                  pl.BlockSpec((None, tm, 512), row),
                  pl.BlockSpec((None, tm, P_BLK), lambda b_, i, j: (b_, i, 6)),
                  pl.BlockSpec((None, tm, 512), row),
                  pl.BlockSpec((1, GLA_DV), lambda b_, i, j: (0, 0)),
                  pl.BlockSpec((4, d, tn), lambda b_, i, j: (0, 0, j)),
                  pl.BlockSpec((4, 1, tn), lambda b_, i, j: (0, 0, j)),
                  pl.BlockSpec((4, 512, tn), lambda b_, i, j: (0, 0, j))],
        out_specs=pl.BlockSpec((None, tm, tn), lambda b_, i, j: (b_, i, j)),
        out_shape=jax.ShapeDtypeStruct((b, s, d), BF16),
        scratch_shapes=[pltpu.VMEM((tm, 512), BF16)],
        compiler_params=_params("arbitrary", "arbitrary", "arbitrary"),
        name="merge",
    )(h, o_a, o_m, o_gf, o_gb, p, o_s, gla_norm, wg, bg, wb)


def _outproj_kernel(y_ref, x_ref, mod_ref, g_ref, wo_ref, wr_ref, br_ref, xo_ref, h2_ref, lg_ref):
    w_hi, w_lo = _split_bf16(wr_ref[...])
    w_cat = jnp.concatenate([w_hi, w_lo], axis=1)
    tm = y_ref.shape[0]
    n_part = 2 if tm % 16 == 0 else 1
    rows = [slice(n * tm // n_part, (n + 1) * tm // n_part) for n in range(n_part)]
    z_next = _dot(y_ref[rows[0], :], wo_ref[...])
    for n, rs in enumerate(rows):
        z = z_next
        if n + 1 < n_part:
            z_next = _dot(y_ref[rows[n + 1], :], wo_ref[...])
        xn = x_ref[rs, :] + mod_ref[2:3, :] * _rms(z, g_ref[1:2, :])
        xo_ref[rs, :] = xn
        h2 = _rms(xn, g_ref[2:3, :]) * (1.0 + mod_ref[4:5, :]) + mod_ref[3:4, :]
        half = h2.shape[1] // 2
        h2_ref[rs, :] = _pack_bf16_pair(h2[:, :half], h2[:, half:])
        h_hi, h_lo = _split_bf16(h2)
        hh = _dot(h_hi, w_cat)
        lg_ref[rs, :] = hh[:, :LANE] + hh[:, LANE:] + _dot(h_lo, w_hi) + br_ref[...]


def _outproj(y, x, mod, norm_g, w_out, wr_pad, br_pad):
    b, s, d = x.shape
    tm = min(256, s)
    row = lambda b_, i: (b_, i, 0)
    return pl.pallas_call(
        _outproj_kernel,
        grid=(b, s // tm),
        in_specs=[pl.BlockSpec((None, tm, d), row),
                  pl.BlockSpec((None, tm, d), row),
                  pl.BlockSpec((None, 8, d), lambda b_, i: (b_, 0, 0)),
                  pl.BlockSpec((4, d), lambda b_, i: (0, 0)),
                  pl.BlockSpec((d, d), lambda b_, i: (0, 0)),
                  pl.BlockSpec((d, LANE), lambda b_, i: (0, 0)),
                  pl.BlockSpec((1, LANE), lambda b_, i: (0, 0))],
        out_specs=[pl.BlockSpec((None, tm, d), row),
                   pl.BlockSpec((None, tm, d // 2), row),
                   pl.BlockSpec((None, tm, LANE), row)],
        out_shape=[jax.ShapeDtypeStruct((b, s, d), F32),
                   jax.ShapeDtypeStruct((b, s, d // 2), jnp.uint32),
                   jax.ShapeDtypeStruct((b, s, LANE), F32)],
        compiler_params=_params("arbitrary", "arbitrary"),
        name="outproj",
    )(y, x, mod, norm_g, w_out, wr_pad, br_pad)


def _route_kernel(lg_ref, idx_ref, w_ref, rank_ref, cnt_ref, carry_ref):
    i = pl.program_id(0)

    @pl.when(i == 0)
    def _():
        carry_ref[...] = jnp.zeros(carry_ref.shape, F32)

    lg = lg_ref[...]
    tm = lg.shape[0]
    lane = lax.broadcasted_iota(I32, lg.shape, 1).astype(F32)
    vals, idxs = [], []
    onehot = jnp.zeros(lg.shape, F32)
    for _ in range(TOP_K):
        m = jnp.max(lg, axis=-1, keepdims=True)
        sel = jnp.min(jnp.where(lg == m, lane, float(LANE)), axis=-1, keepdims=True)
        hit = lane == sel
        vals.append(m)
        idxs.append(sel)
        onehot = jnp.where(hit, 1.0, onehot)
        lg = jnp.where(hit, NEG_BIG, lg)
    es = [jnp.exp(v - vals[0]) for v in vals]
    den = es[0] + es[1] + es[2] + es[3]
    r = lax.broadcasted_iota(I32, (tm, tm), 0)
    c = lax.broadcasted_iota(I32, (tm, tm), 1)
    lower = (c < r).astype(F32).astype(BF16)
    before = _dot(lower, onehot.astype(BF16)) + carry_ref[...]
    for k in range(TOP_K):
        idx_ref[:, k:k + 1] = idxs[k].astype(I32)
        w_ref[:, k:k + 1] = es[k] / den
        rank_ref[:, k:k + 1] = jnp.sum(jnp.where(lane == idxs[k], before, 0.0), axis=-1,
                                       keepdims=True).astype(I32)
    carry_ref[...] = carry_ref[...] + jnp.sum(onehot, axis=0, keepdims=True)
    cnt_ref[...] = carry_ref[...]


def _route(logits):
    t = logits.shape[0]
    tm = 256
    return pl.pallas_call(
        _route_kernel,
        grid=(t // tm,),
        in_specs=[pl.BlockSpec((tm, LANE), lambda i: (i, 0))],
        out_specs=[pl.BlockSpec((tm, TOP_K), lambda i: (i, 0)),
                   pl.BlockSpec((tm, TOP_K), lambda i: (i, 0)),
                   pl.BlockSpec((tm, TOP_K), lambda i: (i, 0)),
                   pl.BlockSpec((1, LANE), lambda i: (0, 0))],
        out_shape=[jax.ShapeDtypeStruct((t, TOP_K), I32),
                   jax.ShapeDtypeStruct((t, TOP_K), F32),
                   jax.ShapeDtypeStruct((t, TOP_K), I32),
                   jax.ShapeDtypeStruct((1, LANE), F32)],
        scratch_shapes=[pltpu.VMEM((1, LANE), F32)],
        compiler_params=_params("arbitrary"),
        name="route",
    )(logits)


def _dispatch_kernel(pad_ref, dest_hbm, src_ref, *rest, zero_pads):
    xs_out, dest_smem, zero_ref, idx_sem, sem = rest[-5:]
    i = pl.program_id(0)
    tm = src_ref.shape[0]
    cp = pltpu.make_async_copy(dest_hbm.at[i], dest_smem, idx_sem)
    cp.start()

    if zero_pads:
        @pl.when(i == 0)
        def _():
            zero_ref[...] = jnp.zeros(zero_ref.shape, zero_ref.dtype)

            def pad_copy(e, r):
                return pltpu.make_async_copy(zero_ref.at[pl.ds(0, 1)],
                                             xs_out.at[pl.ds(pad_ref[e] + r, 1)], sem)

            def start_pads(e, carry):
                def body(r, c):
                    pad_copy(e, r).start()
                    return c
                return lax.fori_loop(0, pad_ref[N_EXPERTS + e], body, carry)

            def wait_pads(e, carry):
                def body(r, c):
                    pad_copy(e, r).wait()
                    return c
                return lax.fori_loop(0, pad_ref[N_EXPERTS + e], body, carry)

            lax.fori_loop(0, N_EXPERTS, start_pads, 0)
            lax.fori_loop(0, N_EXPERTS, wait_pads, 0)

    cp.wait()

    def row_copy(t, k):
        return pltpu.make_async_copy(src_ref.at[pl.ds(t, 1)],
                                     xs_out.at[pl.ds(dest_smem[t * TOP_K + k], 1)], sem)

    def issue(t):
        for k in range(TOP_K):
            row_copy(t, k).start()

    def drain(t):
        for k in range(TOP_K):
            row_copy(0, k).wait()

    _for_each_row(tm, issue)
    _for_each_row(tm, drain)


def _dispatch(pads, dest, src, xs, n_rows):
    t, d = src.shape
    tm = min(2048, t)
    dest2 = dest.reshape(t // tm, tm * TOP_K)
    first = xs is None
    args = (pads, dest2, src) + (() if first else (xs,))
    return pl.pallas_call(
        functools.partial(_dispatch_kernel, zero_pads=first),
        grid=(t // tm,),
        in_specs=[pl.BlockSpec(memory_space=pltpu.SMEM),
                  pl.BlockSpec(memory_space=pl.ANY),
                  pl.BlockSpec((tm, d), lambda i: (i, 0))] + ([] if first else [pl.BlockSpec(memory_space=pl.ANY)]),
        out_specs=pl.BlockSpec(memory_space=pl.ANY),
        out_shape=jax.ShapeDtypeStruct((n_rows, d), src.dtype),
        scratch_shapes=[pltpu.SMEM((tm * TOP_K,), I32), pltpu.VMEM((8, d), src.dtype),
                        pltpu.SemaphoreType.DMA, pltpu.SemaphoreType.DMA],
        input_output_aliases={} if first else {3: 0},
        compiler_params=_params("arbitrary"),
        name="dispatch",
    )(*args)


def _expert_kernel(te_ref, nu_ref, x_ref, w1_ref, b1_ref, w2_ref, b2_ref, y_ref):
    @pl.when(pl.program_id(0) < nu_ref[0])
    def _():
        x_a, x_b = _unpack_bf16_pair(x_ref[...])
        half = x_a.shape[1]
        z = (_dot(x_a.astype(BF16), w1_ref[:half, :]) + _dot(x_b.astype(BF16), w1_ref[half:, :])
             + b1_ref[...])
        glu = jnp.minimum(z[:, :D_EXPERT], SWIGLU_LIMIT)
        lin = jnp.clip(z[:, D_EXPERT:], -SWIGLU_LIMIT, SWIGLU_LIMIT)
        act = glu * jax.nn.sigmoid(SWIGLU_ALPHA * glu) * (lin + 1.0)
        y = _dot(act.astype(BF16), w2_ref[...]) + b2_ref[...]
        y_ref[...] = _pack_bf16_pair(y[:, :half], y[:, half:])


def _experts(layer, tile_e, n_used, xs, w1, b1, w2, b2):
    n_rows, half = xs.shape
    d = 2 * half
    tr = EXPERT_TILE
    n_tiles = n_rows // tr
    row = lambda i, te, nu: (jnp.minimum(i, nu[0] - 1), 0)
    wsel = lambda i, te, nu: (layer, te[i], 0, 0)
    return pl.pallas_call(
        _expert_kernel,
        grid_spec=pltpu.PrefetchScalarGridSpec(
            num_scalar_prefetch=2,
            grid=(n_tiles,),
            in_specs=[pl.BlockSpec((tr, half), row),
                      pl.BlockSpec((None, None, d, 2 * D_EXPERT), wsel),
                      pl.BlockSpec((None, None, 1, 2 * D_EXPERT), wsel),
                      pl.BlockSpec((None, None, D_EXPERT, d), wsel),
                      pl.BlockSpec((None, None, 1, d), wsel)],
            out_specs=pl.BlockSpec((tr, half), row)),
        out_shape=jax.ShapeDtypeStruct((n_rows, half), jnp.uint32),
        compiler_params=_params("arbitrary"),
        name="experts",
    )(tile_e, n_used, xs, w1, b1, w2, b2)


def _combine_kernel(dest_hbm, w_ref, x_ref, mod_ref, g_ref, ys_hbm, o_ref,
                    dest0, dest1, buf0, buf1, idx_sem, sem0, sem1):
    dest_smem, buf, sem = (dest0, dest1), (buf0, buf1), (sem0, sem1)
    tm = x_ref.shape[0]
    n_step = pl.num_programs(0) * pl.num_programs(1)
    i = pl.program_id(0) * pl.num_programs(1) + pl.program_id(1)

    def row_copy(sl, t, k):
        return pltpu.make_async_copy(ys_hbm.at[pl.ds(dest_smem[sl][t * TOP_K + k], 1)],
                                     buf[sl].at[k, pl.ds(t, 1)], sem[sl])

    def start_gather(step, sl):
        cp = pltpu.make_async_copy(dest_hbm.at[step], dest_smem[sl], idx_sem)
        cp.start()
        cp.wait()

        def issue(t):
            for k in range(TOP_K):
                row_copy(sl, t, k).start()

        _for_each_row(tm, issue)

    @pl.when(i == 0)
    def _():
        start_gather(0, 0)

    def step(slot):
        @pl.when(i + 1 < n_step)
        def _():
            start_gather(i + 1, 1 - slot)

        def drain(t):
            for k in range(TOP_K):
                row_copy(slot, 0, k).wait()

        _for_each_row(tm, drain)
        w = w_ref[...]
        f_a, f_b = None, None
        for k in range(TOP_K):
            y_a, y_b = _unpack_bf16_pair(buf[slot][k])
            f_a = w[:, k:k + 1] * y_a if f_a is None else f_a + w[:, k:k + 1] * y_a
            f_b = w[:, k:k + 1] * y_b if f_b is None else f_b + w[:, k:k + 1] * y_b
        half = f_a.shape[1]
        ms = (jnp.sum(f_a * f_a, axis=-1, keepdims=True)
              + jnp.sum(f_b * f_b, axis=-1, keepdims=True)) * (1.0 / (2 * half))
        inv = lax.rsqrt(ms + EPS)
        o_ref[:, :half] = x_ref[:, :half] + mod_ref[5:6, :half] * (f_a * inv * g_ref[3:4, :half])
        o_ref[:, half:] = x_ref[:, half:] + mod_ref[5:6, half:] * (f_b * inv * g_ref[3:4, half:])

    for slot in range(2):
        pl.when(lax.rem(i, 2) == slot)(functools.partial(step, slot))


def _combine(dest, w, x, mod, norm_g, ys):
    b, s, d = x.shape
    tm = min(512, s)
    ns = s // tm
    dest2 = dest.reshape(b * ns, tm * TOP_K)
    return pl.pallas_call(
        _combine_kernel,
        grid=(b, ns),
        in_specs=[pl.BlockSpec(memory_space=pl.ANY),
                  pl.BlockSpec((None, tm, TOP_K), lambda b_, i: (b_, i, 0)),
                  pl.BlockSpec((None, tm, d), lambda b_, i: (b_, i, 0)),
                  pl.BlockSpec((None, 8, d), lambda b_, i: (b_, 0, 0)),
                  pl.BlockSpec((4, d), lambda b_, i: (0, 0)),
                  pl.BlockSpec(memory_space=pl.ANY)],
        out_specs=pl.BlockSpec((None, tm, d), lambda b_, i: (b_, i, 0)),
        out_shape=jax.ShapeDtypeStruct((b, s, d), F32),
        scratch_shapes=[pltpu.SMEM((tm * TOP_K,), I32), pltpu.SMEM((tm * TOP_K,), I32),
                        pltpu.VMEM((TOP_K, tm, d // 2), jnp.uint32),
                        pltpu.VMEM((TOP_K, tm, d // 2), jnp.uint32),
                        pltpu.SemaphoreType.DMA, pltpu.SemaphoreType.DMA, pltpu.SemaphoreType.DMA],
        compiler_params=_params("arbitrary", "arbitrary"),
        name="combine",
    )(dest2, w.reshape(b, s, TOP_K), x, mod, norm_g, ys)


def _rot_half_cols(w, head_dim):
    d_in, n = w.shape
    w3 = w.reshape(d_in, n // head_dim, head_dim)
    half = head_dim // 2
    return jnp.concatenate([-w3[..., half:], w3[..., :half]], axis=-1).reshape(d_in, n)


def _extended_in_weights(w_in):
    sizes = (512, 128, 128, MLA_Q_RANK, MLA_KV_RANK, MLA_ROPE_DIM, 256, 256, 512,
             2 * GLA_DECAY_RANK, 512, 512, 512)
    cols, off = [], 0
    for n in sizes:
        cols.append(w_in[:, off:off + n])
        off += n
    (a_q, a_k, a_v, m_cq, m_ckv, m_kr, g_q, g_k, g_v, g_lr, g_r, s_u, s_v) = cols
    pad = jnp.zeros((w_in.shape[0], P_BLK - 2 * GLA_DECAY_RANK), w_in.dtype)
    ext = jnp.concatenate(
        [a_q, _rot_half_cols(a_q, SWA_HEAD_DIM),
         a_k, _rot_half_cols(a_k, SWA_HEAD_DIM), a_v, m_kr, _rot_half_cols(m_kr, MLA_ROPE_DIM),
         m_cq, m_ckv, g_q, g_k, g_v, g_r, s_u, s_v, g_lr, pad], axis=1)
    return ext.astype(BF16)


def _rope_tables(n_tok):
    n_rows = n_tok // GRID_W
    rows = jnp.repeat(jnp.arange(n_rows, dtype=F32), GRID_W)
    cols = jnp.tile(jnp.arange(GRID_W, dtype=F32), n_rows)
    n_freq = SWA_HEAD_DIM // 4
    inv_freq = ROPE_BASE ** (-jnp.arange(n_freq, dtype=F32) / n_freq)
    ang = jnp.concatenate([rows[:, None] * inv_freq, cols[:, None] * inv_freq], axis=-1)
    cos = jnp.concatenate([jnp.cos(ang), jnp.cos(ang)], axis=-1)
    sin = jnp.concatenate([jnp.sin(ang), jnp.sin(ang)], axis=-1)
    return jnp.tile(cos, (1, 8)), jnp.tile(sin, (1, 8))


def _moe(layer, h2_parts, logits_parts, w1, b1, w2, b2):
    logits = jnp.concatenate(logits_parts, axis=0)
    t_all = logits.shape[0]
    top_idx, top_w, rank, counts = _route(logits)
    counts = counts[0, :N_EXPERTS].astype(I32)
    tr = EXPERT_TILE
    padded = (counts + tr - 1) // tr * tr
    p_end = jnp.cumsum(padded)
    p_start = p_end - padded
    dest = p_start[top_idx] + rank
    n_tiles = -(-t_all * TOP_K // tr) + N_EXPERTS
    tile_start = jnp.arange(n_tiles, dtype=I32) * tr
    tile_e = jnp.minimum(jnp.sum((p_end[None, :] <= tile_start[:, None]).astype(I32), axis=1),
                         N_EXPERTS - 1)
    n_used = (p_end[-1:] // tr).astype(I32)
    pads = jnp.concatenate([p_start + counts, padded - counts]).astype(I32)
    xs = None
    off = 0
    for part in h2_parts:
        n = part.shape[0]
        xs = _dispatch(pads, dest[off:off + n], part, xs, n_tiles * tr)
        off += n
    ys = _experts(layer, tile_e, n_used, xs, w1, b1, w2, b2)
    return dest, top_w, ys


def kernel(x, c, ctx, c_ctx, ada_w, ada_b, norm_g, w_in, attn_sink, mla_q_norm, mla_w_uq, mla_kv_norm, mla_w_ukv, gla_w_decay, gla_b_decay, gla_out_norm, smlp_v_norm, smlp_w_spatial, smlp_b_spatial, w_branch, w_gate, b_gate, w_out, router_w, router_b, expert_w1, expert_b1, expert_w2, expert_b2):
    bsz, seq, d = x.shape
    n_ctx = ctx.shape[1]
    depth = ada_w.shape[0]

    cs = jnp.zeros((8, d), F32).at[:bsz].set(c).at[bsz].set(c_ctx)
    mod_all = _ada(cs, ada_w, ada_b).reshape(depth, 8, N_MOD, d)
    mod_all = jnp.pad(mod_all, ((0, 0), (0, 0), (0, 8 - N_MOD), (0, 0)))

    cos_t, sin_t = _rope_tables(seq)
    cos_c = jnp.ones((bsz * n_ctx, 512), F32)
    sin_c = jnp.zeros((bsz * n_ctx, 512), F32)
    ctx = ctx.reshape(1, bsz * n_ctx, d)

    w1 = expert_w1.astype(BF16)
    b1 = expert_b1.reshape(depth, N_EXPERTS, 1, 2 * D_EXPERT)
    w2 = expert_w2.astype(BF16)
    b2 = expert_b2.reshape(depth, N_EXPERTS, 1, d)

    for l in range(depth):
        last = l == depth - 1
        mod = mod_all[l, :bsz]
        mod_c = mod_all[l, bsz:bsz + 1]
        g_l = norm_g[l]
        w_ext = _extended_in_weights(w_in[l])
        uq = mla_w_uq[l].reshape(MLA_Q_RANK, MLA_HEADS, MLA_NOPE_DIM + MLA_ROPE_DIM)
        uq_rope = uq[:, :, MLA_NOPE_DIM:].reshape(MLA_Q_RANK, MLA_HEADS * MLA_ROPE_DIM)
        wuq_ext = jnp.concatenate(
            [uq[:, :, :MLA_NOPE_DIM].reshape(MLA_Q_RANK, MLA_HEADS * MLA_NOPE_DIM), uq_rope,
             _rot_half_cols(uq_rope, MLA_ROPE_DIM)], axis=1).astype(BF16)
        ukv = mla_w_ukv[l].reshape(MLA_KV_RANK, MLA_HEADS, MLA_NOPE_DIM + MLA_V_DIM)
        wkn = jnp.transpose(ukv[:, :, :MLA_NOPE_DIM], (1, 0, 2)).astype(BF16)
        wv = jnp.transpose(ukv[:, :, MLA_NOPE_DIM:], (1, 0, 2)).astype(BF16)
        qn_g = mla_q_norm[l].reshape(1, MLA_Q_RANK)
        kv_g = mla_kv_norm[l].reshape(1, MLA_KV_RANK)
        wd_pad = jnp.zeros((2, 128, 256), F32)
        for dd in range(2):
            wd_pad = wd_pad.at[dd, dd * GLA_DECAY_RANK:(dd + 1) * GLA_DECAY_RANK].set(gla_w_decay[l, dd])
        wd_pad = wd_pad.astype(BF16)
        bd = gla_b_decay[l].reshape(2, 1, 256)
        gla_norm = gla_out_norm[l].reshape(1, GLA_DV)
        v_norm = smlp_v_norm[l].reshape(1, 512)
        ws = smlp_w_spatial[l].astype(BF16)
        bs_full = jnp.repeat(smlp_b_spatial[l].T, SMLP_GROUP_DIM, axis=1)
        wg = w_gate[l].astype(BF16)
        bg = b_gate[l].reshape(4, 1, d)
        wb = w_branch[l].astype(BF16)
        wo = w_out[l].astype(BF16)
        wr_pad = jnp.pad(router_w[l], ((0, 0), (0, LANE - N_EXPERTS)))
        br_pad = jnp.pad(router_b[l].reshape(1, N_EXPERTS), ((0, 0), (0, LANE - N_EXPERTS)),
                         constant_values=NEG_BIG)
        sink = attn_sink[l]

        h, p = _inproj(x, mod, g_l, w_ext)
        hc, pc1 = _inproj(ctx, mod_c, g_l, w_ext)
        pc = pc1.reshape(bsz, n_ctx, P_WIDTH)
        ka, kcat, q_m = _prep(p, cos_t, sin_t, qn_g, kv_g, wuq_ext, wkn)
        _, kcat_c, q_mc = _prep(pc1, cos_c, sin_c, qn_g, kv_g, wuq_ext, wkn)
        kcat_c = kcat_c.reshape(bsz, n_ctx, MLA_QK_PAD)

        o_a = _swa(sink, p, ka, cos_t, sin_t, pc)
        o_m = _mla(q_m, jnp.concatenate([kcat, kcat_c], axis=1), wv)
        s0 = jnp.zeros((bsz, 2, GLA_HEADS, GLA_DV, GLA_DK), F32)
        ogf_c, ogb_c, s_ctx = _gla(pc, wd_pad, bd, s0)
        ogf, ogb, _ = _gla(p, wd_pad, bd, s_ctx)
        o_s = _smlp(p, v_norm, ws, bs_full)
        y = _merge(h, o_a, o_m, ogf, ogb, p, o_s, gla_norm, wg, bg, wb)
        x, h2, lg = _outproj(y, x, mod, g_l, wo, wr_pad, br_pad)
        h2_parts = [h2.reshape(bsz * seq, d // 2)]
        lg_parts = [lg.reshape(bsz * seq, LANE)]

        if not last:
            oc_a = _swa_ctx(sink, pc)
            q_mc = q_mc.reshape(MLA_HEADS, bsz, n_ctx, MLA_QK_PAD).transpose(1, 0, 2, 3)
            oc_m = _mla(q_mc, kcat_c, wv)
            oc_s = _smlp(pc, v_norm, ws, bs_full)
            flat = lambda a: a.reshape(1, bsz * n_ctx, a.shape[-1])
            yc = _merge(hc, flat(oc_a), flat(oc_m), flat(ogf_c), flat(ogb_c), pc1,
                        flat(oc_s), gla_norm, wg, bg, wb)
            ctx, h2c, lgc = _outproj(yc, ctx, mod_c, g_l, wo, wr_pad, br_pad)
            h2_parts.append(h2c.reshape(bsz * n_ctx, d // 2))
            lg_parts.append(lgc.reshape(bsz * n_ctx, LANE))

        dest, top_w, ys = _moe(l, h2_parts, lg_parts, w1, b1, w2, b2)
        n_lat = bsz * seq
        x = _combine(dest[:n_lat], top_w[:n_lat], x, mod, g_l, ys)
        if not last:
            ctx = _combine(dest[n_lat:], top_w[n_lat:], ctx, mod_c, g_l, ys)
    return x
```
